```python
import jax, jax.numpy as jnp
from jax import lax
import numpy as np

D_MODEL = 1024
BATCH = 8
SEQ = 4096
DEPTH = 1
DEC_BATCH = 32
DEC_SEQ = 1
PAST_LEN = 16384
PAGE_SIZE = 128

MIX_WIDTH = D_MODEL
A_WIDTH = MIX_WIDTH // 2
A_GROUPS = 4
A_GROUP_DIM = A_WIDTH // A_GROUPS
CHUNK = 128
B_WIDTH = MIX_WIDTH - A_WIDTH
HEAD_DIM = 64
N_HEADS = B_WIDTH // HEAD_DIM
N_KV = 2
Q_PER_KV = N_HEADS // N_KV
KV_W = N_KV * HEAD_DIM
ROT_DIM = HEAD_DIM // 4
ROPE_THETA = 500000.0
CMP_LEN = 32
CMP_STRIDE = 16
CMP_HIDDEN = 2 * HEAD_DIM
SLC_BLOCK = 64
N_SELECT = 16
WINDOW = 512
Q_BLOCK = 128
FORCE_BONUS = 1000.0
D_FF = -(-8 * D_MODEL // (3 * 256)) * 256
NORM_EPS = 1e-6
IN_WIDTH = 2 * A_WIDTH + B_WIDTH + 6 * KV_W + 3 * N_HEADS

kernel_name = "hymba_gmlp_nsa_decode_step"


def rms_norm(x, g):
    xf = x.astype(jnp.float32)
    y = xf * lax.rsqrt(jnp.mean(xf * xf, axis=-1, keepdims=True) + NORM_EPS)
    return (y * g.astype(jnp.float32)).astype(x.dtype)


def masked_softmax(s, mask):
    s = jnp.where(mask, s.astype(jnp.float32), -1e30)
    m = jnp.max(s, axis=-1, keepdims=True)
    e = jnp.where(mask, jnp.exp(s - m), 0.0)
    return e / jnp.maximum(jnp.sum(e, axis=-1, keepdims=True), 1e-30)


def rope_partial(x, pos):
    half = ROT_DIM // 2
    inv = ROPE_THETA ** (-jnp.arange(half, dtype=jnp.float32) / half)
    ang = pos.astype(jnp.float32)[:, None] * inv[None, :]
    shp = (ang.shape[0],) + (1,) * (x.ndim - 3) + (half,)
    cos = jnp.cos(ang).reshape(shp)
    sin = jnp.sin(ang).reshape(shp)
    xr = x[..., :ROT_DIM].astype(jnp.float32)
    x1, x2 = xr[..., :half], xr[..., half:]
    rot = jnp.concatenate([x1 * cos - x2 * sin, x2 * cos + x1 * sin], axis=-1)
    return jnp.concatenate([rot.astype(x.dtype), x[..., ROT_DIM:]], axis=-1)


def layer_inputs(x, pos, g_attn, w_in):
    B_, T = x.shape[:2]
    z = jnp.einsum('btd,dp->btp', rms_norm(x, g_attn), w_in)
    o1 = 2 * A_WIDTH
    o2 = o1 + B_WIDTH
    o3 = o2 + 6 * KV_W
    u = z[..., :A_WIDTH]
    v = z[..., A_WIDTH:o1]
    q = rope_partial(z[..., o1:o2].reshape(B_, T, N_HEADS, HEAD_DIM), pos)
    kv = z[..., o2:o3].reshape(B_, T, 3, 2, N_KV, HEAD_DIM)
    k = rope_partial(kv[:, :, :, 0], pos)
    kv = jnp.stack([k, kv[:, :, :, 1]], axis=3)
    gates = z[..., o3:].reshape(B_, T, N_HEADS, 3)
    return u, v, q, kv[:, :, 0], kv[:, :, 1], kv[:, :, 2], gates


def gmlp_mix(u, v, g_sgu, w_s, b_s):
    B_, T = u.shape[:2]
    u = jax.nn.gelu(u)
    vg = rms_norm(jax.nn.gelu(v).reshape(B_, T, A_GROUPS, A_GROUP_DIM),
                  g_sgu.reshape(A_GROUPS, A_GROUP_DIM))
    Tp = -(-T // CHUNK) * CHUNK
    vp = jnp.pad(vg, ((0, 0), (0, Tp - T), (0, 0), (0, 0)))
    vp = vp.reshape(B_, Tp // CHUNK, CHUNK, A_GROUPS, A_GROUP_DIM)
    ws = jnp.where(jnp.tril(jnp.ones((CHUNK, CHUNK), dtype=bool)), w_s, 0)
    s = jnp.einsum('gij,bcjgd->bcigd', ws, vp) + b_s.T[None, None, :, :, None]
    s = s.reshape(B_, Tp, A_WIDTH)[:, :T]
    return u * s, vg.reshape(B_, T, A_WIDTH)


def compress(kv, w_c1, b_c1, w_c2):
    B_, L = kv.shape[:2]
    sub = kv.reshape(B_, L // CMP_STRIDE, CMP_STRIDE, 2, N_KV, HEAD_DIM)
    w1 = w_c1.reshape(2, CMP_LEN // CMP_STRIDE, CMP_STRIDE, HEAD_DIM, CMP_HIDDEN)
    first = jnp.einsum('bnscgd,csdh->bncgh', sub, w1[:, 0])
    second = jnp.einsum('bnscgd,csdh->bncgh', sub, w1[:, 1])
    h = jax.nn.silu(first[:, :-1] + second[:, 1:] + b_c1[:, None, :])
    return jnp.einsum('bncgh,che->bncge', h, w_c2)


def nsa_core(q, gates, pos, kc, cend, gather_slc, kw, wpos):
    T = q.shape[0]
    NC = kc.shape[0]
    NS = (NC + 1) * CMP_STRIDE // SLC_BLOCK
    n_sel = min(N_SELECT, NS)
    scale = HEAD_DIM ** -0.5
    qg = q.reshape(T, N_KV, Q_PER_KV, HEAD_DIM)
    s = jnp.einsum('tgqd,ngd->tgqn', qg, kc[:, 0]) * scale
    p_cmp = masked_softmax(s, (cend[None, :] <= pos[:, None])[:, None, None, :])
    o_cmp = jnp.einsum('tgqn,ngd->tgqd', p_cmp.astype(q.dtype), kc[:, 1])
    ci = jnp.arange(NC)[:, None]
    sj = jnp.arange(NS)[None, :]
    overlap = ((ci * CMP_STRIDE < (sj + 1) * SLC_BLOCK) &
               (ci * CMP_STRIDE + CMP_LEN > sj * SLC_BLOCK)).astype(jnp.float32)
    p_slc = jnp.einsum('tgqn,ns->tgs', p_cmp, overlap)
    blk_t = pos // SLC_BLOCK
    causal = sj <= blk_t[:, None]
    forced = causal & ((sj == 0) | (sj >= blk_t[:, None] - 1))
    score = jnp.where(forced[:, None, :], p_slc + FORCE_BONUS, p_slc)
    score = jnp.where(causal[:, None, :], score, -1e30)
    vals, idx = lax.top_k(score, n_sel)
    sel_ok = vals > -1e29
    kv_sel = gather_slc(idx)
    kpos = idx[..., None] * SLC_BLOCK + jnp.arange(SLC_BLOCK)
    msel = sel_ok[..., None] & (kpos <= pos[:, None, None, None])
    s = jnp.einsum('tgqd,tgnsd->tgqns', qg, kv_sel[..., 0, :]) * scale
    p = masked_softmax(s.reshape(T, N_KV, Q_PER_KV, -1), msel.reshape(T, N_KV, 1, -1))
    o_slc = jnp.einsum('tgqk,tgkd->tgqd', p.astype(q.dtype),
                       kv_sel[..., 1, :].reshape(T, N_KV, -1, HEAD_DIM))
    s = jnp.einsum('tgqd,kgd->tgqk', qg, kw[:, 0]) * scale
    dpos = pos[:, None] - wpos[None, :]
    mw = (dpos >= 0) & (dpos < WINDOW) & (wpos[None, :] >= 0)
    p = masked_softmax(s, mw[:, None, None, :])
    o_win = jnp.einsum('tgqk,kgd->tgqd', p.astype(q.dtype), kw[:, 1])
    g = jax.nn.sigmoid(gates.astype(jnp.float32)).reshape(T, N_KV, Q_PER_KV, 3)
    o = g[..., 0:1] * o_cmp + g[..., 1:2] * o_slc + g[..., 2:3] * o_win
    return o.reshape(T, B_WIDTH).astype(q.dtype)


def nsa_prompt(q, gates, kv_cmp, kv_slc, kv_win, w_c1, b_c1, w_c2):
    B_, T = q.shape[:2]
    kc = compress(kv_cmp, w_c1, b_c1, w_c2)
    NC = kc.shape[1]
    cend = jnp.arange(NC) * CMP_STRIDE + CMP_LEN - 1
    NS = T // SLC_BLOCK
    slc_blocks = kv_slc.reshape(B_, NS, SLC_BLOCK, 2, N_KV, HEAD_DIM).transpose(0, 4, 1, 2, 3, 5)
    kvw_pad = jnp.pad(kv_win, ((0, 0), (WINDOW, 0), (0, 0), (0, 0), (0, 0)))
    nb = T // Q_BLOCK
    gidx = jnp.arange(N_KV)[None, :, None]

    def block_fn(args):
        b, blk = args
        start = blk * Q_BLOCK
        pos = start + jnp.arange(Q_BLOCK, dtype=jnp.int32)
        qb = lax.dynamic_slice_in_dim(q[b], start, Q_BLOCK, 0)
        gb = lax.dynamic_slice_in_dim(gates[b], start, Q_BLOCK, 0)
        kw = lax.dynamic_slice_in_dim(kvw_pad[b], start, Q_BLOCK + WINDOW, 0)
        wpos = start - WINDOW + jnp.arange(Q_BLOCK + WINDOW, dtype=jnp.int32)
        sb = slc_blocks[b]
        return nsa_core(qb, gb, pos, kc[b], cend, lambda idx: sb[gidx, idx], kw, wpos)

    bi = jnp.repeat(jnp.arange(B_, dtype=jnp.int32), nb)
    ki = jnp.tile(jnp.arange(nb, dtype=jnp.int32), B_)
    out = lax.map(block_fn, (bi, ki))
    return out.reshape(B_, T, B_WIDTH)


def nsa_sample(q, gates, pos, new_cmp, new_slc, new_win, cache_cmp, cache_slc, win_buf, page_table,
               w_c1, b_c1, w_c2):
    DB, T = q.shape[:2]
    past = page_table.shape[1] * PAGE_SIZE
    L = past + T
    Lp = -(-L // SLC_BLOCK) * SLC_BLOCK
    past_cmp = cache_cmp[page_table].reshape(DB, past, 2, N_KV, HEAD_DIM)
    full_cmp = jnp.concatenate(
        [past_cmp, new_cmp, jnp.zeros((DB, Lp - L, 2, N_KV, HEAD_DIM), new_cmp.dtype)], axis=1)
    kc = compress(full_cmp, w_c1, b_c1, w_c2)
    NC = kc.shape[1]
    cend = jnp.arange(NC) * CMP_STRIDE + CMP_LEN - 1
    sub_per_page = PAGE_SIZE // SLC_BLOCK
    pool_sub = cache_slc.reshape(-1, SLC_BLOCK, 2, N_KV, HEAD_DIM)
    npb = past // SLC_BLOCK
    nnb = Lp // SLC_BLOCK - npb
    new_blocks = jnp.pad(new_slc, ((0, 0), (0, nnb * SLC_BLOCK - T), (0, 0), (0, 0), (0, 0)))
    new_blocks = new_blocks.reshape(DB, nnb, SLC_BLOCK, 2, N_KV, HEAD_DIM)
    kw = jnp.concatenate([win_buf, new_win], axis=1)
    wb = win_buf.shape[1]
    wpos = past - wb + jnp.arange(wb + T, dtype=jnp.int32)
    gidx = jnp.arange(N_KV)[None, :, None]

    def seq_fn(qb, gb, kcb, ptb, nbb, kwb):
        def gather(idx):
            jp = jnp.minimum(idx, npb - 1)
            phys = ptb[jp // sub_per_page] * sub_per_page + jp % sub_per_page
            from_pool = pool_sub[phys, :, :, gidx]
            from_new = nbb[jnp.clip(idx - npb, 0, nnb - 1), :, :, gidx]
            return jnp.where((idx < npb)[..., None, None, None], from_pool, from_new)
        return nsa_core(qb, gb, pos, kcb, cend, gather, kwb, wpos)

    out = jax.vmap(seq_fn)(q, gates, kc, page_table, new_blocks, kw)
    return out, kw[:, kw.shape[1] - wb:]


def layer_output(x, o_a, o_b, g_out_a, g_out_b, w_out, g_ffn, w_gate_up, w_down):
    mix = jnp.concatenate([rms_norm(o_a, g_out_a), rms_norm(o_b, g_out_b)], axis=-1)
    x = x + jnp.einsum('btm,md->btd', mix, w_out)
    gu = jnp.einsum('btd,df->btf', rms_norm(x, g_ffn), w_gate_up)
    hid = jax.nn.silu(gu[..., :D_FF]) * gu[..., D_FF:]
    return x + jnp.einsum('btf,fd->btd', hid, w_down)


def setup_inputs(seed: int = 0) -> dict:
    key = jax.random.key(seed)
    ks = jax.random.split(key, 24)
    n_pages = PAST_LEN // PAGE_SIZE
    n_used = DEC_BATCH * n_pages
    n_pool = (n_used * 5 + 3) // 4
    wb = min(WINDOW, PAST_LEN)
    nrm = lambda k, shp, sc: jax.random.normal(k, shp, jnp.float32) * sc
    gain = lambda k, shp: 1.0 + 0.02 * jax.random.normal(k, shp, jnp.float32)
    page_table = jax.random.permutation(ks[5], n_pool)[:n_used].reshape(DEC_BATCH, n_pages).astype(jnp.int32)
    return {
        "x_prompt": nrm(ks[0], (BATCH, SEQ, D_MODEL), 1.0),
        "x_sample": nrm(ks[1], (DEC_BATCH, DEC_SEQ, D_MODEL), 1.0),
        "cache_cmp_kv": nrm(ks[2], (DEPTH, n_pool, PAGE_SIZE, 2, N_KV, HEAD_DIM), 1.0),
        "cache_slc_kv": nrm(ks[3], (DEPTH, n_pool, PAGE_SIZE, 2, N_KV, HEAD_DIM), 1.0),
        "state_win_kv": nrm(ks[4], (DEPTH, DEC_BATCH, wb, 2, N_KV, HEAD_DIM), 1.0),
        "page_table": page_table,
        "g_attn": gain(ks[6], (DEPTH, D_MODEL)),
        "w_in": nrm(ks[7], (DEPTH, D_MODEL, IN_WIDTH), D_MODEL ** -0.5),
        "g_sgu": gain(ks[8], (DEPTH, A_WIDTH)),
        "w_s": nrm(ks[9], (DEPTH, A_GROUPS, CHUNK, CHUNK), CHUNK ** -0.5),
        "b_s": gain(ks[10], (DEPTH, A_GROUPS, CHUNK)),
        "w_c1": nrm(ks[11], (DEPTH, 2, CMP_LEN, HEAD_DIM, CMP_HIDDEN), (CMP_LEN * HEAD_DIM) ** -0.5),
        "b_c1": nrm(ks[12], (DEPTH, 2, CMP_HIDDEN), 0.02),
        "w_c2": nrm(ks[13], (DEPTH, 2, CMP_HIDDEN, HEAD_DIM), CMP_HIDDEN ** -0.5),
        "g_out_a": gain(ks[14], (DEPTH, A_WIDTH)),
        "g_out_b": gain(ks[15], (DEPTH, B_WIDTH)),
        "w_out": nrm(ks[16], (DEPTH, MIX_WIDTH, D_MODEL), MIX_WIDTH ** -0.5),
        "g_ffn": gain(ks[17], (DEPTH, D_MODEL)),
        "w_gate_up": nrm(ks[18], (DEPTH, D_MODEL, 2 * D_FF), D_MODEL ** -0.5),
        "w_down": nrm(ks[19], (DEPTH, D_FF, D_MODEL), D_FF ** -0.5),
        "g_final": gain(ks[20], (D_MODEL,)),
    }


def reference(x_prompt, x_sample, cache_cmp_kv, cache_slc_kv, state_win_kv, page_table,
              g_attn, w_in, g_sgu, w_s, b_s, w_c1, b_c1, w_c2, g_out_a, g_out_b, w_out,
              g_ffn, w_gate_up, w_down, g_final):
    t_p = x_prompt.shape[1]
    t_s = x_sample.shape[1]
    past = page_table.shape[1] * PAGE_SIZE
    pos_p = jnp.arange(t_p, dtype=jnp.int32)
    pos_s = past + jnp.arange(t_s, dtype=jnp.int32)
    hp, hs = x_prompt, x_sample
    p_cmp, p_slc, p_win, s_cmp, s_slc, s_win, s_v = [], [], [], [], [], [], []
    for l in range(DEPTH):
        u, v, q, kvc, kvs, kvw, gt = layer_inputs(hp, pos_p, g_attn[l], w_in[l])
        o_a, _ = gmlp_mix(u, v, g_sgu[l], w_s[l], b_s[l])
        o_b = nsa_prompt(q, gt, kvc, kvs, kvw, w_c1[l], b_c1[l], w_c2[l])
        hp = layer_output(hp, o_a, o_b, g_out_a[l], g_out_b[l], w_out[l], g_ffn[l], w_gate_up[l], w_down[l])
        p_cmp.append(kvc)
        p_slc.append(kvs)
        p_win.append(kvw[:, t_p - min(WINDOW, t_p):])
        u, v, q, kvc, kvs, kvw, gt = layer_inputs(hs, pos_s, g_attn[l], w_in[l])
        o_a, v_rows = gmlp_mix(u, v, g_sgu[l], w_s[l], b_s[l])
        o_b, new_win = nsa_sample(q, gt, pos_s, kvc, kvs, kvw, cache_cmp_kv[l], cache_slc_kv[l],
                                  state_win_kv[l], page_table, w_c1[l], b_c1[l], w_c2[l])
        hs = layer_output(hs, o_a, o_b, g_out_a[l], g_out_b[l], w_out[l], g_ffn[l], w_gate_up[l], w_down[l])
        s_cmp.append(kvc)
        s_slc.append(kvs)
        s_win.append(new_win)
        s_v.append(v_rows)
    y_prompt = rms_norm(hp, g_final)
    y_sample = rms_norm(hs, g_final)
    return (y_prompt, y_sample, jnp.stack(p_cmp), jnp.stack(p_slc), jnp.stack(p_win),
            jnp.stack(s_cmp), jnp.stack(s_slc), jnp.stack(s_win), jnp.stack(s_v))
```

```python
import functools

import jax
import jax.numpy as jnp
from jax import lax
from jax.experimental import pallas as pl
from jax.experimental.pallas import tpu as pltpu

F32 = jnp.float32
BF16 = jnp.bfloat16

A_GROUPS = 4
CHUNK = 128
HEAD_DIM = 64
N_KV = 2
ROT_DIM = HEAD_DIM // 4
ROPE_THETA = 500000.0
CMP_LEN = 32
CMP_STRIDE = 16
SLC_BLOCK = 64
N_SELECT = 16
WINDOW = 512
Q_BLOCK = 128
FORCE_BONUS = 1000.0
PAGE_SIZE = 128
NORM_EPS = 1e-6
MASKED = -1e30

LANES = 128
SUBLANES = 8
VMEM_LIMIT_BYTES = 56 * 1024 * 1024

KV_LANES = 2 * N_KV * HEAD_DIM
K_LANES = N_KV * HEAD_DIM
SLC_TILE = 512
WIN_KEYS = WINDOW + Q_BLOCK


def _rms(x, g):
    return x * lax.rsqrt(jnp.mean(x * x, axis=-1, keepdims=True) + NORM_EPS) * g


def _dot(a, b):
    return jnp.dot(a, b, preferred_element_type=F32)


def _dot_nt(a, b, precision=None):
    return lax.dot_general(a, b, (((1,), (1,)), ((), ())), precision=precision,
                           preferred_element_type=F32)


def _rope(z, rc, rs1, rs2):
    return z * rc + pltpu.roll(z, LANES - ROT_DIM // 2, 1) * rs1 + pltpu.roll(z, ROT_DIM // 2, 1) * rs2


def _project(x_ref, gattn_ref, wuv_ref, wq_ref, wkv_ref, wg_ref, rc_ref, rs1_ref, rs2_ref):
    xn = _rms(x_ref[...], gattn_ref[...]).astype(BF16)
    rc, rs1, rs2 = rc_ref[...], rs1_ref[...], rs2_ref[...]
    zuv = _dot(xn, wuv_ref[...])
    a_width = zuv.shape[1] // 2
    zq = _dot(xn, wq_ref[...])
    q = jnp.concatenate([_rope(zq[:, i * LANES:(i + 1) * LANES], rc, rs1, rs2)
                         for i in range(zq.shape[1] // LANES)], axis=1)
    zkv = _dot(xn, wkv_ref[...])
    branches = []
    for br in range(3):
        k = _rope(zkv[:, br * KV_LANES:br * KV_LANES + K_LANES], rc, rs1, rs2)
        v = zkv[:, br * KV_LANES + K_LANES:(br + 1) * KV_LANES]
        branches.append((k, v))
    gates = jax.nn.sigmoid(_dot(xn, wg_ref[...]))
    return zuv[:, :a_width], zuv[:, a_width:], q, branches, gates


def _gmlp_norm_v(v, gsgu):
    v = jax.nn.gelu(v)
    return jnp.concatenate([_rms(v[:, g * LANES:(g + 1) * LANES], gsgu[:, g * LANES:(g + 1) * LANES])
                            for g in range(A_GROUPS)], axis=1)


def _inproj_prompt_kernel(x_ref, gattn_ref, wuv_ref, wq_ref, wkv_ref, wg_ref, rc_ref, rs1_ref, rs2_ref,
                          ws_ref, bs_ref, gsgu_ref, goa_ref,
                          mixa_ref, q_ref, kvc_ref, kvs_ref, kvw_ref, gate_ref,
                          kts_ref, vs_ref, ktw_ref, vw_ref):
    u, v, q, branches, gates = _project(x_ref, gattn_ref, wuv_ref, wq_ref, wkv_ref, wg_ref,
                                        rc_ref, rs1_ref, rs2_ref)
    tm = u.shape[0]
    u = jax.nn.gelu(u)
    vg = _gmlp_norm_v(v, gsgu_ref[...]).astype(BF16)
    row = lax.broadcasted_iota(jnp.int32, (CHUNK, CHUNK), 0)
    col = lax.broadcasted_iota(jnp.int32, (CHUNK, CHUNK), 1)
    bias = bs_ref[...]
    parts = []
    for g in range(A_GROUPS):
        w = jnp.where(row >= col, ws_ref[g], 0.0).astype(BF16)
        s = jnp.concatenate(
            [_dot(w, vg[c * CHUNK:(c + 1) * CHUNK, g * LANES:(g + 1) * LANES]) for c in range(tm // CHUNK)],
            axis=0)
        s = s + jnp.concatenate([bias[:, g * LANES:(g + 1) * LANES]] * (tm // CHUNK), axis=0)
        parts.append(u[:, g * LANES:(g + 1) * LANES] * s)
    mixa_ref[...] = _rms(jnp.concatenate(parts, axis=1), goa_ref[...]).astype(BF16)
    q_ref[...] = q
    for ref, (k, v_) in zip((kvc_ref, kvs_ref, kvw_ref), branches):
        ref[...] = jnp.concatenate([k, v_], axis=1)
    gate_ref[...] = gates
    (ks, vs), (kw, vw) = branches[1], branches[2]
    kts_ref[0] = ks.T.astype(BF16)
    vs_ref[...] = vs.astype(BF16)
    ktw_ref[0] = kw.T.astype(BF16)
    vw_ref[...] = vw.astype(BF16)


def _inproj_sample_kernel(x_ref, gattn_ref, wuv_ref, wq_ref, wkv_ref, wg_ref, rc_ref, rs1_ref, rs2_ref,
                          ws0_ref, bs0_ref, gsgu_ref, goa_ref,
                          mixa_ref, q_ref, kvc_ref, kvs_ref, kvw_ref, gate_ref, vrow_ref):
    u, v, q, branches, gates = _project(x_ref, gattn_ref, wuv_ref, wq_ref, wkv_ref, wg_ref,
                                        rc_ref, rs1_ref, rs2_ref)
    vg = _gmlp_norm_v(v, gsgu_ref[...])
    o_a = jax.nn.gelu(u) * (vg * ws0_ref[...] + bs0_ref[...])
    mixa_ref[...] = _rms(o_a, goa_ref[...]).astype(BF16)
    vrow_ref[...] = vg
    q_ref[...] = q
    for ref, (k, v_) in zip((kvc_ref, kvs_ref, kvw_ref), branches):
        ref[...] = jnp.concatenate([k, v_], axis=1)
    gate_ref[...] = gates


def _full(shape):
    return pl.BlockSpec(shape, lambda *_: (0,) * len(shape))


def _params(*sem):
    return pltpu.CompilerParams(dimension_semantics=sem, vmem_limit_bytes=VMEM_LIMIT_BYTES)


def _in_projection(x2, pos, w, *, prompt_shape=None):
    n, d = x2.shape
    rc, rs1, rs2 = _rope_tables(pos)
    a_width = w["wuv"].shape[1] // 2
    b_width = w["wq"].shape[1]
    weights = [w["g_attn"], w["wuv"], w["wq"], w["wkv"], w["wg"]]
    wspecs = [_full(a.shape) for a in weights]
    tail = [w["g_sgu"], w["g_out_a"]]
    if prompt_shape is None:
        tm, grid = n, (1,)
        row = lambda i: (i, 0)
        rope_map = row
        gm = [w["ws0"], w["bs0"]]
    else:
        b, t = prompt_shape
        tm = 512
        assert t % tm == 0 and tm % CHUNK == 0
        tpb = t // tm
        grid = (b * tpb,)
        row = lambda i: (i, 0)
        rope_map = lambda i: (i % tpb, 0)
        gm = [w["w_s"], w["bs_full"]]
    rspec = pl.BlockSpec((tm, LANES), rope_map)
    in_specs = ([pl.BlockSpec((tm, d), row)] + wspecs + [rspec] * 3
                + [_full(a.shape) for a in gm] + [_full(a.shape) for a in tail])
    outs = [((n, a_width), BF16), ((n, b_width), F32), ((n, KV_LANES), F32), ((n, KV_LANES), F32),
            ((n, KV_LANES), F32), ((n, LANES), F32)]
    out_specs = [pl.BlockSpec((tm, s[1]), row) for s, _ in outs]
    if prompt_shape is None:
        kern = _inproj_sample_kernel
        outs.append(((n, a_width), F32))
        out_specs.append(pl.BlockSpec((tm, a_width), row))
    else:
        kern = _inproj_prompt_kernel
        kt_map = lambda i: (i // tpb, 0, i % tpb)
        outs += [((b, K_LANES, t), BF16), ((n, K_LANES), BF16), ((b, K_LANES, t), BF16), ((n, K_LANES), BF16)]
        out_specs += [pl.BlockSpec((1, K_LANES, tm), kt_map), pl.BlockSpec((tm, K_LANES), row),
                      pl.BlockSpec((1, K_LANES, tm), kt_map), pl.BlockSpec((tm, K_LANES), row)]
    return pl.pallas_call(
        kern, grid=grid, in_specs=in_specs, out_specs=out_specs,
        out_shape=[jax.ShapeDtypeStruct(s, dt) for s, dt in outs],
        compiler_params=_params("parallel"), name="in_projection",
    )(x2, *weights, rc, rs1, rs2, *gm, *tail)


def _rope_tables(pos):
    half = ROT_DIM // 2
    inv = ROPE_THETA ** (-jnp.arange(half, dtype=F32) / half)
    ang = pos.astype(F32)[:, None] * inv[None, :]
    cos, sin = jnp.cos(ang), jnp.sin(ang)
    n = pos.shape[0]
    rest0 = jnp.zeros((n, HEAD_DIM - ROT_DIM), F32)
    zero = jnp.zeros((n, half), F32)
    rc = jnp.concatenate([cos, cos, rest0 + 1.0], axis=1)
    rs1 = jnp.concatenate([-sin, zero, rest0], axis=1)
    rs2 = jnp.concatenate([zero, sin, rest0], axis=1)
    return tuple(jnp.tile(a, (1, LANES // HEAD_DIM)) for a in (rc, rs1, rs2))


def _compress_partial(read_k, read_v, w1k_ref, w1v_ref):
    acc_k = acc_v = None
    for s in range(CMP_STRIDE):
        pk = _dot(read_k(s).astype(BF16), w1k_ref[s])
        pv = _dot(read_v(s).astype(BF16), w1v_ref[s])
        acc_k = pk if acc_k is None else acc_k + pk
        acc_v = pv if acc_v is None else acc_v + pv
    return acc_k, acc_v


def _compress_finish(fs_k, fs_v, b1_ref, w2_ref):
    hid = b1_ref.shape[1]
    outs = []
    for c, fs in enumerate((fs_k, fs_v)):
        hs = []
        for g in range(N_KV):
            first = fs[:, g * 2 * hid:g * 2 * hid + hid]
            second = fs[:, g * 2 * hid + hid:(g + 1) * 2 * hid]
            nxt = pltpu.roll(second, second.shape[0] - 1, 0)
            hs.append(jax.nn.silu(first + nxt + b1_ref[c:c + 1, :]))
        outs.append(_dot(jnp.concatenate(hs, axis=1).astype(BF16), w2_ref[c]))
    return jnp.concatenate(outs, axis=1)


def _compress_prompt_kernel(k_ref, v_ref, w1k_ref, w1v_ref, b1_ref, w2_ref, kc_ref):
    nb = kc_ref.shape[1]
    fs_k, fs_v = _compress_partial(lambda s: k_ref[0, pl.ds(s, nb, stride=CMP_STRIDE), :],
                                   lambda s: v_ref[0, pl.ds(s, nb, stride=CMP_STRIDE), :], w1k_ref, w1v_ref)
    kc_ref[0] = _compress_finish(fs_k, fs_v, b1_ref, w2_ref)


def _compress_prompt(kvc, w):
    b, t, _ = kvc.shape
    nb = t // CMP_STRIDE
    weights = [w["w1k"], w["w1v"], w["b_c1"], w["w2"]]
    return pl.pallas_call(
        _compress_prompt_kernel, grid=(b,),
        in_specs=[pl.BlockSpec((1, t, K_LANES), lambda i: (i, 0, 0)), pl.BlockSpec((1, t, K_LANES), lambda i: (i, 0, 1))]
        + [_full(a.shape) for a in weights],
        out_specs=pl.BlockSpec((1, nb, KV_LANES), lambda i: (i, 0, 0)),
        out_shape=jax.ShapeDtypeStruct((b, nb, KV_LANES), F32),
        compiler_params=_params("parallel"), name="compress_prompt",
    )(kvc, kvc, *weights)


def _stack_heads(q, g, q_per_kv):
    return jnp.concatenate([q[:, (g * q_per_kv + h) * HEAD_DIM:(g * q_per_kv + h + 1) * HEAD_DIM]
                            for h in range(q_per_kv)], axis=0)


def _cmp_branch(qg, kc, mask, reps):
    k, v = kc
    s = _dot_nt(qg, k, precision=lax.Precision.HIGHEST)
    if mask.shape[0] > 1:
        mask = jnp.concatenate([mask.astype(F32)] * reps, axis=0) > 0.5
    s = jnp.where(mask, s, MASKED)
    m = jnp.max(s, axis=-1, keepdims=True)
    e = jnp.where(mask, jnp.exp(s - m), 0.0)
    p = e / jnp.maximum(jnp.sum(e, axis=-1, keepdims=True), 1e-30)
    return p, _dot(p.astype(BF16), v.astype(BF16))


def _select_blocks(score, n_sel):
    rows, n = score.shape
    lane = lax.broadcasted_iota(jnp.int32, (rows, n), 1).astype(F32)
    sel = jnp.zeros((rows, n), F32)
    picks = []
    x = score
    for _ in range(n_sel):
        m = jnp.max(x, axis=-1, keepdims=True)
        idx = jnp.min(jnp.where(x == m, lane, float(n)), axis=-1, keepdims=True)
        hit = lane == idx
        ok = m > 0.1 * MASKED
        sel = jnp.where(hit & ok, 1.0, sel)
        x = jnp.where(hit, -3e38, x)
        picks.append((idx, ok))
    return sel, picks


def _block_scores(p_slc, blk_t, n_blocks):
    sj = lax.broadcasted_iota(jnp.int32, p_slc.shape, 1)
    causal = (sj <= blk_t) & (sj < n_blocks)
    forced = causal & ((sj == 0) | (sj >= blk_t - 1))
    score = jnp.where(forced, p_slc + FORCE_BONUS, p_slc)
    return jnp.where(causal, score, MASKED)


def _nsa_prompt_kernel(q_ref, gate_ref, kc_ref, kts_ref, vs_ref, ktw_ref, vw_ref, ovl_ref, exp_ref, gob_ref,
                       out_ref, m_scr, l_scr, acc_scr, *, n_cmp, n_heads):
    blk = pl.program_id(1)
    start = blk * Q_BLOCK
    q_per_kv = n_heads // N_KV
    q = q_ref[...] * (HEAD_DIM ** -0.5)
    gates = gate_ref[...]
    kc = kc_ref[0]
    ncp = kc.shape[0]
    n_blocks = ovl_ref.shape[1]
    tpos = start + lax.broadcasted_iota(jnp.int32, (Q_BLOCK, 1), 0)
    cn = lax.broadcasted_iota(jnp.int32, (1, ncp), 1)
    cmp_mask = (cn * CMP_STRIDE + CMP_LEN - 1 <= tpos) & (cn < n_cmp)
    ws = pl.multiple_of(jnp.maximum(start - WINDOW, 0), LANES)
    wpos = ws + lax.broadcasted_iota(jnp.int32, (1, WIN_KEYS), 1)
    dpos = tpos - wpos
    win_bias = jnp.where((dpos >= 0) & (dpos < WINDOW), 0.0, MASKED)
    win_bias = jnp.concatenate([win_bias] * q_per_kv, axis=0)
    n_tiles = blk // (SLC_TILE // Q_BLOCK) + 1
    heads = []
    for g in range(N_KV):
        rows = slice(g * HEAD_DIM, (g + 1) * HEAD_DIM)
        qg = _stack_heads(q, g, q_per_kv)
        qb = qg.astype(BF16)
        p_cmp, o_cmp = _cmp_branch(qg, (kc[:, rows], kc[:, K_LANES + g * HEAD_DIM:K_LANES + (g + 1) * HEAD_DIM]),
                                   cmp_mask, q_per_kv)
        p_sum = p_cmp[:Q_BLOCK]
        for h in range(1, q_per_kv):
            p_sum = p_sum + p_cmp[h * Q_BLOCK:(h + 1) * Q_BLOCK]
        p_slc = jnp.dot(p_sum, ovl_ref[...], precision=lax.Precision.HIGHEST, preferred_element_type=F32)
        score = _block_scores(p_slc, tpos // SLC_BLOCK, n_blocks)
        sel, _ = _select_blocks(score, min(N_SELECT, n_blocks))
        selb = sel.astype(BF16)
        m_scr[...] = jnp.full(m_scr.shape, MASKED, F32)
        l_scr[...] = jnp.zeros(l_scr.shape, F32)
        acc_scr[...] = jnp.zeros(acc_scr.shape, F32)

        def tile_step(kt, carry, qb=qb, selb=selb, rows=rows):
            off = pl.multiple_of(kt * SLC_TILE, SLC_TILE)
            s = _dot(qb, kts_ref[0, rows, pl.ds(off, SLC_TILE)])
            chosen = _dot(selb, exp_ref[:, pl.ds(off, SLC_TILE)])
            kpos = off + lax.broadcasted_iota(jnp.int32, (1, SLC_TILE), 1)
            bias = jnp.where((chosen > 0.5) & (kpos <= tpos), 0.0, MASKED)
            s = s + jnp.concatenate([bias] * q_per_kv, axis=0)
            m_old = m_scr[...]
            m_new = jnp.maximum(m_old, jnp.max(s, axis=-1, keepdims=True))
            alpha = jnp.exp(m_old - m_new)
            p = jnp.exp(s - m_new)
            l_scr[...] = alpha * l_scr[...] + jnp.sum(p, axis=-1, keepdims=True)
            acc_scr[...] = alpha * acc_scr[...] + _dot(p.astype(BF16), vs_ref[pl.ds(off, SLC_TILE), :])
            m_scr[...] = m_new
            return carry

        lax.fori_loop(0, n_tiles, tile_step, 0)
        o_slc = (acc_scr[...] / l_scr[...])[:, rows]
        s = _dot(qb, ktw_ref[0, rows, pl.ds(ws, WIN_KEYS)]) + win_bias
        p = jnp.exp(s - jnp.max(s, axis=-1, keepdims=True))
        o_win = _dot(p.astype(BF16), vw_ref[pl.ds(ws, WIN_KEYS), :]) / jnp.sum(p, axis=-1, keepdims=True)
        o_win = o_win[:, rows]
        for h in range(q_per_kv):
            hd = g * q_per_kv + h
            r = slice(h * Q_BLOCK, (h + 1) * Q_BLOCK)
            heads.append(gates[:, 3 * hd:3 * hd + 1] * o_cmp[r] + gates[:, 3 * hd + 1:3 * hd + 2] * o_slc[r]
                         + gates[:, 3 * hd + 2:3 * hd + 3] * o_win[r])
    out_ref[...] = _rms(jnp.concatenate(heads, axis=1), gob_ref[...]).astype(BF16)


def _nsa_prompt(q, gates, kc, kts, vs, ktw, vw, w, b, t):
    n, b_width = q.shape
    n_heads = b_width // HEAD_DIM
    assert t % SLC_TILE == 0 and t >= WIN_KEYS
    nqb = t // Q_BLOCK
    ncp = kc.shape[1]
    n_cmp = ncp - 1
    n_blocks = (n_cmp + 1) * CMP_STRIDE // SLC_BLOCK
    ovl = _overlap_matrix(ncp, n_blocks, n_blocks)
    key_blk = jnp.arange(t, dtype=jnp.int32)[None, :] // SLC_BLOCK
    expand = (key_blk == jnp.arange(n_blocks, dtype=jnp.int32)[:, None]).astype(BF16)
    tok = lambda i, j: (i * nqb + j, 0)
    seq3 = lambda i, j: (i, 0, 0)
    seq2 = lambda i, j: (i, 0)
    rows = (n_heads // N_KV) * Q_BLOCK
    return pl.pallas_call(
        functools.partial(_nsa_prompt_kernel, n_cmp=n_cmp, n_heads=n_heads),
        grid=(b, nqb),
        in_specs=[pl.BlockSpec((Q_BLOCK, b_width), tok), pl.BlockSpec((Q_BLOCK, LANES), tok),
                  pl.BlockSpec((1, ncp, KV_LANES), seq3),
                  pl.BlockSpec((1, K_LANES, t), seq3), pl.BlockSpec((t, K_LANES), seq2),
                  pl.BlockSpec((1, K_LANES, t), seq3), pl.BlockSpec((t, K_LANES), seq2),
                  _full(ovl.shape), _full(expand.shape), _full(w["g_out_b"].shape)],
        out_specs=pl.BlockSpec((Q_BLOCK, b_width), tok),
        out_shape=jax.ShapeDtypeStruct((n, b_width), BF16),
        scratch_shapes=[pltpu.VMEM((rows, 1), F32), pltpu.VMEM((rows, 1), F32), pltpu.VMEM((rows, K_LANES), F32)],
        compiler_params=_params("parallel", "arbitrary"), name="nsa_prompt",
    )(q, gates, kc, kts, vs, ktw, vw, ovl, expand, w["g_out_b"])


def _overlap_matrix(rows, cols, n_blocks):
    ci = jnp.arange(rows, dtype=jnp.int32)[:, None]
    sj = jnp.arange(cols, dtype=jnp.int32)[None, :]
    hit = (ci * CMP_STRIDE < (sj + 1) * SLC_BLOCK) & (ci * CMP_STRIDE + CMP_LEN > sj * SLC_BLOCK) & (sj < n_blocks)
    return hit.astype(F32)


def _compress_sample_kernel(pt_ref, *refs, pages_per_step, n_steps):
    del pt_ref
    k_pages = refs[:pages_per_step]
    v_pages = refs[pages_per_step:2 * pages_per_step]
    new_ref, w1k_ref, w1v_ref, b1_ref, w2_ref, kc_ref, fsk_scr, fsv_scr = refs[2 * pages_per_step:]
    j = pl.program_id(1)
    blocks_per_page = PAGE_SIZE // CMP_STRIDE
    step_blocks = pages_per_step * blocks_per_page

    def reader(pages):
        return lambda s: jnp.concatenate(
            [p[0, pl.ds(s, blocks_per_page, stride=CMP_STRIDE), :] for p in pages], axis=0)

    fs_k, fs_v = _compress_partial(reader(k_pages), reader(v_pages), w1k_ref, w1v_ref)
    off = pl.multiple_of(j * step_blocks, step_blocks)
    fsk_scr[pl.ds(off, step_blocks), :] = fs_k
    fsv_scr[pl.ds(off, step_blocks), :] = fs_v

    @pl.when(j == n_steps - 1)
    def _():
        past_blocks = n_steps * step_blocks
        tail = fsk_scr.shape[0] - past_blocks
        new = new_ref[0]
        is_first = lax.broadcasted_iota(jnp.int32, (tail, 1), 0) == 0
        nk = _dot(new[:, :K_LANES].astype(BF16), w1k_ref[0])
        nv = _dot(new[:, K_LANES:].astype(BF16), w1v_ref[0])
        fsk_scr[pl.ds(past_blocks, tail), :] = jnp.where(is_first, nk, 0.0)
        fsv_scr[pl.ds(past_blocks, tail), :] = jnp.where(is_first, nv, 0.0)
        kc_ref[0] = _compress_finish(fsk_scr[...], fsv_scr[...], b1_ref, w2_ref)


def _compress_sample(cache_cmp, page_table, new_rows, w):
    db, n_pages = page_table.shape
    pages_per_step = 32
    assert n_pages % pages_per_step == 0
    n_steps = n_pages // pages_per_step
    blocks_per_page = PAGE_SIZE // CMP_STRIDE
    past_blocks = n_pages * blocks_per_page
    nbp = past_blocks + SUBLANES
    weights = [w["w1k"], w["w1v"], w["b_c1"], w["w2"]]
    hid2 = w["w1k"].shape[2]
    page_spec = lambda i, c: pl.BlockSpec((1, PAGE_SIZE, K_LANES),
                                          lambda b, j, pt, i=i, c=c: (pt[b, j * pages_per_step + i], 0, c))
    grid_spec = pltpu.PrefetchScalarGridSpec(
        num_scalar_prefetch=1, grid=(db, n_steps),
        in_specs=[page_spec(i, c) for c in range(2) for i in range(pages_per_step)]
        + [pl.BlockSpec((1, 1, KV_LANES), lambda b, j, pt: (b, 0, 0))]
        + [pl.BlockSpec(a.shape, lambda b, j, pt, nd=a.ndim: (0,) * nd) for a in weights],
        out_specs=pl.BlockSpec((1, nbp, KV_LANES), lambda b, j, pt: (b, 0, 0)),
        scratch_shapes=[pltpu.VMEM((nbp, hid2), F32), pltpu.VMEM((nbp, hid2), F32)])
    return pl.pallas_call(
        functools.partial(_compress_sample_kernel, pages_per_step=pages_per_step, n_steps=n_steps),
        grid_spec=grid_spec, out_shape=jax.ShapeDtypeStruct((db, nbp, KV_LANES), F32),
        compiler_params=_params("parallel", "arbitrary"), name="compress_sample",
    )(page_table, *([cache_cmp] * (2 * pages_per_step)), new_rows[:, None, :], *weights)


def _select_sample_kernel(q_ref, kc_ref, ovl_ref, ocmp_ref, idx_ref, *, pos, n_cmp, n_blocks, n_heads):
    q_per_kv = n_heads // N_KV
    q = q_ref[0] * (HEAD_DIM ** -0.5)
    kc = kc_ref[0]
    ncp = kc.shape[0]
    cn = lax.broadcasted_iota(jnp.int32, (1, ncp), 1)
    cmp_mask = (cn * CMP_STRIDE + CMP_LEN - 1 <= pos) & (cn < n_cmp)
    o_rows, scores = [], []
    for g in range(N_KV):
        qg = _stack_heads(q, g, q_per_kv)
        p_cmp, o_cmp = _cmp_branch(qg, (kc[:, g * HEAD_DIM:(g + 1) * HEAD_DIM],
                                        kc[:, K_LANES + g * HEAD_DIM:K_LANES + (g + 1) * HEAD_DIM]),
                                   cmp_mask, q_per_kv)
        o_rows.append(o_cmp)
        p_sum = jnp.sum(p_cmp, axis=0, keepdims=True)
        scores.append(jnp.dot(p_sum, ovl_ref[...], precision=lax.Precision.HIGHEST, preferred_element_type=F32))
    ocmp_ref[0] = jnp.concatenate(o_rows, axis=0)
    score = _block_scores(jnp.concatenate(scores, axis=0), pos // SLC_BLOCK, n_blocks)
    _, picks = _select_blocks(score, min(N_SELECT, n_blocks))
    lane = lax.broadcasted_iota(jnp.int32, (N_KV, LANES), 1)
    out = jnp.full((N_KV, LANES), -1.0, F32)
    for i, (idx, ok) in enumerate(picks):
        out = jnp.where((lane == i) & ok, idx, out)
    idx_ref[0] = out.astype(jnp.int32)


def _select_sample(q, kc, pos, n_cmp, n_blocks):
    db, b_width = q.shape
    n_heads = b_width // HEAD_DIM
    ncp = kc.shape[1]
    nsp = -(-n_blocks // LANES) * LANES
    ovl = _overlap_matrix(ncp, nsp, n_blocks)
    return pl.pallas_call(
        functools.partial(_select_sample_kernel, pos=pos, n_cmp=n_cmp, n_blocks=n_blocks, n_heads=n_heads),
        grid=(db,),
        in_specs=[pl.BlockSpec((1, 1, b_width), lambda i: (i, 0, 0)),
                  pl.BlockSpec((1, ncp, KV_LANES), lambda i: (i, 0, 0)), _full(ovl.shape)],
        out_specs=[pl.BlockSpec((1, n_heads, HEAD_DIM), lambda i: (i, 0, 0)),
                   pl.BlockSpec((1, N_KV, LANES), lambda i: (i, 0, 0))],
        out_shape=[jax.ShapeDtypeStruct((db, n_heads, HEAD_DIM), F32),
                   jax.ShapeDtypeStruct((db, N_KV, LANES), jnp.int32)],
        compiler_params=_params("parallel"), name="select_sample",
    )(q[:, None, :], kc, ovl)


def _attend_sample_kernel(phys_ref, *refs, n_sel, past_blocks, n_heads):
    del phys_ref
    n_slots = N_KV * n_sel
    blocks = refs[:n_slots]
    (q_ref, gate_ref, ocmp_ref, idx_ref, newslc_ref, win_ref, newwin_ref, exp_ref, gob_ref,
     out_ref, winout_ref) = refs[n_slots:]
    q_per_kv = n_heads // N_KV
    q = q_ref[0] * (HEAD_DIM ** -0.5)
    lane = lax.broadcasted_iota(jnp.int32, (1, K_LANES), 1)
    wb = win_ref.shape[1]
    is_last = lax.broadcasted_iota(jnp.int32, (wb, 1), 0) == wb - 1
    win = jnp.where(is_last, newwin_ref[0], pltpu.roll(win_ref[0], wb - 1, 0))
    winout_ref[0] = win
    new_slc = newslc_ref[0]
    idx = idx_ref[0]
    o_slc, o_win = [], []
    for g in range(N_KV):
        qpad = jnp.concatenate(
            [jnp.where(lane // HEAD_DIM == g,
                       jnp.concatenate([q[:, (g * q_per_kv + h) * HEAD_DIM:(g * q_per_kv + h + 1) * HEAD_DIM]] * N_KV,
                                       axis=1), 0.0)
             for h in range(q_per_kv)], axis=0)
        qpb = qpad.astype(BF16)
        kv = jnp.concatenate([blocks[g * n_sel + i][0] for i in range(n_sel)], axis=0).astype(BF16)
        s = _dot_nt(qpb, kv[:, :K_LANES])
        idg = idx[g:g + 1, :]
        slot_ok = ((idg >= 0) & (idg < past_blocks)).astype(BF16)
        key_ok = _dot(slot_ok, exp_ref[...]) > 0.5
        s = jnp.where(key_ok, s, MASKED)
        has_new = jnp.max(jnp.where(idg == past_blocks, 1.0, 0.0), axis=-1, keepdims=True) > 0.5
        s_new = jnp.sum(qpad * new_slc[:, :K_LANES], axis=-1, keepdims=True)
        s_new = jnp.where(has_new, s_new, MASKED)
        m = jnp.maximum(jnp.max(s, axis=-1, keepdims=True), s_new)
        e = jnp.where(key_ok, jnp.exp(s - m), 0.0)
        e_new = jnp.where(has_new, jnp.exp(s_new - m), 0.0)
        den = jnp.maximum(jnp.sum(e, axis=-1, keepdims=True) + e_new, 1e-30)
        o = (_dot(e.astype(BF16), kv[:, K_LANES:]) + e_new * new_slc[:, K_LANES:]) / den
        o_slc.append(o[:, g * HEAD_DIM:(g + 1) * HEAD_DIM])
        s = _dot_nt(qpb, win[:, :K_LANES].astype(BF16))
        e = jnp.exp(s - jnp.max(s, axis=-1, keepdims=True))
        o = _dot(e.astype(BF16), win[:, K_LANES:].astype(BF16)) / jnp.sum(e, axis=-1, keepdims=True)
        o_win.append(o[:, g * HEAD_DIM:(g + 1) * HEAD_DIM])
    gates = gate_ref[0]
    o = (gates[:, 0:1] * ocmp_ref[0] + gates[:, 1:2] * jnp.concatenate(o_slc, axis=0)
         + gates[:, 2:3] * jnp.concatenate(o_win, axis=0))
    ms = jnp.sum(jnp.sum(o * o, axis=-1, keepdims=True), axis=0, keepdims=True) / (n_heads * HEAD_DIM)
    out_ref[0] = o * lax.rsqrt(ms + NORM_EPS) * gob_ref[...]


def _attend_sample(q, gates, o_cmp, idx, cache_slc, page_table, new_slc, win_buf, new_win, w, past_blocks):
    db, b_width = q.shape
    n_heads = b_width // HEAD_DIM
    n_sel = min(N_SELECT, past_blocks + 1)
    sub_per_page = PAGE_SIZE // SLC_BLOCK
    pool_sub = cache_slc.reshape(-1, SLC_BLOCK, KV_LANES)
    picked = idx[:, :, :n_sel]
    jp = jnp.clip(picked, 0, past_blocks - 1)
    page = jnp.take_along_axis(page_table, (jp // sub_per_page).reshape(db, -1), axis=1).reshape(jp.shape)
    phys = (page * sub_per_page + jp % sub_per_page).reshape(db, N_KV * n_sel).astype(jnp.int32)
    n_keys = n_sel * SLC_BLOCK
    expand = (jnp.arange(n_keys, dtype=jnp.int32)[None, :] // SLC_BLOCK
              == jnp.arange(LANES, dtype=jnp.int32)[:, None]).astype(BF16)
    gob = w["g_out_b"].reshape(n_heads, HEAD_DIM)
    wb = win_buf.shape[1]
    per_seq = lambda shape: pl.BlockSpec((1,) + shape, lambda b, ph: (b, 0, 0))
    blk_spec = lambda i: pl.BlockSpec((1, SLC_BLOCK, KV_LANES), lambda b, ph, i=i: (ph[b, i], 0, 0))
    grid_spec = pltpu.PrefetchScalarGridSpec(
        num_scalar_prefetch=1, grid=(db,),
        in_specs=[blk_spec(i) for i in range(N_KV * n_sel)]
        + [per_seq((1, b_width)), per_seq((n_heads, 3)), per_seq((n_heads, HEAD_DIM)), per_seq((N_KV, LANES)),
           per_seq((1, KV_LANES)), per_seq((wb, KV_LANES)), per_seq((1, KV_LANES)),
           pl.BlockSpec(expand.shape, lambda b, ph: (0, 0)), pl.BlockSpec(gob.shape, lambda b, ph: (0, 0))],
        out_specs=[per_seq((n_heads, HEAD_DIM)), per_seq((wb, KV_LANES))])
    return pl.pallas_call(
        functools.partial(_attend_sample_kernel, n_sel=n_sel, past_blocks=past_blocks, n_heads=n_heads),
        grid_spec=grid_spec,
        out_shape=[jax.ShapeDtypeStruct((db, n_heads, HEAD_DIM), F32),
                   jax.ShapeDtypeStruct((db, wb, KV_LANES), F32)],
        compiler_params=_params("parallel"), name="attend_sample",
    )(phys, *([pool_sub] * (N_KV * n_sel)), q[:, None, :], gates[:, :n_heads * 3].reshape(db, n_heads, 3), o_cmp, idx,
      new_slc[:, None, :], win_buf, new_win[:, None, :], expand, gob)


def _ffn_kernel(x_ref, ma_ref, mb_ref, woa_ref, wob_ref, gffn_ref, wgate_ref, wup_ref, wdown_ref, gfin_ref, y_ref,
                acc_scr, xn_scr):
    c = pl.program_id(1)

    @pl.when(c == 0)
    def _():
        x = x_ref[...] + _dot(ma_ref[...], woa_ref[...]) + _dot(mb_ref[...], wob_ref[...])
        acc_scr[...] = x
        xn_scr[...] = _rms(x, gffn_ref[...]).astype(BF16)

    xn = xn_scr[...]
    hid = jax.nn.silu(_dot(xn, wgate_ref[...])) * _dot(xn, wup_ref[...])
    acc_scr[...] += _dot(hid.astype(BF16), wdown_ref[...])

    @pl.when(c == pl.num_programs(1) - 1)
    def _():
        y_ref[...] = _rms(acc_scr[...], gfin_ref[...])


def _output_ffn(x2, mix_a, mix_b, w, tm):
    n, d = x2.shape
    d_ff = w["w_down"].shape[0]
    ff_chunks = 2
    step = d_ff // ff_chunks
    assert n % tm == 0 and d_ff % ff_chunks == 0 and step % LANES == 0
    row = lambda i, c: (i, 0)
    fixed = lambda a: pl.BlockSpec(a.shape, lambda i, c: (0, 0))
    return pl.pallas_call(
        _ffn_kernel,
        grid=(n // tm, ff_chunks),
        in_specs=[pl.BlockSpec((tm, d), row), pl.BlockSpec((tm, mix_a.shape[1]), row),
                  pl.BlockSpec((tm, mix_b.shape[1]), row), fixed(w["wo_a"]), fixed(w["wo_b"]), fixed(w["g_ffn"]),
                  pl.BlockSpec((d, step), lambda i, c: (0, c)), pl.BlockSpec((d, step), lambda i, c: (0, c)),
                  pl.BlockSpec((step, d), lambda i, c: (c, 0)), fixed(w["g_final"])],
        out_specs=pl.BlockSpec((tm, d), row),
        out_shape=jax.ShapeDtypeStruct((n, d), F32),
        scratch_shapes=[pltpu.VMEM((tm, d), F32), pltpu.VMEM((tm, d), BF16)],
        compiler_params=_params("parallel", "arbitrary"), name="output_ffn",
    )(x2, mix_a, mix_b, w["wo_a"], w["wo_b"], w["g_ffn"], w["w_gate"], w["w_up"], w["w_down"], w["g_final"])


def _prepare_weights(l, g_attn, w_in, g_sgu, w_s, b_s, w_c1, b_c1, w_c2, g_out_a, g_out_b, w_out,
                     g_ffn, w_gate_up, w_down, g_final):
    a_width = g_sgu.shape[1]
    b_width = g_out_b.shape[1]
    n_heads = b_width // HEAD_DIM
    d_ff = w_down.shape[1]
    hid = b_c1.shape[2]
    o1 = 2 * a_width
    o2 = o1 + b_width
    o3 = o2 + 3 * KV_LANES
    wi = w_in[l]
    wg = jnp.pad(wi[:, o3:], ((0, 0), (0, LANES - 3 * n_heads)))
    row = lambda a: a.reshape(1, -1)
    w1 = w_c1[l].reshape(2, 2, CMP_STRIDE, HEAD_DIM, hid)
    w1 = jnp.transpose(w1, (0, 2, 3, 1, 4)).reshape(2, CMP_STRIDE, HEAD_DIM, 2 * hid)
    eye = jnp.eye(N_KV, dtype=F32)
    w1 = jnp.einsum("gh,csdn->csgdhn", eye, w1).reshape(2, CMP_STRIDE, K_LANES, N_KV * 2 * hid)
    w2 = jnp.einsum("gh,cne->cgnhe", eye, w_c2[l]).reshape(2, N_KV * hid, K_LANES)
    return {
        "g_attn": row(g_attn[l]), "wuv": wi[:, :o1].astype(BF16), "wq": wi[:, o1:o2].astype(BF16),
        "wkv": wi[:, o2:o3].astype(BF16), "wg": wg.astype(BF16),
        "g_sgu": row(g_sgu[l]), "g_out_a": row(g_out_a[l]), "g_out_b": row(g_out_b[l]),
        "w_s": w_s[l], "bs_full": jnp.repeat(b_s[l].T, LANES, axis=1),
        "ws0": row(jnp.repeat(w_s[l][:, 0, 0], LANES)), "bs0": row(jnp.repeat(b_s[l][:, 0], LANES)),
        "w1k": w1[0].astype(BF16), "w1v": w1[1].astype(BF16), "b_c1": b_c1[l], "w2": w2.astype(BF16),
        "wo_a": w_out[l][:a_width].astype(BF16), "wo_b": w_out[l][a_width:].astype(BF16),
        "g_ffn": row(g_ffn[l]), "w_gate": w_gate_up[l][:, :d_ff].astype(BF16),
        "w_up": w_gate_up[l][:, d_ff:].astype(BF16), "w_down": w_down[l].astype(BF16), "g_final": row(g_final),
    }


def kernel(x_prompt, x_sample, cache_cmp_kv, cache_slc_kv, state_win_kv, page_table, g_attn, w_in, g_sgu, w_s, b_s,
           w_c1, b_c1, w_c2, g_out_a, g_out_b, w_out, g_ffn, w_gate_up, w_down, g_final):
    depth = w_in.shape[0]
    b, t, d = x_prompt.shape
    db, t_s, _ = x_sample.shape
    assert depth == 1 and t_s == 1
    n_pages = page_table.shape[1]
    past = n_pages * PAGE_SIZE
    wb = state_win_kv.shape[2]
    assert wb == WINDOW and past % SLC_BLOCK == 0
    l = 0
    w = _prepare_weights(l, g_attn, w_in, g_sgu, w_s, b_s, w_c1, b_c1, w_c2, g_out_a, g_out_b, w_out,
                         g_ffn, w_gate_up, w_down, g_final)
    kv_shape = (2, N_KV, HEAD_DIM)

    xp = x_prompt.reshape(b * t, d)
    mix_a, q, kvc, kvs, kvw, gates, kts, vs, ktw, vw = _in_projection(
        xp, jnp.arange(t, dtype=jnp.int32), w, prompt_shape=(b, t))
    kc = _compress_prompt(kvc.reshape(b, t, KV_LANES), w)
    mix_b = _nsa_prompt(q, gates, kc, kts, vs, ktw, vw, w, b, t)
    y_prompt = _output_ffn(xp, mix_a, mix_b, w, tm=512).reshape(b, t, d)
    kvw5 = kvw.reshape(b, t, *kv_shape)

    xs = x_sample.reshape(db, d)
    pos_s = past + jnp.zeros((db,), jnp.int32)
    mix_a_s, q_s, kvc_s, kvs_s, kvw_s, gates_s, v_rows = _in_projection(xs, pos_s, w)
    cache_cmp = cache_cmp_kv[l].reshape(-1, PAGE_SIZE, KV_LANES)
    kc_s = _compress_sample(cache_cmp, page_table, kvc_s, w)
    lp = -(-(past + t_s) // SLC_BLOCK) * SLC_BLOCK
    n_cmp_s = lp // CMP_STRIDE - 1
    n_blocks_s = (n_cmp_s + 1) * CMP_STRIDE // SLC_BLOCK
    o_cmp_s, idx_s = _select_sample(q_s, kc_s, past, n_cmp_s, n_blocks_s)
    mix_b_s, win_new = _attend_sample(q_s, gates_s, o_cmp_s, idx_s, cache_slc_kv[l], page_table, kvs_s,
                                      state_win_kv[l].reshape(db, wb, KV_LANES), kvw_s, w, past // SLC_BLOCK)
    y_sample = _output_ffn(xs, mix_a_s, mix_b_s.reshape(db, -1).astype(BF16), w, tm=db).reshape(db, t_s, d)

    return (y_prompt, y_sample,
            kvc.reshape(1, b, t, *kv_shape), kvs.reshape(1, b, t, *kv_shape), kvw5[:, t - min(WINDOW, t):][None],
            kvc_s.reshape(1, db, t_s, *kv_shape), kvs_s.reshape(1, db, t_s, *kv_shape),
            win_new.reshape(1, db, wb, *kv_shape), v_rows.reshape(1, db, t_s, -1))
```

```python
import functools

import jax
import jax.numpy as jnp
from jax import lax
from jax.experimental import pallas as pl
from jax.experimental.pallas import tpu as pltpu

F32 = jnp.float32
BF16 = jnp.bfloat16

A_GROUPS = 4
CHUNK = 128
HEAD_DIM = 64
N_KV = 2
ROT_DIM = HEAD_DIM // 4
ROPE_THETA = 500000.0
CMP_LEN = 32
CMP_STRIDE = 16
SLC_BLOCK = 64
N_SELECT = 16
WINDOW = 512
Q_BLOCK = 128
FORCE_BONUS = 1000.0
PAGE_SIZE = 128
NORM_EPS = 1e-6
MASKED = -1e30

LANES = 128
SUBLANES = 8
VMEM_LIMIT_BYTES = 56 * 1024 * 1024

KV_LANES = 2 * N_KV * HEAD_DIM
K_LANES = N_KV * HEAD_DIM
SLC_TILE = 512
WIN_KEYS = WINDOW + Q_BLOCK


def _rms(x, g):
    return x * lax.rsqrt(jnp.mean(x * x, axis=-1, keepdims=True) + NORM_EPS) * g


def _dot(a, b):
    return jnp.dot(a, b, preferred_element_type=F32)


def _dot_nt(a, b, precision=None):
    return lax.dot_general(a, b, (((1,), (1,)), ((), ())), precision=precision,
                           preferred_element_type=F32)


def _rope(z, rc, rs1, rs2):
    return z * rc + pltpu.roll(z, LANES - ROT_DIM // 2, 1) * rs1 + pltpu.roll(z, ROT_DIM // 2, 1) * rs2


def _project(x_ref, gattn_ref, wuv_ref, wq_ref, wkv_ref, wg_ref, rc_ref, rs1_ref, rs2_ref):
    xn = _rms(x_ref[...], gattn_ref[...]).astype(BF16)
    rc, rs1, rs2 = rc_ref[...], rs1_ref[...], rs2_ref[...]
    zuv = _dot(xn, wuv_ref[...])
    a_width = zuv.shape[1] // 2
    zq = _dot(xn, wq_ref[...])
    q = jnp.concatenate([_rope(zq[:, i * LANES:(i + 1) * LANES], rc, rs1, rs2)
                         for i in range(zq.shape[1] // LANES)], axis=1)
    zkv = _dot(xn, wkv_ref[...])
    branches = []
    for br in range(3):
        k = _rope(zkv[:, br * KV_LANES:br * KV_LANES + K_LANES], rc, rs1, rs2)
        v = zkv[:, br * KV_LANES + K_LANES:(br + 1) * KV_LANES]
        branches.append((k, v))
    gates = jax.nn.sigmoid(_dot(xn, wg_ref[...]))
    return zuv[:, :a_width], zuv[:, a_width:], q, branches, gates


def _gmlp_norm_v(v, gsgu):
    v = jax.nn.gelu(v)
    return jnp.concatenate([_rms(v[:, g * LANES:(g + 1) * LANES], gsgu[:, g * LANES:(g + 1) * LANES])
                            for g in range(A_GROUPS)], axis=1)


def _inproj_prompt_kernel(x_ref, gattn_ref, wuv_ref, wq_ref, wkv_ref, wg_ref, rc_ref, rs1_ref, rs2_ref,
                          ws_ref, bs_ref, gsgu_ref, goa_ref,
                          mixa_ref, q_ref, kvc_ref, gate_ref, kvct_ref, kvst_ref, kvwt_ref,
                          kts_ref, vs_ref, ktw_ref, vw_ref):
    u, v, q, branches, gates = _project(x_ref, gattn_ref, wuv_ref, wq_ref, wkv_ref, wg_ref,
                                        rc_ref, rs1_ref, rs2_ref)
    tm = u.shape[0]
    u = jax.nn.gelu(u)
    vg = _gmlp_norm_v(v, gsgu_ref[...]).astype(BF16)
    row = lax.broadcasted_iota(jnp.int32, (CHUNK, CHUNK), 0)
    col = lax.broadcasted_iota(jnp.int32, (CHUNK, CHUNK), 1)
    bias = bs_ref[...]
    parts = []
    for g in range(A_GROUPS):
        w = jnp.where(row >= col, ws_ref[g], 0.0).astype(BF16)
        s = jnp.concatenate(
            [_dot(w, vg[c * CHUNK:(c + 1) * CHUNK, g * LANES:(g + 1) * LANES]) for c in range(tm // CHUNK)],
            axis=0)
        s = s + jnp.concatenate([bias[:, g * LANES:(g + 1) * LANES]] * (tm // CHUNK), axis=0)
        parts.append(u[:, g * LANES:(g + 1) * LANES] * s)
    mixa_ref[...] = _rms(jnp.concatenate(parts, axis=1), goa_ref[...]).astype(BF16)
    q_ref[...] = q
    gate_ref[...] = gates
    kvc_ref[...] = jnp.concatenate(branches[0], axis=1)
    kts = []
    for ref, (k, v_) in zip((kvct_ref, kvst_ref, kvwt_ref), branches):
        kt = k.T
        ref[0, :K_LANES, :] = kt
        ref[0, K_LANES:, :] = v_.T
        kts.append(kt)
    kts_ref[0] = kts[1].astype(BF16)
    vs_ref[...] = branches[1][1].astype(BF16)
    ktw_ref[0] = kts[2].astype(BF16)
    vw_ref[...] = branches[2][1].astype(BF16)


def _inproj_sample_kernel(x_ref, gattn_ref, wuv_ref, wq_ref, wkv_ref, wg_ref, rc_ref, rs1_ref, rs2_ref,
                          ws0_ref, bs0_ref, gsgu_ref, goa_ref,
                          mixa_ref, q_ref, kvc_ref, kvs_ref, kvw_ref, gate_ref, vrow_ref):
    u, v, q, branches, gates = _project(x_ref, gattn_ref, wuv_ref, wq_ref, wkv_ref, wg_ref,
                                        rc_ref, rs1_ref, rs2_ref)
    vg = _gmlp_norm_v(v, gsgu_ref[...])
    o_a = jax.nn.gelu(u) * (vg * ws0_ref[...] + bs0_ref[...])
    mixa_ref[...] = _rms(o_a, goa_ref[...]).astype(BF16)
    vrow_ref[...] = vg
    q_ref[...] = q
    for ref, (k, v_) in zip((kvc_ref, kvs_ref, kvw_ref), branches):
        ref[...] = jnp.concatenate([k, v_], axis=1)
    gate_ref[...] = gates


def _full(shape):
    return pl.BlockSpec(shape, lambda *_: (0,) * len(shape))


def _params(*sem):
    return pltpu.CompilerParams(dimension_semantics=sem, vmem_limit_bytes=VMEM_LIMIT_BYTES)


def _in_projection(x2, pos, w, *, prompt_shape=None):
    n, d = x2.shape
    rc, rs1, rs2 = _rope_tables(pos)
    a_width = w["wuv"].shape[1] // 2
    b_width = w["wq"].shape[1]
    weights = [w["g_attn"], w["wuv"], w["wq"], w["wkv"], w["wg"]]
    wspecs = [_full(a.shape) for a in weights]
    tail = [w["g_sgu"], w["g_out_a"]]
    if prompt_shape is None:
        tm, grid = n, (1,)
        row = lambda i: (i, 0)
        rope_map = row
        gm = [w["ws0"], w["bs0"]]
    else:
        b, t = prompt_shape
        tm = 512
        assert t % tm == 0 and tm % CHUNK == 0
        tpb = t // tm
        grid = (b * tpb,)
        row = lambda i: (i, 0)
        rope_map = lambda i: (i % tpb, 0)
        gm = [w["w_s"], w["bs_full"]]
    rspec = pl.BlockSpec((tm, LANES), rope_map)
    in_specs = ([pl.BlockSpec((tm, d), row)] + wspecs + [rspec] * 3
                + [_full(a.shape) for a in gm] + [_full(a.shape) for a in tail])
    if prompt_shape is None:
        kern = _inproj_sample_kernel
        outs = [((n, a_width), BF16), ((n, b_width), F32), ((n, KV_LANES), F32), ((n, KV_LANES), F32),
                ((n, KV_LANES), F32), ((n, LANES), F32), ((n, a_width), F32)]
        out_specs = [pl.BlockSpec((tm, s[1]), row) for s, _ in outs]
    else:
        kern = _inproj_prompt_kernel
        kt_map = lambda i: (i // tpb, 0, i % tpb)
        outs = [((n, a_width), BF16), ((n, b_width), F32), ((n, KV_LANES), F32), ((n, LANES), F32)]
        out_specs = [pl.BlockSpec((tm, s[1]), row) for s, _ in outs]
        outs += [((b, KV_LANES, t), F32)] * 3
        out_specs += [pl.BlockSpec((1, KV_LANES, tm), kt_map)] * 3
        outs += [((b, K_LANES, t), BF16), ((n, K_LANES), BF16), ((b, K_LANES, t), BF16), ((n, K_LANES), BF16)]
        out_specs += [pl.BlockSpec((1, K_LANES, tm), kt_map), pl.BlockSpec((tm, K_LANES), row),
                      pl.BlockSpec((1, K_LANES, tm), kt_map), pl.BlockSpec((tm, K_LANES), row)]
    return pl.pallas_call(
        kern, grid=grid, in_specs=in_specs, out_specs=out_specs,
        out_shape=[jax.ShapeDtypeStruct(s, dt) for s, dt in outs],
        compiler_params=_params("parallel"), name="in_projection",
    )(x2, *weights, rc, rs1, rs2, *gm, *tail)


def _rope_tables(pos):
    half = ROT_DIM // 2
    inv = ROPE_THETA ** (-jnp.arange(half, dtype=F32) / half)
    ang = pos.astype(F32)[:, None] * inv[None, :]
    cos, sin = jnp.cos(ang), jnp.sin(ang)
    n = pos.shape[0]
    rest0 = jnp.zeros((n, HEAD_DIM - ROT_DIM), F32)
    zero = jnp.zeros((n, half), F32)
    rc = jnp.concatenate([cos, cos, rest0 + 1.0], axis=1)
    rs1 = jnp.concatenate([-sin, zero, rest0], axis=1)
    rs2 = jnp.concatenate([zero, sin, rest0], axis=1)
    return tuple(jnp.tile(a, (1, LANES // HEAD_DIM)) for a in (rc, rs1, rs2))


def _compress_partial(read_k, read_v, w1k_ref, w1v_ref):
    acc_k = acc_v = None
    for s in range(CMP_STRIDE):
        pk = _dot(read_k(s).astype(BF16), w1k_ref[s])
        pv = _dot(read_v(s).astype(BF16), w1v_ref[s])
        acc_k = pk if acc_k is None else acc_k + pk
        acc_v = pv if acc_v is None else acc_v + pv
    return acc_k, acc_v


def _compress_finish(fs_k, fs_v, b1_ref, w2_ref):
    hid = b1_ref.shape[1]
    outs = []
    for c, fs in enumerate((fs_k, fs_v)):
        hs = []
        for g in range(N_KV):
            first = fs[:, g * 2 * hid:g * 2 * hid + hid]
            second = fs[:, g * 2 * hid + hid:(g + 1) * 2 * hid]
            nxt = pltpu.roll(second, second.shape[0] - 1, 0)
            hs.append(jax.nn.silu(first + nxt + b1_ref[c:c + 1, :]))
        outs.append(_dot(jnp.concatenate(hs, axis=1).astype(BF16), w2_ref[c]))
    return jnp.concatenate(outs, axis=1)


def _compress_prompt_kernel(k_ref, v_ref, w1k_ref, w1v_ref, b1_ref, w2_ref, kc_ref):
    nb = kc_ref.shape[1]
    fs_k, fs_v = _compress_partial(lambda s: k_ref[0, pl.ds(s, nb, stride=CMP_STRIDE), :],
                                   lambda s: v_ref[0, pl.ds(s, nb, stride=CMP_STRIDE), :], w1k_ref, w1v_ref)
    kc_ref[0] = _compress_finish(fs_k, fs_v, b1_ref, w2_ref)


def _compress_prompt(kvc, w):
    b, t, _ = kvc.shape
    nb = t // CMP_STRIDE
    weights = [w["w1k"], w["w1v"], w["b_c1"], w["w2"]]
    return pl.pallas_call(
        _compress_prompt_kernel, grid=(b,),
        in_specs=[pl.BlockSpec((1, t, K_LANES), lambda i: (i, 0, 0)), pl.BlockSpec((1, t, K_LANES), lambda i: (i, 0, 1))]
        + [_full(a.shape) for a in weights],
        out_specs=pl.BlockSpec((1, nb, KV_LANES), lambda i: (i, 0, 0)),
        out_shape=jax.ShapeDtypeStruct((b, nb, KV_LANES), F32),
        compiler_params=_params("parallel"), name="compress_prompt",
    )(kvc, kvc, *weights)


def _stack_heads(q, g, q_per_kv):
    return jnp.concatenate([q[:, (g * q_per_kv + h) * HEAD_DIM:(g * q_per_kv + h + 1) * HEAD_DIM]
                            for h in range(q_per_kv)], axis=0)


def _cmp_branch(qg, kc, mask, reps):
    k, v = kc
    s = _dot_nt(qg, k, precision=lax.Precision.HIGHEST)
    if mask.shape[0] > 1:
        mask = jnp.concatenate([mask.astype(F32)] * reps, axis=0) > 0.5
    s = jnp.where(mask, s, MASKED)
    m = jnp.max(s, axis=-1, keepdims=True)
    e = jnp.where(mask, jnp.exp(s - m), 0.0)
    p = e / jnp.maximum(jnp.sum(e, axis=-1, keepdims=True), 1e-30)
    return p, _dot(p.astype(BF16), v.astype(BF16))


def _select_blocks(score, n_sel):
    rows, n = score.shape
    lane = lax.broadcasted_iota(jnp.int32, (rows, n), 1).astype(F32)
    sel = jnp.zeros((rows, n), F32)
    picks = []
    x = score
    for _ in range(n_sel):
        m = jnp.max(x, axis=-1, keepdims=True)
        idx = jnp.min(jnp.where(x == m, lane, float(n)), axis=-1, keepdims=True)
        hit = lane == idx
        ok = m > 0.1 * MASKED
        sel = jnp.where(hit & ok, 1.0, sel)
        x = jnp.where(hit, -3e38, x)
        picks.append((idx, ok))
    return sel, picks


def _block_scores(p_slc, blk_t, n_blocks):
    sj = lax.broadcasted_iota(jnp.int32, p_slc.shape, 1)
    causal = (sj <= blk_t) & (sj < n_blocks)
    forced = causal & ((sj == 0) | (sj >= blk_t - 1))
    score = jnp.where(forced, p_slc + FORCE_BONUS, p_slc)
    return jnp.where(causal, score, MASKED)


def _nsa_prompt_kernel(q_ref, gate_ref, kc_ref, kts_ref, vs_ref, ktw_ref, vw_ref, ovl_ref, exp_ref, gob_ref,
                       out_ref, m_scr, l_scr, acc_scr, *, n_cmp, n_heads):
    blk = pl.program_id(1)
    start = blk * Q_BLOCK
    q_per_kv = n_heads // N_KV
    q = q_ref[...] * (HEAD_DIM ** -0.5)
    gates = gate_ref[...]
    kc = kc_ref[0]
    ncp = kc.shape[0]
    n_blocks = ovl_ref.shape[1]
    tpos = start + lax.broadcasted_iota(jnp.int32, (Q_BLOCK, 1), 0)
    cn = lax.broadcasted_iota(jnp.int32, (1, ncp), 1)
    cmp_mask = (cn * CMP_STRIDE + CMP_LEN - 1 <= tpos) & (cn < n_cmp)
    ws = pl.multiple_of(jnp.maximum(start - WINDOW, 0), LANES)
    wpos = ws + lax.broadcasted_iota(jnp.int32, (1, WIN_KEYS), 1)
    dpos = tpos - wpos
    win_bias = jnp.where((dpos >= 0) & (dpos < WINDOW), 0.0, MASKED)
    win_bias = jnp.concatenate([win_bias] * q_per_kv, axis=0)
    n_tiles = blk // (SLC_TILE // Q_BLOCK) + 1
    heads = []
    for g in range(N_KV):
        rows = slice(g * HEAD_DIM, (g + 1) * HEAD_DIM)
        qg = _stack_heads(q, g, q_per_kv)
        qb = qg.astype(BF16)
        p_cmp, o_cmp = _cmp_branch(qg, (kc[:, rows], kc[:, K_LANES + g * HEAD_DIM:K_LANES + (g + 1) * HEAD_DIM]),
                                   cmp_mask, q_per_kv)
        p_sum = p_cmp[:Q_BLOCK]
        for h in range(1, q_per_kv):
            p_sum = p_sum + p_cmp[h * Q_BLOCK:(h + 1) * Q_BLOCK]
        p_slc = jnp.dot(p_sum, ovl_ref[...], precision=lax.Precision.HIGHEST, preferred_element_type=F32)
        score = _block_scores(p_slc, tpos // SLC_BLOCK, n_blocks)
        sel, _ = _select_blocks(score, min(N_SELECT, n_blocks))
        selb = sel.astype(BF16)
        m_scr[...] = jnp.full(m_scr.shape, MASKED, F32)
        l_scr[...] = jnp.zeros(l_scr.shape, F32)
        acc_scr[...] = jnp.zeros(acc_scr.shape, F32)

        def tile_step(kt, carry, qb=qb, selb=selb, rows=rows):
            off = pl.multiple_of(kt * SLC_TILE, SLC_TILE)
            s = _dot(qb, kts_ref[0, rows, pl.ds(off, SLC_TILE)])
            chosen = _dot(selb, exp_ref[:, pl.ds(off, SLC_TILE)])
            kpos = off + lax.broadcasted_iota(jnp.int32, (1, SLC_TILE), 1)
            bias = jnp.where((chosen > 0.5) & (kpos <= tpos), 0.0, MASKED)
            s = s + jnp.concatenate([bias] * q_per_kv, axis=0)
            m_old = m_scr[...]
            m_new = jnp.maximum(m_old, jnp.max(s, axis=-1, keepdims=True))
            alpha = jnp.exp(m_old - m_new)
            p = jnp.exp(s - m_new)
            l_scr[...] = alpha * l_scr[...] + jnp.sum(p, axis=-1, keepdims=True)
            acc_scr[...] = alpha * acc_scr[...] + _dot(p.astype(BF16), vs_ref[pl.ds(off, SLC_TILE), :])
            m_scr[...] = m_new
            return carry

        lax.fori_loop(0, n_tiles, tile_step, 0)
        o_slc = (acc_scr[...] / l_scr[...])[:, rows]
        s = _dot(qb, ktw_ref[0, rows, pl.ds(ws, WIN_KEYS)]) + win_bias
        p = jnp.exp(s - jnp.max(s, axis=-1, keepdims=True))
        o_win = _dot(p.astype(BF16), vw_ref[pl.ds(ws, WIN_KEYS), :]) / jnp.sum(p, axis=-1, keepdims=True)
        o_win = o_win[:, rows]
        for h in range(q_per_kv):
            hd = g * q_per_kv + h
            r = slice(h * Q_BLOCK, (h + 1) * Q_BLOCK)
            heads.append(gates[:, 3 * hd:3 * hd + 1] * o_cmp[r] + gates[:, 3 * hd + 1:3 * hd + 2] * o_slc[r]
                         + gates[:, 3 * hd + 2:3 * hd + 3] * o_win[r])
    out_ref[...] = _rms(jnp.concatenate(heads, axis=1), gob_ref[...]).astype(BF16)


def _nsa_prompt(q, gates, kc, kts, vs, ktw, vw, w, b, t):
    n, b_width = q.shape
    n_heads = b_width // HEAD_DIM
    assert t % SLC_TILE == 0 and t >= WIN_KEYS
    nqb = t // Q_BLOCK
    ncp = kc.shape[1]
    n_cmp = ncp - 1
    n_blocks = (n_cmp + 1) * CMP_STRIDE // SLC_BLOCK
    ovl = _overlap_matrix(ncp, n_blocks, n_blocks)
    key_blk = jnp.arange(t, dtype=jnp.int32)[None, :] // SLC_BLOCK
    expand = (key_blk == jnp.arange(n_blocks, dtype=jnp.int32)[:, None]).astype(BF16)
    tok = lambda i, j: (i * nqb + j, 0)
    seq3 = lambda i, j: (i, 0, 0)
    seq2 = lambda i, j: (i, 0)
    rows = (n_heads // N_KV) * Q_BLOCK
    return pl.pallas_call(
        functools.partial(_nsa_prompt_kernel, n_cmp=n_cmp, n_heads=n_heads),
        grid=(b, nqb),
        in_specs=[pl.BlockSpec((Q_BLOCK, b_width), tok), pl.BlockSpec((Q_BLOCK, LANES), tok),
                  pl.BlockSpec((1, ncp, KV_LANES), seq3),
                  pl.BlockSpec((1, K_LANES, t), seq3), pl.BlockSpec((t, K_LANES), seq2),
                  pl.BlockSpec((1, K_LANES, t), seq3), pl.BlockSpec((t, K_LANES), seq2),
                  _full(ovl.shape), _full(expand.shape), _full(w["g_out_b"].shape)],
        out_specs=pl.BlockSpec((Q_BLOCK, b_width), tok),
        out_shape=jax.ShapeDtypeStruct((n, b_width), BF16),
        scratch_shapes=[pltpu.VMEM((rows, 1), F32), pltpu.VMEM((rows, 1), F32), pltpu.VMEM((rows, K_LANES), F32)],
        compiler_params=_params("parallel", "arbitrary"), name="nsa_prompt",
    )(q, gates, kc, kts, vs, ktw, vw, ovl, expand, w["g_out_b"])


def _overlap_matrix(rows, cols, n_blocks):
    ci = jnp.arange(rows, dtype=jnp.int32)[:, None]
    sj = jnp.arange(cols, dtype=jnp.int32)[None, :]
    hit = (ci * CMP_STRIDE < (sj + 1) * SLC_BLOCK) & (ci * CMP_STRIDE + CMP_LEN > sj * SLC_BLOCK) & (sj < n_blocks)
    return hit.astype(F32)


def _compress_sample_kernel(pt_ref, *refs, pages_per_step, n_steps):
    del pt_ref
    pages = refs[:pages_per_step]
    new_ref, w1k_ref, w1v_ref, b1_ref, w2_ref, kc_ref, fsk_scr, fsv_scr, xk_scr, xv_scr = refs[pages_per_step:]
    j = pl.program_id(1)
    blocks_per_page = PAGE_SIZE // CMP_STRIDE
    step_blocks = pages_per_step * blocks_per_page
    for i, page in enumerate(pages):
        xk_scr[pl.ds(i * PAGE_SIZE, PAGE_SIZE), :] = page[0, :K_LANES, :].T
        xv_scr[pl.ds(i * PAGE_SIZE, PAGE_SIZE), :] = page[0, K_LANES:, :].T
    fs_k, fs_v = _compress_partial(lambda s: xk_scr[pl.ds(s, step_blocks, stride=CMP_STRIDE), :],
                                   lambda s: xv_scr[pl.ds(s, step_blocks, stride=CMP_STRIDE), :], w1k_ref, w1v_ref)
    off = pl.multiple_of(j * step_blocks, step_blocks)
    fsk_scr[pl.ds(off, step_blocks), :] = fs_k
    fsv_scr[pl.ds(off, step_blocks), :] = fs_v

    @pl.when(j == n_steps - 1)
    def _():
        past_blocks = n_steps * step_blocks
        tail = fsk_scr.shape[0] - past_blocks
        new = new_ref[0]
        is_first = lax.broadcasted_iota(jnp.int32, (tail, 1), 0) == 0
        nk = _dot(new[:, :K_LANES].astype(BF16), w1k_ref[0])
        nv = _dot(new[:, K_LANES:].astype(BF16), w1v_ref[0])
        fsk_scr[pl.ds(past_blocks, tail), :] = jnp.where(is_first, nk, 0.0)
        fsv_scr[pl.ds(past_blocks, tail), :] = jnp.where(is_first, nv, 0.0)
        kc_ref[0] = _compress_finish(fsk_scr[...], fsv_scr[...], b1_ref, w2_ref)


def _compress_sample(cache_cmp_t, page_table, new_rows, w):
    db, n_pages = page_table.shape
    pages_per_step = 32
    assert n_pages % pages_per_step == 0
    n_steps = n_pages // pages_per_step
    blocks_per_page = PAGE_SIZE // CMP_STRIDE
    past_blocks = n_pages * blocks_per_page
    nbp = past_blocks + SUBLANES
    weights = [w["w1k"], w["w1v"], w["b_c1"], w["w2"]]
    hid2 = w["w1k"].shape[2]
    page_spec = lambda i: pl.BlockSpec((1, KV_LANES, PAGE_SIZE),
                                       lambda b, j, pt, i=i: (pt[b, j * pages_per_step + i], 0, 0))
    step_rows = pages_per_step * PAGE_SIZE
    grid_spec = pltpu.PrefetchScalarGridSpec(
        num_scalar_prefetch=1, grid=(db, n_steps),
        in_specs=[page_spec(i) for i in range(pages_per_step)]
        + [pl.BlockSpec((1, 1, KV_LANES), lambda b, j, pt: (b, 0, 0))]
        + [pl.BlockSpec(a.shape, lambda b, j, pt, nd=a.ndim: (0,) * nd) for a in weights],
        out_specs=pl.BlockSpec((1, nbp, KV_LANES), lambda b, j, pt: (b, 0, 0)),
        scratch_shapes=[pltpu.VMEM((nbp, hid2), F32), pltpu.VMEM((nbp, hid2), F32),
                        pltpu.VMEM((step_rows, K_LANES), F32), pltpu.VMEM((step_rows, K_LANES), F32)])
    return pl.pallas_call(
        functools.partial(_compress_sample_kernel, pages_per_step=pages_per_step, n_steps=n_steps),
        grid_spec=grid_spec, out_shape=jax.ShapeDtypeStruct((db, nbp, KV_LANES), F32),
        compiler_params=_params("parallel", "arbitrary"), name="compress_sample",
    )(page_table, *([cache_cmp_t] * pages_per_step), new_rows[:, None, :], *weights)


def _select_sample_kernel(q_ref, kc_ref, ovl_ref, ocmp_ref, idx_ref, *, pos, n_cmp, n_blocks, n_heads):
    q_per_kv = n_heads // N_KV
    q = q_ref[0] * (HEAD_DIM ** -0.5)
    kc = kc_ref[0]
    ncp = kc.shape[0]
    cn = lax.broadcasted_iota(jnp.int32, (1, ncp), 1)
    cmp_mask = (cn * CMP_STRIDE + CMP_LEN - 1 <= pos) & (cn < n_cmp)
    o_rows, scores = [], []
    for g in range(N_KV):
        qg = _stack_heads(q, g, q_per_kv)
        p_cmp, o_cmp = _cmp_branch(qg, (kc[:, g * HEAD_DIM:(g + 1) * HEAD_DIM],
                                        kc[:, K_LANES + g * HEAD_DIM:K_LANES + (g + 1) * HEAD_DIM]),
                                   cmp_mask, q_per_kv)
        o_rows.append(o_cmp)
        p_sum = jnp.sum(p_cmp, axis=0, keepdims=True)
        scores.append(jnp.dot(p_sum, ovl_ref[...], precision=lax.Precision.HIGHEST, preferred_element_type=F32))
    ocmp_ref[0] = jnp.concatenate(o_rows, axis=0)
    score = _block_scores(jnp.concatenate(scores, axis=0), pos // SLC_BLOCK, n_blocks)
    _, picks = _select_blocks(score, min(N_SELECT, n_blocks))
    lane = lax.broadcasted_iota(jnp.int32, (N_KV, LANES), 1)
    out = jnp.full((N_KV, LANES), -1.0, F32)
    for i, (idx, ok) in enumerate(picks):
        out = jnp.where((lane == i) & ok, idx, out)
    idx_ref[0] = out.astype(jnp.int32)


def _select_sample(q, kc, pos, n_cmp, n_blocks):
    db, b_width = q.shape
    n_heads = b_width // HEAD_DIM
    ncp = kc.shape[1]
    nsp = -(-n_blocks // LANES) * LANES
    ovl = _overlap_matrix(ncp, nsp, n_blocks)
    return pl.pallas_call(
        functools.partial(_select_sample_kernel, pos=pos, n_cmp=n_cmp, n_blocks=n_blocks, n_heads=n_heads),
        grid=(db,),
        in_specs=[pl.BlockSpec((1, 1, b_width), lambda i: (i, 0, 0)),
                  pl.BlockSpec((1, ncp, KV_LANES), lambda i: (i, 0, 0)), _full(ovl.shape)],
        out_specs=[pl.BlockSpec((1, n_heads, HEAD_DIM), lambda i: (i, 0, 0)),
                   pl.BlockSpec((1, N_KV, LANES), lambda i: (i, 0, 0))],
        out_shape=[jax.ShapeDtypeStruct((db, n_heads, HEAD_DIM), F32),
                   jax.ShapeDtypeStruct((db, N_KV, LANES), jnp.int32)],
        compiler_params=_params("parallel"), name="select_sample",
    )(q[:, None, :], kc, ovl)


def _attend_sample_kernel(page_ref, *refs, n_sel, past_blocks, n_heads):
    del page_ref
    n_slots = N_KV * n_sel
    pages = refs[:n_slots]
    (q_ref, gate_ref, ocmp_ref, idx_ref, newslc_ref, win_ref, newwin_ref, exp_ref, gob_ref,
     out_ref, winout_ref) = refs[n_slots:]
    q_per_kv = n_heads // N_KV
    q = q_ref[0] * (HEAD_DIM ** -0.5)
    lane = lax.broadcasted_iota(jnp.int32, (1, K_LANES), 1)
    wb = win_ref.shape[2]
    is_last = lax.broadcasted_iota(jnp.int32, (1, wb), 1) == wb - 1
    win = jnp.where(is_last, newwin_ref[0], pltpu.roll(win_ref[0], wb - 1, 1))
    winout_ref[0] = win
    win_k = win[:K_LANES].astype(BF16)
    win_v = win[K_LANES:].astype(BF16)
    new_slc = newslc_ref[0]
    idx = idx_ref[0].astype(F32)
    n_keys = n_sel * PAGE_SIZE
    key_half = (lax.broadcasted_iota(jnp.int32, (1, n_keys), 1) % PAGE_SIZE) // SLC_BLOCK
    o_slc, o_win = [], []
    for g in range(N_KV):
        qpad = jnp.concatenate(
            [jnp.where(lane // HEAD_DIM == g,
                       jnp.concatenate([q[:, (g * q_per_kv + h) * HEAD_DIM:(g * q_per_kv + h + 1) * HEAD_DIM]] * N_KV,
                                       axis=1), 0.0)
             for h in range(q_per_kv)], axis=0)
        qpb = qpad.astype(BF16)
        kt = jnp.concatenate([pages[g * n_sel + i][0, :K_LANES, :] for i in range(n_sel)], axis=1).astype(BF16)
        vt = jnp.concatenate([pages[g * n_sel + i][0, K_LANES:, :] for i in range(n_sel)], axis=1).astype(BF16)
        s = _dot(qpb, kt)
        idg = idx[g:g + 1, :]
        idk = jnp.dot(idg, exp_ref[...], precision=lax.Precision.HIGHEST, preferred_element_type=F32)
        parity = idk - 2.0 * jnp.floor(idk * 0.5)
        key_ok = (idk >= 0.0) & (idk < past_blocks) & (parity == key_half.astype(F32))
        s = jnp.where(key_ok, s, MASKED)
        has_new = jnp.max(jnp.where(idg == past_blocks, 1.0, 0.0), axis=-1, keepdims=True) > 0.5
        s_new = jnp.sum(qpad * new_slc[:, :K_LANES], axis=-1, keepdims=True)
        s_new = jnp.where(has_new, s_new, MASKED)
        m = jnp.maximum(jnp.max(s, axis=-1, keepdims=True), s_new)
        e = jnp.where(key_ok, jnp.exp(s - m), 0.0)
        e_new = jnp.where(has_new, jnp.exp(s_new - m), 0.0)
        den = jnp.maximum(jnp.sum(e, axis=-1, keepdims=True) + e_new, 1e-30)
        o = (_dot_nt(e.astype(BF16), vt) + e_new * new_slc[:, K_LANES:]) / den
        o_slc.append(o[:, g * HEAD_DIM:(g + 1) * HEAD_DIM])
        s = _dot(qpb, win_k)
        e = jnp.exp(s - jnp.max(s, axis=-1, keepdims=True))
        o = _dot_nt(e.astype(BF16), win_v) / jnp.sum(e, axis=-1, keepdims=True)
        o_win.append(o[:, g * HEAD_DIM:(g + 1) * HEAD_DIM])
    gates = gate_ref[0]
    o = (gates[:, 0:1] * ocmp_ref[0] + gates[:, 1:2] * jnp.concatenate(o_slc, axis=0)
         + gates[:, 2:3] * jnp.concatenate(o_win, axis=0))
    ms = jnp.sum(jnp.sum(o * o, axis=-1, keepdims=True), axis=0, keepdims=True) / (n_heads * HEAD_DIM)
    out_ref[0] = o * lax.rsqrt(ms + NORM_EPS) * gob_ref[...]


def _attend_sample(q, gates, o_cmp, idx, cache_slc_t, page_table, new_slc, win_t, new_win, w, past_blocks):
    db, b_width = q.shape
    n_heads = b_width // HEAD_DIM
    n_sel = min(N_SELECT, past_blocks + 1)
    sub_per_page = PAGE_SIZE // SLC_BLOCK
    jp = jnp.clip(idx[:, :, :n_sel], 0, past_blocks - 1)
    page = jnp.take_along_axis(page_table, (jp // sub_per_page).reshape(db, -1), axis=1).astype(jnp.int32)
    n_keys = n_sel * PAGE_SIZE
    expand = (jnp.arange(n_keys, dtype=jnp.int32)[None, :] // PAGE_SIZE
              == jnp.arange(LANES, dtype=jnp.int32)[:, None]).astype(F32)
    gob = w["g_out_b"].reshape(n_heads, HEAD_DIM)
    wb = win_t.shape[2]
    per_seq = lambda shape: pl.BlockSpec((1,) + shape, lambda b, pg: (b, 0, 0))
    page_spec = lambda i: pl.BlockSpec((1, KV_LANES, PAGE_SIZE), lambda b, pg, i=i: (pg[b, i], 0, 0))
    grid_spec = pltpu.PrefetchScalarGridSpec(
        num_scalar_prefetch=1, grid=(db,),
        in_specs=[page_spec(i) for i in range(N_KV * n_sel)]
        + [per_seq((1, b_width)), per_seq((n_heads, 3)), per_seq((n_heads, HEAD_DIM)), per_seq((N_KV, LANES)),
           per_seq((1, KV_LANES)), per_seq((KV_LANES, wb)), per_seq((KV_LANES, 1)),
           pl.BlockSpec(expand.shape, lambda b, pg: (0, 0)), pl.BlockSpec(gob.shape, lambda b, pg: (0, 0))],
        out_specs=[per_seq((n_heads, HEAD_DIM)), per_seq((KV_LANES, wb))])
    return pl.pallas_call(
        functools.partial(_attend_sample_kernel, n_sel=n_sel, past_blocks=past_blocks, n_heads=n_heads),
        grid_spec=grid_spec,
        out_shape=[jax.ShapeDtypeStruct((db, n_heads, HEAD_DIM), F32),
                   jax.ShapeDtypeStruct((db, KV_LANES, wb), F32)],
        compiler_params=_params("parallel"), name="attend_sample",
    )(page, *([cache_slc_t] * (N_KV * n_sel)), q[:, None, :], gates[:, :n_heads * 3].reshape(db, n_heads, 3), o_cmp,
      idx, new_slc[:, None, :], win_t, new_win[:, :, None], expand, gob)


def _ffn_kernel(x_ref, ma_ref, mb_ref, woa_ref, wob_ref, gffn_ref, wgate_ref, wup_ref, wdown_ref, gfin_ref, y_ref,
                acc_scr, xn_scr):
    c = pl.program_id(1)

    @pl.when(c == 0)
    def _():
        x = x_ref[...] + _dot(ma_ref[...], woa_ref[...]) + _dot(mb_ref[...], wob_ref[...])
        acc_scr[...] = x
        xn_scr[...] = _rms(x, gffn_ref[...]).astype(BF16)

    xn = xn_scr[...]
    hid = jax.nn.silu(_dot(xn, wgate_ref[...])) * _dot(xn, wup_ref[...])
    acc_scr[...] += _dot(hid.astype(BF16), wdown_ref[...])

    @pl.when(c == pl.num_programs(1) - 1)
    def _():
        y_ref[...] = _rms(acc_scr[...], gfin_ref[...])


def _output_ffn(x2, mix_a, mix_b, w, tm):
    n, d = x2.shape
    d_ff = w["w_down"].shape[0]
    ff_chunks = 2
    step = d_ff // ff_chunks
    assert n % tm == 0 and d_ff % ff_chunks == 0 and step % LANES == 0
    row = lambda i, c: (i, 0)
    fixed = lambda a: pl.BlockSpec(a.shape, lambda i, c: (0, 0))
    return pl.pallas_call(
        _ffn_kernel,
        grid=(n // tm, ff_chunks),
        in_specs=[pl.BlockSpec((tm, d), row), pl.BlockSpec((tm, mix_a.shape[1]), row),
                  pl.BlockSpec((tm, mix_b.shape[1]), row), fixed(w["wo_a"]), fixed(w["wo_b"]), fixed(w["g_ffn"]),
                  pl.BlockSpec((d, step), lambda i, c: (0, c)), pl.BlockSpec((d, step), lambda i, c: (0, c)),
                  pl.BlockSpec((step, d), lambda i, c: (c, 0)), fixed(w["g_final"])],
        out_specs=pl.BlockSpec((tm, d), row),
        out_shape=jax.ShapeDtypeStruct((n, d), F32),
        scratch_shapes=[pltpu.VMEM((tm, d), F32), pltpu.VMEM((tm, d), BF16)],
        compiler_params=_params("parallel", "arbitrary"), name="output_ffn",
    )(x2, mix_a, mix_b, w["wo_a"], w["wo_b"], w["g_ffn"], w["w_gate"], w["w_up"], w["w_down"], w["g_final"])


def _prepare_weights(l, g_attn, w_in, g_sgu, w_s, b_s, w_c1, b_c1, w_c2, g_out_a, g_out_b, w_out,
                     g_ffn, w_gate_up, w_down, g_final):
    a_width = g_sgu.shape[1]
    b_width = g_out_b.shape[1]
    n_heads = b_width // HEAD_DIM
    d_ff = w_down.shape[1]
    hid = b_c1.shape[2]
    o1 = 2 * a_width
    o2 = o1 + b_width
    o3 = o2 + 3 * KV_LANES
    wi = w_in[l]
    wg = jnp.pad(wi[:, o3:], ((0, 0), (0, LANES - 3 * n_heads)))
    row = lambda a: a.reshape(1, -1)
    w1 = w_c1[l].reshape(2, 2, CMP_STRIDE, HEAD_DIM, hid)
    w1 = jnp.transpose(w1, (0, 2, 3, 1, 4)).reshape(2, CMP_STRIDE, HEAD_DIM, 2 * hid)
    eye = jnp.eye(N_KV, dtype=F32)
    w1 = jnp.einsum("gh,csdn->csgdhn", eye, w1).reshape(2, CMP_STRIDE, K_LANES, N_KV * 2 * hid)
    w2 = jnp.einsum("gh,cne->cgnhe", eye, w_c2[l]).reshape(2, N_KV * hid, K_LANES)
    return {
        "g_attn": row(g_attn[l]), "wuv": wi[:, :o1].astype(BF16), "wq": wi[:, o1:o2].astype(BF16),
        "wkv": wi[:, o2:o3].astype(BF16), "wg": wg.astype(BF16),
        "g_sgu": row(g_sgu[l]), "g_out_a": row(g_out_a[l]), "g_out_b": row(g_out_b[l]),
        "w_s": w_s[l], "bs_full": jnp.repeat(b_s[l].T, LANES, axis=1),
        "ws0": row(jnp.repeat(w_s[l][:, 0, 0], LANES)), "bs0": row(jnp.repeat(b_s[l][:, 0], LANES)),
        "w1k": w1[0].astype(BF16), "w1v": w1[1].astype(BF16), "b_c1": b_c1[l], "w2": w2.astype(BF16),
        "wo_a": w_out[l][:a_width].astype(BF16), "wo_b": w_out[l][a_width:].astype(BF16),
        "g_ffn": row(g_ffn[l]), "w_gate": w_gate_up[l][:, :d_ff].astype(BF16),
        "w_up": w_gate_up[l][:, d_ff:].astype(BF16), "w_down": w_down[l].astype(BF16), "g_final": row(g_final),
    }


def kernel(x_prompt, x_sample, cache_cmp_kv, cache_slc_kv, state_win_kv, page_table, g_attn, w_in, g_sgu, w_s, b_s,
           w_c1, b_c1, w_c2, g_out_a, g_out_b, w_out, g_ffn, w_gate_up, w_down, g_final):
    depth = w_in.shape[0]
    b, t, d = x_prompt.shape
    db, t_s, _ = x_sample.shape
    assert depth == 1 and t_s == 1
    n_pages = page_table.shape[1]
    past = n_pages * PAGE_SIZE
    wb = state_win_kv.shape[2]
    assert wb == WINDOW and past % SLC_BLOCK == 0
    l = 0
    w = _prepare_weights(l, g_attn, w_in, g_sgu, w_s, b_s, w_c1, b_c1, w_c2, g_out_a, g_out_b, w_out,
                         g_ffn, w_gate_up, w_down, g_final)
    xp = x_prompt.reshape(b * t, d)
    mix_a, q, kvc, gates, kvc_t, kvs_t, kvw_t, kts, vs, ktw, vw = _in_projection(
        xp, jnp.arange(t, dtype=jnp.int32), w, prompt_shape=(b, t))
    kc = _compress_prompt(kvc.reshape(b, t, KV_LANES), w)
    mix_b = _nsa_prompt(q, gates, kc, kts, vs, ktw, vw, w, b, t)
    y_prompt = _output_ffn(xp, mix_a, mix_b, w, tm=512).reshape(b, t, d)

    xs = x_sample.reshape(db, d)
    pos_s = past + jnp.zeros((db,), jnp.int32)
    mix_a_s, q_s, kvc_s, kvs_s, kvw_s, gates_s, v_rows = _in_projection(xs, pos_s, w)
    kc_s = _compress_sample(_feature_major(cache_cmp_kv[l]), page_table, kvc_s, w)
    lp = -(-(past + t_s) // SLC_BLOCK) * SLC_BLOCK
    n_cmp_s = lp // CMP_STRIDE - 1
    n_blocks_s = (n_cmp_s + 1) * CMP_STRIDE // SLC_BLOCK
    o_cmp_s, idx_s = _select_sample(q_s, kc_s, past, n_cmp_s, n_blocks_s)
    mix_b_s, win_new_t = _attend_sample(q_s, gates_s, o_cmp_s, idx_s, _feature_major(cache_slc_kv[l]), page_table,
                                        kvs_s, _feature_major(state_win_kv[l]), kvw_s, w, past // SLC_BLOCK)
    y_sample = _output_ffn(xs, mix_a_s, mix_b_s.reshape(db, -1).astype(BF16), w, tm=db).reshape(db, t_s, d)

    kv_shape = (2, N_KV, HEAD_DIM)
    return (y_prompt, y_sample,
            _row_major(kvc_t)[None], _row_major(kvs_t)[None], _row_major(kvw_t[:, :, t - min(WINDOW, t):])[None],
            kvc_s.reshape(1, db, t_s, *kv_shape), kvs_s.reshape(1, db, t_s, *kv_shape),
            _row_major(win_new_t)[None], v_rows.reshape(1, db, t_s, -1))


def _feature_major(kv):
    n, rows = kv.shape[:2]
    return jnp.transpose(kv, (0, 2, 3, 4, 1)).reshape(n, KV_LANES, rows)


def _row_major(kv_t):
    n, _, rows = kv_t.shape
    return jnp.transpose(kv_t.reshape(n, 2, N_KV, HEAD_DIM, rows), (0, 4, 1, 2, 3))
```

```python
import functools

import jax
import jax.numpy as jnp
from jax import lax
from jax.experimental import pallas as pl
from jax.experimental.pallas import tpu as pltpu

F32 = jnp.float32
BF16 = jnp.bfloat16

A_GROUPS = 4
CHUNK = 128
HEAD_DIM = 64
N_KV = 2
ROT_DIM = HEAD_DIM // 4
ROPE_THETA = 500000.0
CMP_LEN = 32
CMP_STRIDE = 16
SLC_BLOCK = 64
N_SELECT = 16
WINDOW = 512
Q_BLOCK = 128
FORCE_BONUS = 1000.0
PAGE_SIZE = 128
NORM_EPS = 1e-6
MASKED = -1e30
LOG2_E = 1.4426950408889634

LANES = 128
SUBLANES = 8
VMEM_LIMIT_BYTES = 56 * 1024 * 1024

KV_LANES = 2 * N_KV * HEAD_DIM
K_LANES = N_KV * HEAD_DIM
SLC_TILE = 512
WIN_KEYS = WINDOW + Q_BLOCK


def _rms(x, g):
    return x * lax.rsqrt(jnp.mean(x * x, axis=-1, keepdims=True) + NORM_EPS) * g


def _dot(a, b):
    return jnp.dot(a, b, preferred_element_type=F32)


def _dot_nt(a, b, precision=None):
    return lax.dot_general(a, b, (((1,), (1,)), ((), ())), precision=precision,
                           preferred_element_type=F32)


def _rope(z, rc, rs1, rs2):
    return z * rc + pltpu.roll(z, LANES - ROT_DIM // 2, 1) * rs1 + pltpu.roll(z, ROT_DIM // 2, 1) * rs2


def _project(x_ref, gattn_ref, wuv_ref, wq_ref, wkv_ref, wg_ref, rc_ref, rs1_ref, rs2_ref):
    xn = _rms(x_ref[...], gattn_ref[...]).astype(BF16)
    rc, rs1, rs2 = rc_ref[...], rs1_ref[...], rs2_ref[...]
    zuv = _dot(xn, wuv_ref[...])
    a_width = zuv.shape[1] // 2
    zq = _dot(xn, wq_ref[...])
    q = jnp.concatenate([_rope(zq[:, i * LANES:(i + 1) * LANES], rc, rs1, rs2)
                         for i in range(zq.shape[1] // LANES)], axis=1)
    zkv = _dot(xn, wkv_ref[...])
    branches = []
    for br in range(3):
        k = _rope(zkv[:, br * KV_LANES:br * KV_LANES + K_LANES], rc, rs1, rs2)
        v = zkv[:, br * KV_LANES + K_LANES:(br + 1) * KV_LANES]
        branches.append((k, v))
    gates = jax.nn.sigmoid(_dot(xn, wg_ref[...]))
    return zuv[:, :a_width], zuv[:, a_width:], q, branches, gates


def _gmlp_norm_v(v, gsgu):
    v = jax.nn.gelu(v)
    return jnp.concatenate([_rms(v[:, g * LANES:(g + 1) * LANES], gsgu[:, g * LANES:(g + 1) * LANES])
                            for g in range(A_GROUPS)], axis=1)


def _inproj_prompt_kernel(x_ref, gattn_ref, wuv_ref, wq_ref, wkv_ref, wg_ref, rc_ref, rs1_ref, rs2_ref,
                          ws_ref, bs_ref, gsgu_ref, goa_ref,
                          mixa_ref, q_ref, kvc_ref, gate_ref, kvct_ref, kvst_ref, kvwt_ref,
                          kts_ref, vs_ref, ktw_ref, vw_ref):
    u, v, q, branches, gates = _project(x_ref, gattn_ref, wuv_ref, wq_ref, wkv_ref, wg_ref,
                                        rc_ref, rs1_ref, rs2_ref)
    tm = u.shape[0]
    u = jax.nn.gelu(u)
    vg = _gmlp_norm_v(v, gsgu_ref[...]).astype(BF16)
    row = lax.broadcasted_iota(jnp.int32, (CHUNK, CHUNK), 0)
    col = lax.broadcasted_iota(jnp.int32, (CHUNK, CHUNK), 1)
    bias = bs_ref[...]
    parts = []
    for g in range(A_GROUPS):
        w = jnp.where(row >= col, ws_ref[g], 0.0).astype(BF16)
        s = jnp.concatenate(
            [_dot(w, vg[c * CHUNK:(c + 1) * CHUNK, g * LANES:(g + 1) * LANES]) for c in range(tm // CHUNK)],
            axis=0)
        s = s + jnp.concatenate([bias[:, g * LANES:(g + 1) * LANES]] * (tm // CHUNK), axis=0)
        parts.append(u[:, g * LANES:(g + 1) * LANES] * s)
    mixa_ref[...] = _rms(jnp.concatenate(parts, axis=1), goa_ref[...]).astype(BF16)
    q_ref[...] = q
    gate_ref[...] = gates
    kvc_ref[...] = jnp.concatenate(branches[0], axis=1)
    kts = []
    for ref, (k, v_) in zip((kvct_ref, kvst_ref, kvwt_ref), branches):
        kt = k.T
        ref[0, :K_LANES, :] = kt
        ref[0, K_LANES:, :] = v_.T
        kts.append(kt)
    own = [lax.broadcasted_iota(jnp.int32, (1, K_LANES), 1) // HEAD_DIM == g for g in range(N_KV)]
    for kt_ref, va_ref, br in ((kts_ref, vs_ref, 1), (ktw_ref, vw_ref, 2)):
        kt_ref[0] = kts[br].astype(BF16)
        va_ref[...] = jnp.concatenate([jnp.where(m, branches[br][1], 1.0) for m in own], axis=1).astype(BF16)


def _inproj_sample_kernel(x_ref, gattn_ref, wuv_ref, wq_ref, wkv_ref, wg_ref, rc_ref, rs1_ref, rs2_ref,
                          ws0_ref, bs0_ref, gsgu_ref, goa_ref,
                          mixa_ref, q_ref, kvc_ref, kvs_ref, kvw_ref, gate_ref, vrow_ref):
    u, v, q, branches, gates = _project(x_ref, gattn_ref, wuv_ref, wq_ref, wkv_ref, wg_ref,
                                        rc_ref, rs1_ref, rs2_ref)
    vg = _gmlp_norm_v(v, gsgu_ref[...])
    o_a = jax.nn.gelu(u) * (vg * ws0_ref[...] + bs0_ref[...])
    mixa_ref[...] = _rms(o_a, goa_ref[...]).astype(BF16)
    vrow_ref[...] = vg
    q_ref[...] = q
    for ref, (k, v_) in zip((kvc_ref, kvs_ref, kvw_ref), branches):
        ref[...] = jnp.concatenate([k, v_], axis=1)
    gate_ref[...] = gates


def _full(shape):
    return pl.BlockSpec(shape, lambda *_: (0,) * len(shape))


def _params(*sem):
    return pltpu.CompilerParams(dimension_semantics=sem, vmem_limit_bytes=VMEM_LIMIT_BYTES)


def _in_projection(x2, pos, w, *, prompt_shape=None):
    n, d = x2.shape
    rc, rs1, rs2 = _rope_tables(pos)
    a_width = w["wuv"].shape[1] // 2
    b_width = w["wq"].shape[1]
    weights = [w["g_attn"], w["wuv"], w["wq"], w["wkv"], w["wg"]]
    wspecs = [_full(a.shape) for a in weights]
    tail = [w["g_sgu"], w["g_out_a"]]
    if prompt_shape is None:
        tm, grid = n, (1,)
        row = lambda i: (i, 0)
        rope_map = row
        gm = [w["ws0"], w["bs0"]]
    else:
        b, t = prompt_shape
        tm = 512
        assert t % tm == 0 and tm % CHUNK == 0
        tpb = t // tm
        grid = (b * tpb,)
        row = lambda i: (i, 0)
        rope_map = lambda i: (i % tpb, 0)
        gm = [w["w_s"], w["bs_full"]]
    rspec = pl.BlockSpec((tm, LANES), rope_map)
    in_specs = ([pl.BlockSpec((tm, d), row)] + wspecs + [rspec] * 3
                + [_full(a.shape) for a in gm] + [_full(a.shape) for a in tail])
    if prompt_shape is None:
        kern = _inproj_sample_kernel
        outs = [((n, a_width), BF16), ((n, b_width), F32), ((n, KV_LANES), F32), ((n, KV_LANES), F32),
                ((n, KV_LANES), F32), ((n, LANES), F32), ((n, a_width), F32)]
        out_specs = [pl.BlockSpec((tm, s[1]), row) for s, _ in outs]
    else:
        kern = _inproj_prompt_kernel
        kt_map = lambda i: (i // tpb, 0, i % tpb)
        outs = [((n, a_width), BF16), ((n, b_width), F32), ((n, KV_LANES), F32), ((n, LANES), F32)]
        out_specs = [pl.BlockSpec((tm, s[1]), row) for s, _ in outs]
        outs += [((b, KV_LANES, t), F32)] * 3
        out_specs += [pl.BlockSpec((1, KV_LANES, tm), kt_map)] * 3
        outs += [((b, K_LANES, t), BF16), ((n, N_KV * K_LANES), BF16)] * 2
        out_specs += [pl.BlockSpec((1, K_LANES, tm), kt_map), pl.BlockSpec((tm, N_KV * K_LANES), row)] * 2
    return pl.pallas_call(
        kern, grid=grid, in_specs=in_specs, out_specs=out_specs,
        out_shape=[jax.ShapeDtypeStruct(s, dt) for s, dt in outs],
        compiler_params=_params("parallel"), name="in_projection",
    )(x2, *weights, rc, rs1, rs2, *gm, *tail)


def _rope_tables(pos):
    half = ROT_DIM // 2
    inv = ROPE_THETA ** (-jnp.arange(half, dtype=F32) / half)
    ang = pos.astype(F32)[:, None] * inv[None, :]
    cos, sin = jnp.cos(ang), jnp.sin(ang)
    n = pos.shape[0]
    rest0 = jnp.zeros((n, HEAD_DIM - ROT_DIM), F32)
    zero = jnp.zeros((n, half), F32)
    rc = jnp.concatenate([cos, cos, rest0 + 1.0], axis=1)
    rs1 = jnp.concatenate([-sin, zero, rest0], axis=1)
    rs2 = jnp.concatenate([zero, sin, rest0], axis=1)
    return tuple(jnp.tile(a, (1, LANES // HEAD_DIM)) for a in (rc, rs1, rs2))


def _compress_partial(read_k, read_v, w1k_ref, w1v_ref):
    acc_k = acc_v = None
    for s in range(CMP_STRIDE):
        pk = _dot(read_k(s).astype(BF16), w1k_ref[s])
        pv = _dot(read_v(s).astype(BF16), w1v_ref[s])
        acc_k = pk if acc_k is None else acc_k + pk
        acc_v = pv if acc_v is None else acc_v + pv
    return acc_k, acc_v


def _compress_finish(fs_k, fs_v, b1_ref, w2_ref):
    hid = b1_ref.shape[1]
    outs = []
    for c, fs in enumerate((fs_k, fs_v)):
        hs = []
        for g in range(N_KV):
            first = fs[:, g * 2 * hid:g * 2 * hid + hid]
            second = fs[:, g * 2 * hid + hid:(g + 1) * 2 * hid]
            nxt = pltpu.roll(second, second.shape[0] - 1, 0)
            hs.append(jax.nn.silu(first + nxt + b1_ref[c:c + 1, :]))
        outs.append(_dot(jnp.concatenate(hs, axis=1).astype(BF16), w2_ref[c]))
    return jnp.concatenate(outs, axis=1)


def _compress_prompt_kernel(k_ref, v_ref, w1k_ref, w1v_ref, b1_ref, w2_ref, kc_ref):
    nb = kc_ref.shape[1]
    fs_k, fs_v = _compress_partial(lambda s: k_ref[0, pl.ds(s, nb, stride=CMP_STRIDE), :],
                                   lambda s: v_ref[0, pl.ds(s, nb, stride=CMP_STRIDE), :], w1k_ref, w1v_ref)
    kc_ref[0] = _compress_finish(fs_k, fs_v, b1_ref, w2_ref)


def _compress_prompt(kvc, w):
    b, t, _ = kvc.shape
    nb = t // CMP_STRIDE
    weights = [w["w1k"], w["w1v"], w["b_c1"], w["w2"]]
    return pl.pallas_call(
        _compress_prompt_kernel, grid=(b,),
        in_specs=[pl.BlockSpec((1, t, K_LANES), lambda i: (i, 0, 0)), pl.BlockSpec((1, t, K_LANES), lambda i: (i, 0, 1))]
        + [_full(a.shape) for a in weights],
        out_specs=pl.BlockSpec((1, nb, KV_LANES), lambda i: (i, 0, 0)),
        out_shape=jax.ShapeDtypeStruct((b, nb, KV_LANES), F32),
        compiler_params=_params("parallel"), name="compress_prompt",
    )(kvc, kvc, *weights)


def _stack_heads(q, g, q_per_kv):
    return jnp.concatenate([q[:, (g * q_per_kv + h) * HEAD_DIM:(g * q_per_kv + h + 1) * HEAD_DIM]
                            for h in range(q_per_kv)], axis=0)


def _cmp_branch(qg, kc, mask, reps):
    k, v = kc
    s = _dot_nt(qg, k, precision=lax.Precision.HIGHEST)
    if mask.shape[0] > 1:
        mask = jnp.concatenate([mask.astype(F32)] * reps, axis=0) > 0.5
    s = jnp.where(mask, s, MASKED)
    m = jnp.max(s, axis=-1, keepdims=True)
    e = jnp.where(mask, jnp.exp(s - m), 0.0)
    p = e / jnp.maximum(jnp.sum(e, axis=-1, keepdims=True), 1e-30)
    return p, _dot(p.astype(BF16), v.astype(BF16))


def _select_blocks(score, n_sel):
    rows, n = score.shape
    lane = lax.broadcasted_iota(jnp.int32, (rows, n), 1).astype(F32)
    sel = jnp.zeros((rows, n), F32)
    picks = []
    x = score
    for _ in range(n_sel):
        m = jnp.max(x, axis=-1, keepdims=True)
        idx = jnp.min(jnp.where(x == m, lane, float(n)), axis=-1, keepdims=True)
        hit = lane == idx
        ok = m > 0.1 * MASKED
        sel = jnp.where(hit & ok, 1.0, sel)
        x = jnp.where(hit, -3e38, x)
        picks.append((idx, ok))
    return sel, picks


def _select_mask_t(xt, n_sel):
    n, cols = xt.shape
    tiles = [xt[t * SUBLANES:(t + 1) * SUBLANES] for t in range(n // SUBLANES)]
    row = lax.broadcasted_iota(jnp.int32, (SUBLANES, cols), 0)
    ahead = [jnp.zeros((SUBLANES, cols), F32) for _ in tiles]
    for i in range(n):
        xi = xt[i:i + 1, :]
        for t, x in enumerate(tiles):
            first, last = t * SUBLANES, (t + 1) * SUBLANES - 1
            if first > i:
                inc = jnp.where(xi >= x, 1.0, 0.0)
            elif last <= i:
                inc = jnp.where(xi > x, 1.0, 0.0)
            else:
                inc = jnp.where(row + first > i, jnp.where(xi >= x, 1.0, 0.0), jnp.where(xi > x, 1.0, 0.0))
            ahead[t] = ahead[t] + inc
    ahead = jnp.concatenate(ahead, axis=0)
    return jnp.where((ahead < n_sel) & (xt > 0.1 * MASKED), 1.0, 0.0)


def _block_scores(p_slc, blk_t, n_blocks, axis=1):
    sj = lax.broadcasted_iota(jnp.int32, p_slc.shape, axis)
    causal = (sj <= blk_t) & (sj < n_blocks)
    forced = causal & ((sj == 0) | (sj >= blk_t - 1))
    score = jnp.where(forced, p_slc + FORCE_BONUS, p_slc)
    return jnp.where(causal, score, MASKED)


def _nsa_prompt_kernel(q_ref, gate_ref, kc_ref, kts_ref, vsa_ref, ktw_ref, vwa_ref, ovl_ref, exp_ref, gob_ref,
                       out_ref, s_scr, mx_scr, acc_scr, *, n_cmp, n_heads):
    blk = pl.program_id(1)
    start = blk * Q_BLOCK
    q_per_kv = n_heads // N_KV
    q = q_ref[...] * (LOG2_E * HEAD_DIM ** -0.5)
    gates = gate_ref[...]
    kc = kc_ref[0]
    ncp = kc.shape[0]
    n_blocks = ovl_ref.shape[0]
    tpos = start + lax.broadcasted_iota(jnp.int32, (Q_BLOCK, 1), 0)
    cn = lax.broadcasted_iota(jnp.int32, (1, ncp), 1)
    cmp_mask = ((cn * CMP_STRIDE + CMP_LEN - 1 <= tpos) & (cn < n_cmp)).astype(F32)
    cmp_mask = jnp.concatenate([cmp_mask] * q_per_kv, axis=0) > 0.5
    grp_lanes = [slice(g * HEAD_DIM, (g + 1) * HEAD_DIM) for g in range(N_KV)]
    qbs = [_stack_heads(q, g, q_per_kv).astype(BF16) for g in range(N_KV)]

    def normalised(acc, g):
        return (acc / pltpu.roll(acc, HEAD_DIM, 1))[:, grp_lanes[g]]

    o_cmp, p_slc = [], []
    ovl = ovl_ref[...]
    for g in range(N_KV):
        s = _dot_nt(qbs[g], kc[:, grp_lanes[g]].astype(BF16))
        s = jnp.where(cmp_mask, s, MASKED)
        e = jnp.where(cmp_mask, jnp.exp2(s - jnp.max(s, axis=-1, keepdims=True)), 0.0)
        p = e / jnp.maximum(jnp.sum(e, axis=-1, keepdims=True), 1e-30)
        o_cmp.append(_dot(p.astype(BF16), kc[:, K_LANES + g * HEAD_DIM:K_LANES + (g + 1) * HEAD_DIM].astype(BF16)))
        p_sum = p[:Q_BLOCK]
        for h in range(1, q_per_kv):
            p_sum = p_sum + p[h * Q_BLOCK:(h + 1) * Q_BLOCK]
        p_hi = p_sum.astype(BF16)
        p_lo = (p_sum - p_hi.astype(F32)).astype(BF16)
        p_slc.append(_dot_nt(ovl, p_hi) + _dot_nt(ovl, p_lo))
    blk_t = (start + lax.broadcasted_iota(jnp.int32, (1, Q_BLOCK), 1)) // SLC_BLOCK
    score = _block_scores(jnp.concatenate(p_slc, axis=1), jnp.concatenate([blk_t] * N_KV, axis=1), n_blocks, axis=0)
    sel = _select_mask_t(score, min(N_SELECT, n_blocks))
    sel = jnp.concatenate([sel, jnp.zeros((exp_ref.shape[0] - n_blocks, N_KV * Q_BLOCK), F32)], axis=0)
    selb = [sel[:, g * Q_BLOCK:(g + 1) * Q_BLOCK].T.astype(BF16) for g in range(N_KV)]

    ws = pl.multiple_of(jnp.maximum(start - WINDOW, 0), LANES)
    dpos = tpos - (ws + lax.broadcasted_iota(jnp.int32, (1, WIN_KEYS), 1))
    win_bias = jnp.where((dpos >= 0) & (dpos < WINDOW), 0.0, MASKED)
    win_bias = jnp.concatenate([win_bias] * q_per_kv, axis=0)
    o_win = []
    for g in range(N_KV):
        s = _dot(qbs[g], ktw_ref[0, grp_lanes[g], pl.ds(ws, WIN_KEYS)]) + win_bias
        p = jnp.exp2(s - jnp.max(s, axis=-1, keepdims=True))
        o_win.append(normalised(_dot(p.astype(BF16), vwa_ref[pl.ds(ws, WIN_KEYS), g * K_LANES:(g + 1) * K_LANES]), g))

    n_tiles = blk // (SLC_TILE // Q_BLOCK) + 1
    lane_tiles = SLC_TILE // LANES
    mx_scr[...] = jnp.full(mx_scr.shape, MASKED, F32)
    acc_scr[...] = jnp.zeros(acc_scr.shape, F32)

    def score_tile(kt, carry):
        off = pl.multiple_of(kt * SLC_TILE, SLC_TILE)
        causal = off + lax.broadcasted_iota(jnp.int32, (1, SLC_TILE), 1) <= tpos
        for g in range(N_KV):
            chosen = _dot(selb[g], exp_ref[:, pl.ds(off, SLC_TILE)])
            bias = jnp.where((chosen > 0.5) & causal, 0.0, MASKED)
            s = _dot(qbs[g], kts_ref[0, grp_lanes[g], pl.ds(off, SLC_TILE)]) + jnp.concatenate([bias] * q_per_kv, axis=0)
            s_scr[g, :, pl.ds(off, SLC_TILE)] = s
            m = s[:, :LANES]
            for i in range(1, lane_tiles):
                m = jnp.maximum(m, s[:, i * LANES:(i + 1) * LANES])
            mx_scr[g] = jnp.maximum(mx_scr[g], m)
        return carry

    lax.fori_loop(0, n_tiles, score_tile, 0)
    for g in range(N_KV):
        mx_scr[g] = jnp.broadcast_to(jnp.max(mx_scr[g], axis=-1, keepdims=True), mx_scr.shape[1:])

    def value_tile(kt, carry):
        off = pl.multiple_of(kt * SLC_TILE, SLC_TILE)
        for g in range(N_KV):
            p = jnp.exp2(s_scr[g, :, pl.ds(off, SLC_TILE)] - jnp.concatenate([mx_scr[g]] * lane_tiles, axis=1))
            acc_scr[g] += _dot(p.astype(BF16), vsa_ref[pl.ds(off, SLC_TILE), g * K_LANES:(g + 1) * K_LANES])
        return carry

    lax.fori_loop(0, n_tiles, value_tile, 0)

    heads = []
    for g in range(N_KV):
        o_slc = normalised(acc_scr[g], g)
        for h in range(q_per_kv):
            hd = g * q_per_kv + h
            r = slice(h * Q_BLOCK, (h + 1) * Q_BLOCK)
            heads.append(gates[:, 3 * hd:3 * hd + 1] * o_cmp[g][r] + gates[:, 3 * hd + 1:3 * hd + 2] * o_slc[r]
                         + gates[:, 3 * hd + 2:3 * hd + 3] * o_win[g][r])
    out_ref[...] = _rms(jnp.concatenate(heads, axis=1), gob_ref[...]).astype(BF16)


def _nsa_prompt(q, gates, kc, kts, vsa, ktw, vwa, w, b, t):
    n, b_width = q.shape
    n_heads = b_width // HEAD_DIM
    assert t % SLC_TILE == 0 and t >= WIN_KEYS
    nqb = t // Q_BLOCK
    ncp = kc.shape[1]
    n_cmp = ncp - 1
    n_blocks = (n_cmp + 1) * CMP_STRIDE // SLC_BLOCK
    assert n_blocks % SUBLANES == 0 and n_blocks <= LANES
    ovl = _overlap_matrix(ncp, n_blocks, n_blocks).T.astype(BF16)
    key_blk = jnp.arange(t, dtype=jnp.int32)[None, :] // SLC_BLOCK
    expand = (key_blk == jnp.arange(LANES, dtype=jnp.int32)[:, None]).astype(BF16)
    tok = lambda i, j: (i * nqb + j, 0)
    seq3 = lambda i, j: (i, 0, 0)
    seq2 = lambda i, j: (i, 0)
    rows = (n_heads // N_KV) * Q_BLOCK
    return pl.pallas_call(
        functools.partial(_nsa_prompt_kernel, n_cmp=n_cmp, n_heads=n_heads),
        grid=(b, nqb),
        in_specs=[pl.BlockSpec((Q_BLOCK, b_width), tok), pl.BlockSpec((Q_BLOCK, LANES), tok),
                  pl.BlockSpec((1, ncp, KV_LANES), seq3),
                  pl.BlockSpec((1, K_LANES, t), seq3), pl.BlockSpec((t, N_KV * K_LANES), seq2),
                  pl.BlockSpec((1, K_LANES, t), seq3), pl.BlockSpec((t, N_KV * K_LANES), seq2),
                  _full(ovl.shape), _full(expand.shape), _full(w["g_out_b"].shape)],
        out_specs=pl.BlockSpec((Q_BLOCK, b_width), tok),
        out_shape=jax.ShapeDtypeStruct((n, b_width), BF16),
        scratch_shapes=[pltpu.VMEM((N_KV, rows, t), F32), pltpu.VMEM((N_KV, rows, LANES), F32),
                        pltpu.VMEM((N_KV, rows, K_LANES), F32)],
        compiler_params=_params("parallel", "arbitrary"), name="nsa_prompt",
    )(q, gates, kc, kts, vsa, ktw, vwa, ovl, expand, w["g_out_b"])


def _overlap_matrix(rows, cols, n_blocks):
    ci = jnp.arange(rows, dtype=jnp.int32)[:, None]
    sj = jnp.arange(cols, dtype=jnp.int32)[None, :]
    hit = (ci * CMP_STRIDE < (sj + 1) * SLC_BLOCK) & (ci * CMP_STRIDE + CMP_LEN > sj * SLC_BLOCK) & (sj < n_blocks)
    return hit.astype(F32)


def _compress_sample_kernel(pt_ref, *refs, pages_per_step, n_steps):
    del pt_ref
    pages = refs[:pages_per_step]
    new_ref, w1k_ref, w1v_ref, b1_ref, w2_ref, kc_ref, fsk_scr, fsv_scr, xk_scr, xv_scr = refs[pages_per_step:]
    j = pl.program_id(1)
    blocks_per_page = PAGE_SIZE // CMP_STRIDE
    step_blocks = pages_per_step * blocks_per_page
    for i, page in enumerate(pages):
        xk_scr[pl.ds(i * PAGE_SIZE, PAGE_SIZE), :] = page[0, :K_LANES, :].T
        xv_scr[pl.ds(i * PAGE_SIZE, PAGE_SIZE), :] = page[0, K_LANES:, :].T
    fs_k, fs_v = _compress_partial(lambda s: xk_scr[pl.ds(s, step_blocks, stride=CMP_STRIDE), :],
                                   lambda s: xv_scr[pl.ds(s, step_blocks, stride=CMP_STRIDE), :], w1k_ref, w1v_ref)
    off = pl.multiple_of(j * step_blocks, step_blocks)
    fsk_scr[pl.ds(off, step_blocks), :] = fs_k
    fsv_scr[pl.ds(off, step_blocks), :] = fs_v

    @pl.when(j == n_steps - 1)
    def _():
        past_blocks = n_steps * step_blocks
        tail = fsk_scr.shape[0] - past_blocks
        new = new_ref[0]
        is_first = lax.broadcasted_iota(jnp.int32, (tail, 1), 0) == 0
        nk = _dot(new[:, :K_LANES].astype(BF16), w1k_ref[0])
        nv = _dot(new[:, K_LANES:].astype(BF16), w1v_ref[0])
        fsk_scr[pl.ds(past_blocks, tail), :] = jnp.where(is_first, nk, 0.0)
        fsv_scr[pl.ds(past_blocks, tail), :] = jnp.where(is_first, nv, 0.0)
        kc_ref[0] = _compress_finish(fsk_scr[...], fsv_scr[...], b1_ref, w2_ref)


def _compress_sample(cache_cmp_t, page_table, new_rows, w):
    db, n_pages = page_table.shape
    pages_per_step = 32
    assert n_pages % pages_per_step == 0
    n_steps = n_pages // pages_per_step
    blocks_per_page = PAGE_SIZE // CMP_STRIDE
    past_blocks = n_pages * blocks_per_page
    nbp = past_blocks + SUBLANES
    weights = [w["w1k"], w["w1v"], w["b_c1"], w["w2"]]
    hid2 = w["w1k"].shape[2]
    page_spec = lambda i: pl.BlockSpec((1, KV_LANES, PAGE_SIZE),
                                       lambda b, j, pt, i=i: (pt[b, j * pages_per_step + i], 0, 0))
    step_rows = pages_per_step * PAGE_SIZE
    grid_spec = pltpu.PrefetchScalarGridSpec(
        num_scalar_prefetch=1, grid=(db, n_steps),
        in_specs=[page_spec(i) for i in range(pages_per_step)]
        + [pl.BlockSpec((1, 1, KV_LANES), lambda b, j, pt: (b, 0, 0))]
        + [pl.BlockSpec(a.shape, lambda b, j, pt, nd=a.ndim: (0,) * nd) for a in weights],
        out_specs=pl.BlockSpec((1, nbp, KV_LANES), lambda b, j, pt: (b, 0, 0)),
        scratch_shapes=[pltpu.VMEM((nbp, hid2), F32), pltpu.VMEM((nbp, hid2), F32),
                        pltpu.VMEM((step_rows, K_LANES), F32), pltpu.VMEM((step_rows, K_LANES), F32)])
    return pl.pallas_call(
        functools.partial(_compress_sample_kernel, pages_per_step=pages_per_step, n_steps=n_steps),
        grid_spec=grid_spec, out_shape=jax.ShapeDtypeStruct((db, nbp, KV_LANES), F32),
        compiler_params=_params("parallel", "arbitrary"), name="compress_sample",
    )(page_table, *([cache_cmp_t] * pages_per_step), new_rows[:, None, :], *weights)


def _select_sample_kernel(q_ref, kc_ref, ovl_ref, ocmp_ref, idx_ref, *, pos, n_cmp, n_blocks, n_heads):
    q_per_kv = n_heads // N_KV
    q = q_ref[0] * (HEAD_DIM ** -0.5)
    kc = kc_ref[0]
    ncp = kc.shape[0]
    cn = lax.broadcasted_iota(jnp.int32, (1, ncp), 1)
    cmp_mask = (cn * CMP_STRIDE + CMP_LEN - 1 <= pos) & (cn < n_cmp)
    o_rows, scores = [], []
    for g in range(N_KV):
        qg = _stack_heads(q, g, q_per_kv)
        p_cmp, o_cmp = _cmp_branch(qg, (kc[:, g * HEAD_DIM:(g + 1) * HEAD_DIM],
                                        kc[:, K_LANES + g * HEAD_DIM:K_LANES + (g + 1) * HEAD_DIM]),
                                   cmp_mask, q_per_kv)
        o_rows.append(o_cmp)
        p_sum = jnp.sum(p_cmp, axis=0, keepdims=True)
        scores.append(jnp.dot(p_sum, ovl_ref[...], precision=lax.Precision.HIGHEST, preferred_element_type=F32))
    ocmp_ref[0] = jnp.concatenate(o_rows, axis=0)
    score = _block_scores(jnp.concatenate(scores, axis=0), pos // SLC_BLOCK, n_blocks)
    _, picks = _select_blocks(score, min(N_SELECT, n_blocks))
    lane = lax.broadcasted_iota(jnp.int32, (N_KV, LANES), 1)
    out = jnp.full((N_KV, LANES), -1.0, F32)
    for i, (idx, ok) in enumerate(picks):
        out = jnp.where((lane == i) & ok, idx, out)
    idx_ref[0] = out.astype(jnp.int32)


def _select_sample(q, kc, pos, n_cmp, n_blocks):
    db, b_width = q.shape
    n_heads = b_width // HEAD_DIM
    ncp = kc.shape[1]
    nsp = -(-n_blocks // LANES) * LANES
    ovl = _overlap_matrix(ncp, nsp, n_blocks)
    return pl.pallas_call(
        functools.partial(_select_sample_kernel, pos=pos, n_cmp=n_cmp, n_blocks=n_blocks, n_heads=n_heads),
        grid=(db,),
        in_specs=[pl.BlockSpec((1, 1, b_width), lambda i: (i, 0, 0)),
                  pl.BlockSpec((1, ncp, KV_LANES), lambda i: (i, 0, 0)), _full(ovl.shape)],
        out_specs=[pl.BlockSpec((1, n_heads, HEAD_DIM), lambda i: (i, 0, 0)),
                   pl.BlockSpec((1, N_KV, LANES), lambda i: (i, 0, 0))],
        out_shape=[jax.ShapeDtypeStruct((db, n_heads, HEAD_DIM), F32),
                   jax.ShapeDtypeStruct((db, N_KV, LANES), jnp.int32)],
        compiler_params=_params("parallel"), name="select_sample",
    )(q[:, None, :], kc, ovl)


def _attend_sample_kernel(page_ref, *refs, n_sel, past_blocks, n_heads):
    del page_ref
    n_slots = N_KV * n_sel
    pages = refs[:n_slots]
    (q_ref, gate_ref, ocmp_ref, idx_ref, newslc_ref, win_ref, newwin_ref, exp_ref, gob_ref,
     out_ref, winout_ref) = refs[n_slots:]
    q_per_kv = n_heads // N_KV
    q = q_ref[0] * (HEAD_DIM ** -0.5)
    lane = lax.broadcasted_iota(jnp.int32, (1, K_LANES), 1)
    wb = win_ref.shape[2]
    is_last = lax.broadcasted_iota(jnp.int32, (1, wb), 1) == wb - 1
    win = jnp.where(is_last, newwin_ref[0], pltpu.roll(win_ref[0], wb - 1, 1))
    winout_ref[0] = win
    win_k = win[:K_LANES].astype(BF16)
    win_v = win[K_LANES:].astype(BF16)
    new_slc = newslc_ref[0]
    idx = idx_ref[0].astype(F32)
    n_keys = n_sel * PAGE_SIZE
    key_half = (lax.broadcasted_iota(jnp.int32, (1, n_keys), 1) % PAGE_SIZE) // SLC_BLOCK
    o_slc, o_win = [], []
    for g in range(N_KV):
        qpad = jnp.concatenate(
            [jnp.where(lane // HEAD_DIM == g,
                       jnp.concatenate([q[:, (g * q_per_kv + h) * HEAD_DIM:(g * q_per_kv + h + 1) * HEAD_DIM]] * N_KV,
                                       axis=1), 0.0)
             for h in range(q_per_kv)], axis=0)
        qpb = qpad.astype(BF16)
        kt = jnp.concatenate([pages[g * n_sel + i][0, :K_LANES, :] for i in range(n_sel)], axis=1).astype(BF16)
        vt = jnp.concatenate([pages[g * n_sel + i][0, K_LANES:, :] for i in range(n_sel)], axis=1).astype(BF16)
        s = _dot(qpb, kt)
        idg = idx[g:g + 1, :]
        idk = jnp.dot(idg, exp_ref[...], precision=lax.Precision.HIGHEST, preferred_element_type=F32)
        parity = idk - 2.0 * jnp.floor(idk * 0.5)
        key_ok = (idk >= 0.0) & (idk < past_blocks) & (parity == key_half.astype(F32))
        s = jnp.where(key_ok, s, MASKED)
        has_new = jnp.max(jnp.where(idg == past_blocks, 1.0, 0.0), axis=-1, keepdims=True) > 0.5
        s_new = jnp.sum(qpad * new_slc[:, :K_LANES], axis=-1, keepdims=True)
        s_new = jnp.where(has_new, s_new, MASKED)
        m = jnp.maximum(jnp.max(s, axis=-1, keepdims=True), s_new)
        e = jnp.where(key_ok, jnp.exp(s - m), 0.0)
        e_new = jnp.where(has_new, jnp.exp(s_new - m), 0.0)
        den = jnp.maximum(jnp.sum(e, axis=-1, keepdims=True) + e_new, 1e-30)
        o = (_dot_nt(e.astype(BF16), vt) + e_new * new_slc[:, K_LANES:]) / den
        o_slc.append(o[:, g * HEAD_DIM:(g + 1) * HEAD_DIM])
        s = _dot(qpb, win_k)
        e = jnp.exp(s - jnp.max(s, axis=-1, keepdims=True))
        o = _dot_nt(e.astype(BF16), win_v) / jnp.sum(e, axis=-1, keepdims=True)
        o_win.append(o[:, g * HEAD_DIM:(g + 1) * HEAD_DIM])
    gates = gate_ref[0]
    o = (gates[:, 0:1] * ocmp_ref[0] + gates[:, 1:2] * jnp.concatenate(o_slc, axis=0)
         + gates[:, 2:3] * jnp.concatenate(o_win, axis=0))
    ms = jnp.sum(jnp.sum(o * o, axis=-1, keepdims=True), axis=0, keepdims=True) / (n_heads * HEAD_DIM)
    out_ref[0] = o * lax.rsqrt(ms + NORM_EPS) * gob_ref[...]


def _attend_sample(q, gates, o_cmp, idx, cache_slc_t, page_table, new_slc, win_t, new_win, w, past_blocks):
    db, b_width = q.shape
    n_heads = b_width // HEAD_DIM
    n_sel = min(N_SELECT, past_blocks + 1)
    sub_per_page = PAGE_SIZE // SLC_BLOCK
    jp = jnp.clip(idx[:, :, :n_sel], 0, past_blocks - 1)
    page = jnp.take_along_axis(page_table, (jp // sub_per_page).reshape(db, -1), axis=1).astype(jnp.int32)
    n_keys = n_sel * PAGE_SIZE
    expand = (jnp.arange(n_keys, dtype=jnp.int32)[None, :] // PAGE_SIZE
              == jnp.arange(LANES, dtype=jnp.int32)[:, None]).astype(F32)
    gob = w["g_out_b"].reshape(n_heads, HEAD_DIM)
    wb = win_t.shape[2]
    per_seq = lambda shape: pl.BlockSpec((1,) + shape, lambda b, pg: (b, 0, 0))
    page_spec = lambda i: pl.BlockSpec((1, KV_LANES, PAGE_SIZE), lambda b, pg, i=i: (pg[b, i], 0, 0))
    grid_spec = pltpu.PrefetchScalarGridSpec(
        num_scalar_prefetch=1, grid=(db,),
        in_specs=[page_spec(i) for i in range(N_KV * n_sel)]
        + [per_seq((1, b_width)), per_seq((n_heads, 3)), per_seq((n_heads, HEAD_DIM)), per_seq((N_KV, LANES)),
           per_seq((1, KV_LANES)), per_seq((KV_LANES, wb)), per_seq((KV_LANES, 1)),
           pl.BlockSpec(expand.shape, lambda b, pg: (0, 0)), pl.BlockSpec(gob.shape, lambda b, pg: (0, 0))],
        out_specs=[per_seq((n_heads, HEAD_DIM)), per_seq((KV_LANES, wb))])
    return pl.pallas_call(
        functools.partial(_attend_sample_kernel, n_sel=n_sel, past_blocks=past_blocks, n_heads=n_heads),
        grid_spec=grid_spec,
        out_shape=[jax.ShapeDtypeStruct((db, n_heads, HEAD_DIM), F32),
                   jax.ShapeDtypeStruct((db, KV_LANES, wb), F32)],
        compiler_params=_params("parallel"), name="attend_sample",
    )(page, *([cache_slc_t] * (N_KV * n_sel)), q[:, None, :], gates[:, :n_heads * 3].reshape(db, n_heads, 3), o_cmp,
      idx, new_slc[:, None, :], win_t, new_win[:, :, None], expand, gob)


def _ffn_kernel(x_ref, ma_ref, mb_ref, woa_ref, wob_ref, gffn_ref, wgate_ref, wup_ref, wdown_ref, gfin_ref, y_ref,
                acc_scr, xn_scr):
    c = pl.program_id(1)

    @pl.when(c == 0)
    def _():
        x = x_ref[...] + _dot(ma_ref[...], woa_ref[...]) + _dot(mb_ref[...], wob_ref[...])
        acc_scr[...] = x
        xn_scr[...] = _rms(x, gffn_ref[...]).astype(BF16)

    xn = xn_scr[...]
    hid = jax.nn.silu(_dot(xn, wgate_ref[...])) * _dot(xn, wup_ref[...])
    acc_scr[...] += _dot(hid.astype(BF16), wdown_ref[...])

    @pl.when(c == pl.num_programs(1) - 1)
    def _():
        y_ref[...] = _rms(acc_scr[...], gfin_ref[...])


def _output_ffn(x2, mix_a, mix_b, w, tm):
    n, d = x2.shape
    d_ff = w["w_down"].shape[0]
    ff_chunks = 2
    step = d_ff // ff_chunks
    assert n % tm == 0 and d_ff % ff_chunks == 0 and step % LANES == 0
    row = lambda i, c: (i, 0)
    fixed = lambda a: pl.BlockSpec(a.shape, lambda i, c: (0, 0))
    return pl.pallas_call(
        _ffn_kernel,
        grid=(n // tm, ff_chunks),
        in_specs=[pl.BlockSpec((tm, d), row), pl.BlockSpec((tm, mix_a.shape[1]), row),
                  pl.BlockSpec((tm, mix_b.shape[1]), row), fixed(w["wo_a"]), fixed(w["wo_b"]), fixed(w["g_ffn"]),
                  pl.BlockSpec((d, step), lambda i, c: (0, c)), pl.BlockSpec((d, step), lambda i, c: (0, c)),
                  pl.BlockSpec((step, d), lambda i, c: (c, 0)), fixed(w["g_final"])],
        out_specs=pl.BlockSpec((tm, d), row),
        out_shape=jax.ShapeDtypeStruct((n, d), F32),
        scratch_shapes=[pltpu.VMEM((tm, d), F32), pltpu.VMEM((tm, d), BF16)],
        compiler_params=_params("parallel", "arbitrary"), name="output_ffn",
    )(x2, mix_a, mix_b, w["wo_a"], w["wo_b"], w["g_ffn"], w["w_gate"], w["w_up"], w["w_down"], w["g_final"])


def _prepare_weights(l, g_attn, w_in, g_sgu, w_s, b_s, w_c1, b_c1, w_c2, g_out_a, g_out_b, w_out,
                     g_ffn, w_gate_up, w_down, g_final):
    a_width = g_sgu.shape[1]
    b_width = g_out_b.shape[1]
    n_heads = b_width // HEAD_DIM
    d_ff = w_down.shape[1]
    hid = b_c1.shape[2]
    o1 = 2 * a_width
    o2 = o1 + b_width
    o3 = o2 + 3 * KV_LANES
    wi = w_in[l]
    wg = jnp.pad(wi[:, o3:], ((0, 0), (0, LANES - 3 * n_heads)))
    row = lambda a: a.reshape(1, -1)
    w1 = w_c1[l].reshape(2, 2, CMP_STRIDE, HEAD_DIM, hid)
    w1 = jnp.transpose(w1, (0, 2, 3, 1, 4)).reshape(2, CMP_STRIDE, HEAD_DIM, 2 * hid)
    eye = jnp.eye(N_KV, dtype=F32)
    w1 = jnp.einsum("gh,csdn->csgdhn", eye, w1).reshape(2, CMP_STRIDE, K_LANES, N_KV * 2 * hid)
    w2 = jnp.einsum("gh,cne->cgnhe", eye, w_c2[l]).reshape(2, N_KV * hid, K_LANES)
    return {
        "g_attn": row(g_attn[l]), "wuv": wi[:, :o1].astype(BF16), "wq": wi[:, o1:o2].astype(BF16),
        "wkv": wi[:, o2:o3].astype(BF16), "wg": wg.astype(BF16),
        "g_sgu": row(g_sgu[l]), "g_out_a": row(g_out_a[l]), "g_out_b": row(g_out_b[l]),
        "w_s": w_s[l], "bs_full": jnp.repeat(b_s[l].T, LANES, axis=1),
        "ws0": row(jnp.repeat(w_s[l][:, 0, 0], LANES)), "bs0": row(jnp.repeat(b_s[l][:, 0], LANES)),
        "w1k": w1[0].astype(BF16), "w1v": w1[1].astype(BF16), "b_c1": b_c1[l], "w2": w2.astype(BF16),
        "wo_a": w_out[l][:a_width].astype(BF16), "wo_b": w_out[l][a_width:].astype(BF16),
        "g_ffn": row(g_ffn[l]), "w_gate": w_gate_up[l][:, :d_ff].astype(BF16),
        "w_up": w_gate_up[l][:, d_ff:].astype(BF16), "w_down": w_down[l].astype(BF16), "g_final": row(g_final),
    }


def kernel(x_prompt, x_sample, cache_cmp_kv, cache_slc_kv, state_win_kv, page_table, g_attn, w_in, g_sgu, w_s, b_s,
           w_c1, b_c1, w_c2, g_out_a, g_out_b, w_out, g_ffn, w_gate_up, w_down, g_final):
    depth = w_in.shape[0]
    b, t, d = x_prompt.shape
    db, t_s, _ = x_sample.shape
    assert depth == 1 and t_s == 1
    n_pages = page_table.shape[1]
    past = n_pages * PAGE_SIZE
    wb = state_win_kv.shape[2]
    assert wb == WINDOW and past % SLC_BLOCK == 0
    l = 0
    w = _prepare_weights(l, g_attn, w_in, g_sgu, w_s, b_s, w_c1, b_c1, w_c2, g_out_a, g_out_b, w_out,
                         g_ffn, w_gate_up, w_down, g_final)
    xp = x_prompt.reshape(b * t, d)
    mix_a, q, kvc, gates, kvc_t, kvs_t, kvw_t, kts, vs, ktw, vw = _in_projection(
        xp, jnp.arange(t, dtype=jnp.int32), w, prompt_shape=(b, t))
    kc = _compress_prompt(kvc.reshape(b, t, KV_LANES), w)
    mix_b = _nsa_prompt(q, gates, kc, kts, vs, ktw, vw, w, b, t)
    y_prompt = _output_ffn(xp, mix_a, mix_b, w, tm=512).reshape(b, t, d)

    xs = x_sample.reshape(db, d)
    pos_s = past + jnp.zeros((db,), jnp.int32)
    mix_a_s, q_s, kvc_s, kvs_s, kvw_s, gates_s, v_rows = _in_projection(xs, pos_s, w)
    kc_s = _compress_sample(_feature_major(cache_cmp_kv[l]), page_table, kvc_s, w)
    lp = -(-(past + t_s) // SLC_BLOCK) * SLC_BLOCK
    n_cmp_s = lp // CMP_STRIDE - 1
    n_blocks_s = (n_cmp_s + 1) * CMP_STRIDE // SLC_BLOCK
    o_cmp_s, idx_s = _select_sample(q_s, kc_s, past, n_cmp_s, n_blocks_s)
    mix_b_s, win_new_t = _attend_sample(q_s, gates_s, o_cmp_s, idx_s, _feature_major(cache_slc_kv[l]), page_table,
                                        kvs_s, _feature_major(state_win_kv[l]), kvw_s, w, past // SLC_BLOCK)
    y_sample = _output_ffn(xs, mix_a_s, mix_b_s.reshape(db, -1).astype(BF16), w, tm=db).reshape(db, t_s, d)

    kv_shape = (2, N_KV, HEAD_DIM)
    return (y_prompt, y_sample,
            _row_major(kvc_t)[None], _row_major(kvs_t)[None], _row_major(kvw_t[:, :, t - min(WINDOW, t):])[None],
            kvc_s.reshape(1, db, t_s, *kv_shape), kvs_s.reshape(1, db, t_s, *kv_shape),
            _row_major(win_new_t)[None], v_rows.reshape(1, db, t_s, -1))


def _feature_major(kv):
    n, rows = kv.shape[:2]
    return jnp.transpose(kv, (0, 2, 3, 4, 1)).reshape(n, KV_LANES, rows)


def _row_major(kv_t):
    n, _, rows = kv_t.shape
    return jnp.transpose(kv_t.reshape(n, 2, N_KV, HEAD_DIM, rows), (0, 4, 1, 2, 3))
```

```python
import functools

import jax
import jax.numpy as jnp
from jax import lax
from jax.experimental import pallas as pl
from jax.experimental.pallas import tpu as pltpu

F32 = jnp.float32
BF16 = jnp.bfloat16

A_GROUPS = 4
CHUNK = 128
HEAD_DIM = 64
N_KV = 2
ROT_DIM = HEAD_DIM // 4
ROPE_THETA = 500000.0
CMP_LEN = 32
CMP_STRIDE = 16
SLC_BLOCK = 64
N_SELECT = 16
WINDOW = 512
Q_BLOCK = 128
FORCE_BONUS = 1000.0
PAGE_SIZE = 128
NORM_EPS = 1e-6
MASKED = -1e30
LOG2_E = 1.4426950408889634

LANES = 128
SUBLANES = 8
VMEM_LIMIT_BYTES = 56 * 1024 * 1024

KV_LANES = 2 * N_KV * HEAD_DIM
K_LANES = N_KV * HEAD_DIM
CMP_STACK = 2
SLC_TILE = 512
WIN_KEYS = WINDOW + Q_BLOCK


def _rms(x, g):
    return x * lax.rsqrt(jnp.mean(x * x, axis=-1, keepdims=True) + NORM_EPS) * g


def _dot(a, b):
    return jnp.dot(a, b, preferred_element_type=F32)


def _dot_nt(a, b, precision=None):
    return lax.dot_general(a, b, (((1,), (1,)), ((), ())), precision=precision,
                           preferred_element_type=F32)


def _rope(z, rc, rs1, rs2):
    return z * rc + pltpu.roll(z, LANES - ROT_DIM // 2, 1) * rs1 + pltpu.roll(z, ROT_DIM // 2, 1) * rs2


def _project(x_ref, gattn_ref, wuv_ref, wq_ref, wkv_ref, wg_ref, rc_ref, rs1_ref, rs2_ref):
    xn = _rms(x_ref[...], gattn_ref[...]).astype(BF16)
    rc, rs1, rs2 = rc_ref[...], rs1_ref[...], rs2_ref[...]
    zuv = _dot(xn, wuv_ref[...])
    a_width = zuv.shape[1] // 2
    zq = _dot(xn, wq_ref[...])
    q = jnp.concatenate([_rope(zq[:, i * LANES:(i + 1) * LANES], rc, rs1, rs2)
                         for i in range(zq.shape[1] // LANES)], axis=1)
    zkv = _dot(xn, wkv_ref[...])
    branches = []
    for br in range(3):
        k = _rope(zkv[:, br * KV_LANES:br * KV_LANES + K_LANES], rc, rs1, rs2)
        v = zkv[:, br * KV_LANES + K_LANES:(br + 1) * KV_LANES]
        branches.append((k, v))
    gates = jax.nn.sigmoid(_dot(xn, wg_ref[...]))
    return zuv[:, :a_width], zuv[:, a_width:], q, branches, gates


def _gmlp_norm_v(v, gsgu):
    v = jax.nn.gelu(v)
    return jnp.concatenate([_rms(v[:, g * LANES:(g + 1) * LANES], gsgu[:, g * LANES:(g + 1) * LANES])
                            for g in range(A_GROUPS)], axis=1)


def _inproj_prompt_kernel(x_ref, gattn_ref, wuv_ref, wq_ref, wkv_ref, wg_ref, rc_ref, rs1_ref, rs2_ref,
                          ws_ref, bs_ref, gsgu_ref, goa_ref,
                          mixa_ref, q_ref, kvc_ref, gate_ref, kvct_ref, kvst_ref, kvwt_ref,
                          kts_ref, vs_ref, ktw_ref, vw_ref):
    u, v, q, branches, gates = _project(x_ref, gattn_ref, wuv_ref, wq_ref, wkv_ref, wg_ref,
                                        rc_ref, rs1_ref, rs2_ref)
    tm = u.shape[0]
    u = jax.nn.gelu(u)
    vg = _gmlp_norm_v(v, gsgu_ref[...]).astype(BF16)
    row = lax.broadcasted_iota(jnp.int32, (CHUNK, CHUNK), 0)
    col = lax.broadcasted_iota(jnp.int32, (CHUNK, CHUNK), 1)
    bias = bs_ref[...]
    parts = []
    for g in range(A_GROUPS):
        w = jnp.where(row >= col, ws_ref[g], 0.0).astype(BF16)
        s = jnp.concatenate(
            [_dot(w, vg[c * CHUNK:(c + 1) * CHUNK, g * LANES:(g + 1) * LANES]) for c in range(tm // CHUNK)],
            axis=0)
        s = s + jnp.concatenate([bias[:, g * LANES:(g + 1) * LANES]] * (tm // CHUNK), axis=0)
        parts.append(u[:, g * LANES:(g + 1) * LANES] * s)
    mixa_ref[...] = _rms(jnp.concatenate(parts, axis=1), goa_ref[...]).astype(BF16)
    q_ref[...] = q
    gate_ref[...] = gates
    kvc_ref[...] = jnp.concatenate(branches[0], axis=1)
    kts = []
    for ref, (k, v_) in zip((kvct_ref, kvst_ref, kvwt_ref), branches):
        kt = k.T
        ref[0, :K_LANES, :] = kt
        ref[0, K_LANES:, :] = v_.T
        kts.append(kt)
    own = [lax.broadcasted_iota(jnp.int32, (1, K_LANES), 1) // HEAD_DIM == g for g in range(N_KV)]
    for kt_ref, va_ref, br in ((kts_ref, vs_ref, 1), (ktw_ref, vw_ref, 2)):
        kt_ref[0] = kts[br].astype(BF16)
        va_ref[...] = jnp.concatenate([jnp.where(m, branches[br][1], 1.0) for m in own], axis=1).astype(BF16)


def _inproj_sample_kernel(x_ref, gattn_ref, wuv_ref, wq_ref, wkv_ref, wg_ref, rc_ref, rs1_ref, rs2_ref,
                          ws0_ref, bs0_ref, gsgu_ref, goa_ref,
                          mixa_ref, q_ref, kvc_ref, kvs_ref, kvw_ref, gate_ref, vrow_ref):
    u, v, q, branches, gates = _project(x_ref, gattn_ref, wuv_ref, wq_ref, wkv_ref, wg_ref,
                                        rc_ref, rs1_ref, rs2_ref)
    vg = _gmlp_norm_v(v, gsgu_ref[...])
    o_a = jax.nn.gelu(u) * (vg * ws0_ref[...] + bs0_ref[...])
    mixa_ref[...] = _rms(o_a, goa_ref[...]).astype(BF16)
    vrow_ref[...] = vg
    q_ref[...] = q
    for ref, (k, v_) in zip((kvc_ref, kvs_ref, kvw_ref), branches):
        ref[...] = jnp.concatenate([k, v_], axis=1)
    gate_ref[...] = gates


def _full(shape):
    return pl.BlockSpec(shape, lambda *_: (0,) * len(shape))


def _params(*sem):
    return pltpu.CompilerParams(dimension_semantics=sem, vmem_limit_bytes=VMEM_LIMIT_BYTES)


def _in_projection(x2, pos, w, *, prompt_shape=None):
    n, d = x2.shape
    rc, rs1, rs2 = _rope_tables(pos)
    a_width = w["wuv"].shape[1] // 2
    b_width = w["wq"].shape[1]
    weights = [w["g_attn"], w["wuv"], w["wq"], w["wkv"], w["wg"]]
    wspecs = [_full(a.shape) for a in weights]
    tail = [w["g_sgu"], w["g_out_a"]]
    if prompt_shape is None:
        tm, grid = n, (1,)
        row = lambda i: (i, 0)
        rope_map = row
        gm = [w["ws0"], w["bs0"]]
    else:
        b, t = prompt_shape
        tm = 512
        assert t % tm == 0 and tm % CHUNK == 0
        tpb = t // tm
        grid = (b * tpb,)
        row = lambda i: (i, 0)
        rope_map = lambda i: (i % tpb, 0)
        gm = [w["w_s"], w["bs_full"]]
    rspec = pl.BlockSpec((tm, LANES), rope_map)
    in_specs = ([pl.BlockSpec((tm, d), row)] + wspecs + [rspec] * 3
                + [_full(a.shape) for a in gm] + [_full(a.shape) for a in tail])
    if prompt_shape is None:
        kern = _inproj_sample_kernel
        outs = [((n, a_width), BF16), ((n, b_width), F32), ((n, KV_LANES), F32), ((n, KV_LANES), F32),
                ((n, KV_LANES), F32), ((n, LANES), F32), ((n, a_width), F32)]
        out_specs = [pl.BlockSpec((tm, s[1]), row) for s, _ in outs]
    else:
        kern = _inproj_prompt_kernel
        kt_map = lambda i: (i // tpb, 0, i % tpb)
        outs = [((n, a_width), BF16), ((n, b_width), F32), ((n, KV_LANES), F32), ((n, LANES), F32)]
        out_specs = [pl.BlockSpec((tm, s[1]), row) for s, _ in outs]
        outs += [((b, KV_LANES, t), F32)] * 3
        out_specs += [pl.BlockSpec((1, KV_LANES, tm), kt_map)] * 3
        outs += [((b, K_LANES, t), BF16), ((n, N_KV * K_LANES), BF16)] * 2
        out_specs += [pl.BlockSpec((1, K_LANES, tm), kt_map), pl.BlockSpec((tm, N_KV * K_LANES), row)] * 2
    return pl.pallas_call(
        kern, grid=grid, in_specs=in_specs, out_specs=out_specs,
        out_shape=[jax.ShapeDtypeStruct(s, dt) for s, dt in outs],
        compiler_params=_params("parallel"), name="in_projection",
    )(x2, *weights, rc, rs1, rs2, *gm, *tail)


def _rope_tables(pos):
    half = ROT_DIM // 2
    inv = ROPE_THETA ** (-jnp.arange(half, dtype=F32) / half)
    ang = pos.astype(F32)[:, None] * inv[None, :]
    cos, sin = jnp.cos(ang), jnp.sin(ang)
    n = pos.shape[0]
    rest0 = jnp.zeros((n, HEAD_DIM - ROT_DIM), F32)
    zero = jnp.zeros((n, half), F32)
    rc = jnp.concatenate([cos, cos, rest0 + 1.0], axis=1)
    rs1 = jnp.concatenate([-sin, zero, rest0], axis=1)
    rs2 = jnp.concatenate([zero, sin, rest0], axis=1)
    return tuple(jnp.tile(a, (1, LANES // HEAD_DIM)) for a in (rc, rs1, rs2))


def _compress_partial(read_k, read_v, w1k_ref, w1v_ref):
    acc_k = acc_v = None
    stack = w1k_ref.shape[1] // K_LANES
    for i in range(CMP_STRIDE // stack):
        rows = range(i * stack, (i + 1) * stack)
        pk = _dot(jnp.concatenate([read_k(s).astype(BF16) for s in rows], axis=1), w1k_ref[i])
        pv = _dot(jnp.concatenate([read_v(s).astype(BF16) for s in rows], axis=1), w1v_ref[i])
        acc_k = pk if acc_k is None else acc_k + pk
        acc_v = pv if acc_v is None else acc_v + pv
    return acc_k, acc_v


def _compress_finish(fs_k, fs_v, b1_ref, w2_ref):
    hid = b1_ref.shape[1]
    outs = []
    for c, fs in enumerate((fs_k, fs_v)):
        hs = []
        for g in range(N_KV):
            first = fs[:, g * 2 * hid:g * 2 * hid + hid]
            second = fs[:, g * 2 * hid + hid:(g + 1) * 2 * hid]
            nxt = pltpu.roll(second, second.shape[0] - 1, 0)
            hs.append(jax.nn.silu(first + nxt + b1_ref[c:c + 1, :]))
        outs.append(_dot(jnp.concatenate(hs, axis=1).astype(BF16), w2_ref[c]))
    return jnp.concatenate(outs, axis=1)


def _compress_prompt_kernel(k_ref, v_ref, w1k_ref, w1v_ref, b1_ref, w2_ref, kc_ref):
    nb = kc_ref.shape[1]
    fs_k, fs_v = _compress_partial(lambda s: k_ref[0, pl.ds(s, nb, stride=CMP_STRIDE), :],
                                   lambda s: v_ref[0, pl.ds(s, nb, stride=CMP_STRIDE), :], w1k_ref, w1v_ref)
    kc_ref[0] = _compress_finish(fs_k, fs_v, b1_ref, w2_ref)


def _compress_prompt(kvc, w):
    b, t, _ = kvc.shape
    nb = t // CMP_STRIDE
    weights = [w["w1k"], w["w1v"], w["b_c1"], w["w2"]]
    return pl.pallas_call(
        _compress_prompt_kernel, grid=(b,),
        in_specs=[pl.BlockSpec((1, t, K_LANES), lambda i: (i, 0, 0)), pl.BlockSpec((1, t, K_LANES), lambda i: (i, 0, 1))]
        + [_full(a.shape) for a in weights],
        out_specs=pl.BlockSpec((1, nb, KV_LANES), lambda i: (i, 0, 0)),
        out_shape=jax.ShapeDtypeStruct((b, nb, KV_LANES), F32),
        compiler_params=_params("parallel"), name="compress_prompt",
    )(kvc, kvc, *weights)


def _stack_heads(q, g, q_per_kv):
    return jnp.concatenate([q[:, (g * q_per_kv + h) * HEAD_DIM:(g * q_per_kv + h + 1) * HEAD_DIM]
                            for h in range(q_per_kv)], axis=0)


def _select_blocks(score, n_sel):
    rows, n = score.shape
    lane = lax.broadcasted_iota(jnp.int32, (rows, n), 1).astype(F32)
    sel = jnp.zeros((rows, n), F32)
    picks = []
    x = score
    for _ in range(n_sel):
        m = jnp.max(x, axis=-1, keepdims=True)
        idx = jnp.min(jnp.where(x == m, lane, float(n)), axis=-1, keepdims=True)
        hit = lane == idx
        ok = m > 0.1 * MASKED
        sel = jnp.where(hit & ok, 1.0, sel)
        x = jnp.where(hit, -3e38, x)
        picks.append((idx, ok))
    return sel, picks


def _select_mask_t(xt, n_sel):
    n, cols = xt.shape
    tiles = [xt[t * SUBLANES:(t + 1) * SUBLANES] for t in range(n // SUBLANES)]
    row = lax.broadcasted_iota(jnp.int32, (SUBLANES, cols), 0)
    ahead = [jnp.zeros((SUBLANES, cols), F32) for _ in tiles]
    for i in range(n):
        xi = xt[i:i + 1, :]
        for t, x in enumerate(tiles):
            first, last = t * SUBLANES, (t + 1) * SUBLANES - 1
            if first > i:
                inc = jnp.where(xi >= x, 1.0, 0.0)
            elif last <= i:
                inc = jnp.where(xi > x, 1.0, 0.0)
            else:
                inc = jnp.where(row + first > i, jnp.where(xi >= x, 1.0, 0.0), jnp.where(xi > x, 1.0, 0.0))
            ahead[t] = ahead[t] + inc
    ahead = jnp.concatenate(ahead, axis=0)
    return jnp.where((ahead < n_sel) & (xt > 0.1 * MASKED), 1.0, 0.0)


def _block_scores(p_slc, blk_t, n_blocks, axis=1):
    sj = lax.broadcasted_iota(jnp.int32, p_slc.shape, axis)
    causal = (sj <= blk_t) & (sj < n_blocks)
    forced = causal & ((sj == 0) | (sj >= blk_t - 1))
    score = jnp.where(forced, p_slc + FORCE_BONUS, p_slc)
    return jnp.where(causal, score, MASKED)


def _nsa_prompt_kernel(q_ref, gate_ref, kc_ref, kts_ref, vsa_ref, ktw_ref, vwa_ref, ovl_ref, exp_ref, gob_ref,
                       out_ref, s_scr, mx_scr, acc_scr, *, n_cmp, n_heads):
    blk = pl.program_id(1)
    start = blk * Q_BLOCK
    q_per_kv = n_heads // N_KV
    q = q_ref[...] * (LOG2_E * HEAD_DIM ** -0.5)
    gates = gate_ref[...]
    kc = kc_ref[0]
    ncp = kc.shape[0]
    n_blocks = ovl_ref.shape[0]
    tpos = start + lax.broadcasted_iota(jnp.int32, (Q_BLOCK, 1), 0)
    cn = lax.broadcasted_iota(jnp.int32, (1, ncp), 1)
    cmp_mask = ((cn * CMP_STRIDE + CMP_LEN - 1 <= tpos) & (cn < n_cmp)).astype(F32)
    cmp_mask = jnp.concatenate([cmp_mask] * q_per_kv, axis=0) > 0.5
    grp_lanes = [slice(g * HEAD_DIM, (g + 1) * HEAD_DIM) for g in range(N_KV)]
    qbs = [_stack_heads(q, g, q_per_kv).astype(BF16) for g in range(N_KV)]

    def normalised(acc, g):
        return (acc / pltpu.roll(acc, HEAD_DIM, 1))[:, grp_lanes[g]]

    o_cmp, p_slc = [], []
    ovl = ovl_ref[...]
    for g in range(N_KV):
        s = _dot_nt(qbs[g], kc[:, grp_lanes[g]].astype(BF16))
        s = jnp.where(cmp_mask, s, MASKED)
        e = jnp.where(cmp_mask, jnp.exp2(s - jnp.max(s, axis=-1, keepdims=True)), 0.0)
        p = e / jnp.maximum(jnp.sum(e, axis=-1, keepdims=True), 1e-30)
        o_cmp.append(_dot(p.astype(BF16), kc[:, K_LANES + g * HEAD_DIM:K_LANES + (g + 1) * HEAD_DIM].astype(BF16)))
        p_sum = p[:Q_BLOCK]
        for h in range(1, q_per_kv):
            p_sum = p_sum + p[h * Q_BLOCK:(h + 1) * Q_BLOCK]
        p_hi = p_sum.astype(BF16)
        p_lo = (p_sum - p_hi.astype(F32)).astype(BF16)
        p_slc.append(_dot_nt(ovl, p_hi) + _dot_nt(ovl, p_lo))
    blk_t = (start + lax.broadcasted_iota(jnp.int32, (1, Q_BLOCK), 1)) // SLC_BLOCK
    score = _block_scores(jnp.concatenate(p_slc, axis=1), jnp.concatenate([blk_t] * N_KV, axis=1), n_blocks, axis=0)
    sel = _select_mask_t(score, min(N_SELECT, n_blocks))
    sel = jnp.concatenate([sel, jnp.zeros((exp_ref.shape[0] - n_blocks, N_KV * Q_BLOCK), F32)], axis=0)
    selb = [sel[:, g * Q_BLOCK:(g + 1) * Q_BLOCK].T.astype(BF16) for g in range(N_KV)]

    ws = pl.multiple_of(jnp.maximum(start - WINDOW, 0), LANES)
    dpos = tpos - (ws + lax.broadcasted_iota(jnp.int32, (1, WIN_KEYS), 1))
    win_bias = jnp.where((dpos >= 0) & (dpos < WINDOW), 0.0, MASKED)
    win_bias = jnp.concatenate([win_bias] * q_per_kv, axis=0)
    o_win = []
    for g in range(N_KV):
        s = _dot(qbs[g], ktw_ref[0, grp_lanes[g], pl.ds(ws, WIN_KEYS)]) + win_bias
        p = jnp.exp2(s - jnp.max(s, axis=-1, keepdims=True))
        o_win.append(normalised(_dot(p.astype(BF16), vwa_ref[pl.ds(ws, WIN_KEYS), g * K_LANES:(g + 1) * K_LANES]), g))

    n_tiles = blk // (SLC_TILE // Q_BLOCK) + 1
    lane_tiles = SLC_TILE // LANES
    mx_scr[...] = jnp.full(mx_scr.shape, MASKED, F32)
    acc_scr[...] = jnp.zeros(acc_scr.shape, F32)

    def score_tile(kt, carry):
        off = pl.multiple_of(kt * SLC_TILE, SLC_TILE)
        causal = off + lax.broadcasted_iota(jnp.int32, (1, SLC_TILE), 1) <= tpos
        for g in range(N_KV):
            chosen = _dot(selb[g], exp_ref[:, pl.ds(off, SLC_TILE)])
            bias = jnp.where((chosen > 0.5) & causal, 0.0, MASKED)
            s = _dot(qbs[g], kts_ref[0, grp_lanes[g], pl.ds(off, SLC_TILE)]) + jnp.concatenate([bias] * q_per_kv, axis=0)
            s_scr[g, :, pl.ds(off, SLC_TILE)] = s
            m = s[:, :LANES]
            for i in range(1, lane_tiles):
                m = jnp.maximum(m, s[:, i * LANES:(i + 1) * LANES])
            mx_scr[g] = jnp.maximum(mx_scr[g], m)
        return carry

    lax.fori_loop(0, n_tiles, score_tile, 0)
    for g in range(N_KV):
        mx_scr[g] = jnp.broadcast_to(jnp.max(mx_scr[g], axis=-1, keepdims=True), mx_scr.shape[1:])

    def value_tile(kt, carry):
        off = pl.multiple_of(kt * SLC_TILE, SLC_TILE)
        for g in range(N_KV):
            p = jnp.exp2(s_scr[g, :, pl.ds(off, SLC_TILE)] - jnp.concatenate([mx_scr[g]] * lane_tiles, axis=1))
            acc_scr[g] += _dot(p.astype(BF16), vsa_ref[pl.ds(off, SLC_TILE), g * K_LANES:(g + 1) * K_LANES])
        return carry

    lax.fori_loop(0, n_tiles, value_tile, 0)

    heads = []
    for g in range(N_KV):
        o_slc = normalised(acc_scr[g], g)
        for h in range(q_per_kv):
            hd = g * q_per_kv + h
            r = slice(h * Q_BLOCK, (h + 1) * Q_BLOCK)
            heads.append(gates[:, 3 * hd:3 * hd + 1] * o_cmp[g][r] + gates[:, 3 * hd + 1:3 * hd + 2] * o_slc[r]
                         + gates[:, 3 * hd + 2:3 * hd + 3] * o_win[g][r])
    out_ref[...] = _rms(jnp.concatenate(heads, axis=1), gob_ref[...]).astype(BF16)


def _nsa_prompt(q, gates, kc, kts, vsa, ktw, vwa, w, b, t):
    n, b_width = q.shape
    n_heads = b_width // HEAD_DIM
    assert t % SLC_TILE == 0 and t >= WIN_KEYS
    nqb = t // Q_BLOCK
    ncp = kc.shape[1]
    n_cmp = ncp - 1
    n_blocks = (n_cmp + 1) * CMP_STRIDE // SLC_BLOCK
    assert n_blocks % SUBLANES == 0 and n_blocks <= LANES
    ovl = _overlap_matrix(ncp, n_blocks, n_blocks).T.astype(BF16)
    key_blk = jnp.arange(t, dtype=jnp.int32)[None, :] // SLC_BLOCK
    expand = (key_blk == jnp.arange(LANES, dtype=jnp.int32)[:, None]).astype(BF16)
    tok = lambda i, j: (i * nqb + j, 0)
    seq3 = lambda i, j: (i, 0, 0)
    seq2 = lambda i, j: (i, 0)
    rows = (n_heads // N_KV) * Q_BLOCK
    return pl.pallas_call(
        functools.partial(_nsa_prompt_kernel, n_cmp=n_cmp, n_heads=n_heads),
        grid=(b, nqb),
        in_specs=[pl.BlockSpec((Q_BLOCK, b_width), tok), pl.BlockSpec((Q_BLOCK, LANES), tok),
                  pl.BlockSpec((1, ncp, KV_LANES), seq3),
                  pl.BlockSpec((1, K_LANES, t), seq3), pl.BlockSpec((t, N_KV * K_LANES), seq2),
                  pl.BlockSpec((1, K_LANES, t), seq3), pl.BlockSpec((t, N_KV * K_LANES), seq2),
                  _full(ovl.shape), _full(expand.shape), _full(w["g_out_b"].shape)],
        out_specs=pl.BlockSpec((Q_BLOCK, b_width), tok),
        out_shape=jax.ShapeDtypeStruct((n, b_width), BF16),
        scratch_shapes=[pltpu.VMEM((N_KV, rows, t), F32), pltpu.VMEM((N_KV, rows, LANES), F32),
                        pltpu.VMEM((N_KV, rows, K_LANES), F32)],
        compiler_params=_params("parallel", "arbitrary"), name="nsa_prompt",
    )(q, gates, kc, kts, vsa, ktw, vwa, ovl, expand, w["g_out_b"])


def _overlap_matrix(rows, cols, n_blocks):
    ci = jnp.arange(rows, dtype=jnp.int32)[:, None]
    sj = jnp.arange(cols, dtype=jnp.int32)[None, :]
    hit = (ci * CMP_STRIDE < (sj + 1) * SLC_BLOCK) & (ci * CMP_STRIDE + CMP_LEN > sj * SLC_BLOCK) & (sj < n_blocks)
    return hit.astype(F32)


def _compress_sample_kernel(pt_ref, *refs, pages_per_step, n_steps):
    del pt_ref
    pages = refs[:pages_per_step]
    new_ref, w1k_ref, w1v_ref, b1_ref, w2_ref, kc_ref, fsk_scr, fsv_scr = refs[pages_per_step:pages_per_step + 8]
    chunk_scr = refs[pages_per_step + 8:]
    n_chunks = len(chunk_scr) // 2
    chunk_pages = pages_per_step // n_chunks
    j = pl.program_id(1)
    blocks_per_page = PAGE_SIZE // CMP_STRIDE
    chunk_blocks = chunk_pages * blocks_per_page
    for c in range(n_chunks):
        xk_scr, xv_scr = chunk_scr[2 * c], chunk_scr[2 * c + 1]
        for i in range(chunk_pages):
            page = pages[c * chunk_pages + i]
            xk_scr[pl.ds(i * PAGE_SIZE, PAGE_SIZE), :] = page[0, :K_LANES, :].T
            xv_scr[pl.ds(i * PAGE_SIZE, PAGE_SIZE), :] = page[0, K_LANES:, :].T
    for c in range(n_chunks):
        xk_scr, xv_scr = chunk_scr[2 * c], chunk_scr[2 * c + 1]
        fs_k, fs_v = _compress_partial(lambda s: xk_scr[pl.ds(s, chunk_blocks, stride=CMP_STRIDE), :],
                                       lambda s: xv_scr[pl.ds(s, chunk_blocks, stride=CMP_STRIDE), :],
                                       w1k_ref, w1v_ref)
        off = pl.multiple_of((j * n_chunks + c) * chunk_blocks, chunk_blocks)
        fsk_scr[pl.ds(off, chunk_blocks), :] = fs_k
        fsv_scr[pl.ds(off, chunk_blocks), :] = fs_v

    @pl.when(j == n_steps - 1)
    def _():
        past_blocks = n_steps * pages_per_step * blocks_per_page
        tail = fsk_scr.shape[0] - past_blocks
        new = new_ref[0]
        is_first = lax.broadcasted_iota(jnp.int32, (tail, 1), 0) == 0
        nk = _dot(new[:, :K_LANES].astype(BF16), w1k_ref[0, :K_LANES, :])
        nv = _dot(new[:, K_LANES:].astype(BF16), w1v_ref[0, :K_LANES, :])
        fsk_scr[pl.ds(past_blocks, tail), :] = jnp.where(is_first, nk, 0.0)
        fsv_scr[pl.ds(past_blocks, tail), :] = jnp.where(is_first, nv, 0.0)
        kc_ref[0] = _compress_finish(fsk_scr[...], fsv_scr[...], b1_ref, w2_ref)


def _compress_sample(cache_cmp_t, page_table, new_rows, w):
    db, n_pages = page_table.shape
    pages_per_step, n_chunks = 64, 2
    assert n_pages % pages_per_step == 0
    n_steps = n_pages // pages_per_step
    blocks_per_page = PAGE_SIZE // CMP_STRIDE
    past_blocks = n_pages * blocks_per_page
    nbp = past_blocks + SUBLANES
    weights = [w["w1k"], w["w1v"], w["b_c1"], w["w2"]]
    hid2 = w["w1k"].shape[2]
    page_spec = lambda i: pl.BlockSpec((1, KV_LANES, PAGE_SIZE),
                                       lambda b, j, pt, i=i: (pt[b, j * pages_per_step + i], 0, 0))
    chunk_rows = pages_per_step // n_chunks * PAGE_SIZE
    grid_spec = pltpu.PrefetchScalarGridSpec(
        num_scalar_prefetch=1, grid=(db, n_steps),
        in_specs=[page_spec(i) for i in range(pages_per_step)]
        + [pl.BlockSpec((1, 1, KV_LANES), lambda b, j, pt: (b, 0, 0))]
        + [pl.BlockSpec(a.shape, lambda b, j, pt, nd=a.ndim: (0,) * nd) for a in weights],
        out_specs=pl.BlockSpec((1, nbp, KV_LANES), lambda b, j, pt: (b, 0, 0)),
        scratch_shapes=[pltpu.VMEM((nbp, hid2), F32), pltpu.VMEM((nbp, hid2), F32)]
        + [pltpu.VMEM((chunk_rows, K_LANES), F32)] * (2 * n_chunks))
    return pl.pallas_call(
        functools.partial(_compress_sample_kernel, pages_per_step=pages_per_step, n_steps=n_steps),
        grid_spec=grid_spec, out_shape=jax.ShapeDtypeStruct((db, nbp, KV_LANES), F32),
        compiler_params=_params("parallel", "arbitrary"), name="compress_sample",
    )(page_table, *([cache_cmp_t] * pages_per_step), new_rows[:, None, :], *weights)


def _select_sample_kernel(q_ref, kc_ref, ocmp_ref, psum_ref, *, pos, n_cmp, n_heads):
    q_per_kv = n_heads // N_KV
    q = q_ref[0] * (LOG2_E * HEAD_DIM ** -0.5)
    kc = kc_ref[0]
    ncp = kc.shape[0]
    cn = lax.broadcasted_iota(jnp.int32, (1, ncp), 1)
    mask = (cn * CMP_STRIDE + CMP_LEN - 1 <= pos) & (cn < n_cmp)
    o_rows, p_rows = [], []
    for g in range(N_KV):
        qg = _stack_heads(q, g, q_per_kv).astype(BF16)
        s = jnp.where(mask, _dot_nt(qg, kc[:, g * HEAD_DIM:(g + 1) * HEAD_DIM].astype(BF16)), MASKED)
        e = jnp.where(mask, jnp.exp2(s - jnp.max(s, axis=-1, keepdims=True)), 0.0)
        p = e / jnp.maximum(jnp.sum(e, axis=-1, keepdims=True), 1e-30)
        o_rows.append(_dot(p.astype(BF16), kc[:, K_LANES + g * HEAD_DIM:K_LANES + (g + 1) * HEAD_DIM].astype(BF16)))
        p_rows.append(jnp.sum(p, axis=0, keepdims=True))
    ocmp_ref[0] = jnp.concatenate(o_rows, axis=0)
    psum_ref[0] = jnp.concatenate(p_rows, axis=0)


def _pick_sample_kernel(psum_ref, ovl_ref, idx_ref, *, pos, n_blocks):
    p = psum_ref[...]
    p_hi = p.astype(BF16)
    p_lo = (p - p_hi.astype(F32)).astype(BF16)
    ovl = ovl_ref[...]
    score = _block_scores(_dot(p_hi, ovl) + _dot(p_lo, ovl), pos // SLC_BLOCK, n_blocks)
    _, picks = _select_blocks(score, min(N_SELECT, n_blocks))
    lane = lax.broadcasted_iota(jnp.int32, idx_ref.shape, 1)
    out = jnp.full(idx_ref.shape, -1.0, F32)
    for i, (idx, ok) in enumerate(picks):
        out = jnp.where((lane == i) & ok, idx, out)
    idx_ref[...] = out.astype(jnp.int32)


def _select_sample(q, kc, pos, n_cmp, n_blocks):
    db, b_width = q.shape
    n_heads = b_width // HEAD_DIM
    ncp = kc.shape[1]
    o_cmp, p_sum = pl.pallas_call(
        functools.partial(_select_sample_kernel, pos=pos, n_cmp=n_cmp, n_heads=n_heads),
        grid=(db,),
        in_specs=[pl.BlockSpec((1, 1, b_width), lambda i: (i, 0, 0)),
                  pl.BlockSpec((1, ncp, KV_LANES), lambda i: (i, 0, 0))],
        out_specs=[pl.BlockSpec((1, n_heads, HEAD_DIM), lambda i: (i, 0, 0)),
                   pl.BlockSpec((1, N_KV, ncp), lambda i: (i, 0, 0))],
        out_shape=[jax.ShapeDtypeStruct((db, n_heads, HEAD_DIM), F32),
                   jax.ShapeDtypeStruct((db, N_KV, ncp), F32)],
        compiler_params=_params("parallel"), name="select_sample",
    )(q[:, None, :], kc)
    nsp = -(-n_blocks // LANES) * LANES
    ovl = _overlap_matrix(ncp, nsp, n_blocks).astype(BF16)
    idx = pl.pallas_call(
        functools.partial(_pick_sample_kernel, pos=pos, n_blocks=n_blocks),
        grid=(1,),
        in_specs=[_full((db * N_KV, ncp)), _full(ovl.shape)],
        out_specs=_full((db * N_KV, LANES)),
        out_shape=jax.ShapeDtypeStruct((db * N_KV, LANES), jnp.int32),
        compiler_params=_params("arbitrary"), name="pick_sample",
    )(p_sum.reshape(db * N_KV, ncp), ovl)
    return o_cmp, idx.reshape(db, N_KV, LANES)


def _attend_sample_kernel(page_ref, *refs, n_sel, past_blocks, n_heads):
    del page_ref
    n_slots = N_KV * n_sel
    pages = refs[:n_slots]
    (q_ref, gate_ref, ocmp_ref, idx_ref, newslc_ref, win_ref, newwin_ref, exp_ref, gob_ref,
     out_ref, winout_ref) = refs[n_slots:]
    q_per_kv = n_heads // N_KV
    q = q_ref[0] * (HEAD_DIM ** -0.5)
    lane = lax.broadcasted_iota(jnp.int32, (1, K_LANES), 1)
    wb = win_ref.shape[2]
    is_last = lax.broadcasted_iota(jnp.int32, (1, wb), 1) == wb - 1
    win = jnp.where(is_last, newwin_ref[0], pltpu.roll(win_ref[0], wb - 1, 1))
    winout_ref[0] = win
    win_k = win[:K_LANES].astype(BF16)
    win_v = win[K_LANES:].astype(BF16)
    new_slc = newslc_ref[0]
    idx = idx_ref[0].astype(F32)
    n_keys = n_sel * PAGE_SIZE
    key_half = (lax.broadcasted_iota(jnp.int32, (1, n_keys), 1) % PAGE_SIZE) // SLC_BLOCK
    o_slc, o_win = [], []
    for g in range(N_KV):
        qpad = jnp.concatenate(
            [jnp.where(lane // HEAD_DIM == g,
                       jnp.concatenate([q[:, (g * q_per_kv + h) * HEAD_DIM:(g * q_per_kv + h + 1) * HEAD_DIM]] * N_KV,
                                       axis=1), 0.0)
             for h in range(q_per_kv)], axis=0)
        qpb = qpad.astype(BF16)
        kt = jnp.concatenate([pages[g * n_sel + i][0, :K_LANES, :] for i in range(n_sel)], axis=1).astype(BF16)
        vt = jnp.concatenate([pages[g * n_sel + i][0, K_LANES:, :] for i in range(n_sel)], axis=1).astype(BF16)
        s = _dot(qpb, kt)
        idg = idx[g:g + 1, :]
        idk = jnp.dot(idg, exp_ref[...], precision=lax.Precision.HIGHEST, preferred_element_type=F32)
        parity = idk - 2.0 * jnp.floor(idk * 0.5)
        key_ok = (idk >= 0.0) & (idk < past_blocks) & (parity == key_half.astype(F32))
        s = jnp.where(key_ok, s, MASKED)
        has_new = jnp.max(jnp.where(idg == past_blocks, 1.0, 0.0), axis=-1, keepdims=True) > 0.5
        s_new = jnp.sum(qpad * new_slc[:, :K_LANES], axis=-1, keepdims=True)
        s_new = jnp.where(has_new, s_new, MASKED)
        m = jnp.maximum(jnp.max(s, axis=-1, keepdims=True), s_new)
        e = jnp.where(key_ok, jnp.exp(s - m), 0.0)
        e_new = jnp.where(has_new, jnp.exp(s_new - m), 0.0)
        den = jnp.maximum(jnp.sum(e, axis=-1, keepdims=True) + e_new, 1e-30)
        o = (_dot_nt(e.astype(BF16), vt) + e_new * new_slc[:, K_LANES:]) / den
        o_slc.append(o[:, g * HEAD_DIM:(g + 1) * HEAD_DIM])
        s = _dot(qpb, win_k)
        e = jnp.exp(s - jnp.max(s, axis=-1, keepdims=True))
        o = _dot_nt(e.astype(BF16), win_v) / jnp.sum(e, axis=-1, keepdims=True)
        o_win.append(o[:, g * HEAD_DIM:(g + 1) * HEAD_DIM])
    gates = gate_ref[0]
    o = (gates[:, 0:1] * ocmp_ref[0] + gates[:, 1:2] * jnp.concatenate(o_slc, axis=0)
         + gates[:, 2:3] * jnp.concatenate(o_win, axis=0))
    ms = jnp.sum(jnp.sum(o * o, axis=-1, keepdims=True), axis=0, keepdims=True) / (n_heads * HEAD_DIM)
    out_ref[0] = o * lax.rsqrt(ms + NORM_EPS) * gob_ref[...]


def _attend_sample(q, gates, o_cmp, idx, cache_slc_t, page_table, new_slc, win_t, new_win, w, past_blocks):
    db, b_width = q.shape
    n_heads = b_width // HEAD_DIM
    n_sel = min(N_SELECT, past_blocks + 1)
    sub_per_page = PAGE_SIZE // SLC_BLOCK
    jp = jnp.clip(idx[:, :, :n_sel], 0, past_blocks - 1)
    page = jnp.take_along_axis(page_table, (jp // sub_per_page).reshape(db, -1), axis=1).astype(jnp.int32)
    n_keys = n_sel * PAGE_SIZE
    expand = (jnp.arange(n_keys, dtype=jnp.int32)[None, :] // PAGE_SIZE
              == jnp.arange(LANES, dtype=jnp.int32)[:, None]).astype(F32)
    gob = w["g_out_b"].reshape(n_heads, HEAD_DIM)
    wb = win_t.shape[2]
    per_seq = lambda shape: pl.BlockSpec((1,) + shape, lambda b, pg: (b, 0, 0))
    page_spec = lambda i: pl.BlockSpec((1, KV_LANES, PAGE_SIZE), lambda b, pg, i=i: (pg[b, i], 0, 0))
    grid_spec = pltpu.PrefetchScalarGridSpec(
        num_scalar_prefetch=1, grid=(db,),
        in_specs=[page_spec(i) for i in range(N_KV * n_sel)]
        + [per_seq((1, b_width)), per_seq((n_heads, 3)), per_seq((n_heads, HEAD_DIM)), per_seq((N_KV, LANES)),
           per_seq((1, KV_LANES)), per_seq((KV_LANES, wb)), per_seq((KV_LANES, 1)),
           pl.BlockSpec(expand.shape, lambda b, pg: (0, 0)), pl.BlockSpec(gob.shape, lambda b, pg: (0, 0))],
        out_specs=[per_seq((n_heads, HEAD_DIM)), per_seq((KV_LANES, wb))])
    return pl.pallas_call(
        functools.partial(_attend_sample_kernel, n_sel=n_sel, past_blocks=past_blocks, n_heads=n_heads),
        grid_spec=grid_spec,
        out_shape=[jax.ShapeDtypeStruct((db, n_heads, HEAD_DIM), F32),
                   jax.ShapeDtypeStruct((db, KV_LANES, wb), F32)],
        compiler_params=_params("parallel"), name="attend_sample",
    )(page, *([cache_slc_t] * (N_KV * n_sel)), q[:, None, :], gates[:, :n_heads * 3].reshape(db, n_heads, 3), o_cmp,
      idx, new_slc[:, None, :], win_t, new_win[:, :, None], expand, gob)


def _ffn_kernel(x_ref, ma_ref, mb_ref, woa_ref, wob_ref, gffn_ref, wgate_ref, wup_ref, wdown_ref, gfin_ref, y_ref,
                acc_scr, xn_scr):
    c = pl.program_id(1)

    @pl.when(c == 0)
    def _():
        x = x_ref[...] + _dot(ma_ref[...], woa_ref[...]) + _dot(mb_ref[...], wob_ref[...])
        acc_scr[...] = x
        xn_scr[...] = _rms(x, gffn_ref[...]).astype(BF16)

    xn = xn_scr[...]
    hid = jax.nn.silu(_dot(xn, wgate_ref[...])) * _dot(xn, wup_ref[...])
    acc_scr[...] += _dot(hid.astype(BF16), wdown_ref[...])

    @pl.when(c == pl.num_programs(1) - 1)
    def _():
        y_ref[...] = _rms(acc_scr[...], gfin_ref[...])


def _output_ffn(x2, mix_a, mix_b, w, tm):
    n, d = x2.shape
    d_ff = w["w_down"].shape[0]
    ff_chunks = 2
    step = d_ff // ff_chunks
    assert n % tm == 0 and d_ff % ff_chunks == 0 and step % LANES == 0
    row = lambda i, c: (i, 0)
    fixed = lambda a: pl.BlockSpec(a.shape, lambda i, c: (0, 0))
    return pl.pallas_call(
        _ffn_kernel,
        grid=(n // tm, ff_chunks),
        in_specs=[pl.BlockSpec((tm, d), row), pl.BlockSpec((tm, mix_a.shape[1]), row),
                  pl.BlockSpec((tm, mix_b.shape[1]), row), fixed(w["wo_a"]), fixed(w["wo_b"]), fixed(w["g_ffn"]),
                  pl.BlockSpec((d, step), lambda i, c: (0, c)), pl.BlockSpec((d, step), lambda i, c: (0, c)),
                  pl.BlockSpec((step, d), lambda i, c: (c, 0)), fixed(w["g_final"])],
        out_specs=pl.BlockSpec((tm, d), row),
        out_shape=jax.ShapeDtypeStruct((n, d), F32),
        scratch_shapes=[pltpu.VMEM((tm, d), F32), pltpu.VMEM((tm, d), BF16)],
        compiler_params=_params("parallel", "arbitrary"), name="output_ffn",
    )(x2, mix_a, mix_b, w["wo_a"], w["wo_b"], w["g_ffn"], w["w_gate"], w["w_up"], w["w_down"], w["g_final"])


def _prepare_weights(l, g_attn, w_in, g_sgu, w_s, b_s, w_c1, b_c1, w_c2, g_out_a, g_out_b, w_out,
                     g_ffn, w_gate_up, w_down, g_final):
    a_width = g_sgu.shape[1]
    b_width = g_out_b.shape[1]
    n_heads = b_width // HEAD_DIM
    d_ff = w_down.shape[1]
    hid = b_c1.shape[2]
    o1 = 2 * a_width
    o2 = o1 + b_width
    o3 = o2 + 3 * KV_LANES
    wi = w_in[l]
    wg = jnp.pad(wi[:, o3:], ((0, 0), (0, LANES - 3 * n_heads)))
    row = lambda a: a.reshape(1, -1)
    w1 = w_c1[l].reshape(2, 2, CMP_STRIDE, HEAD_DIM, hid)
    w1 = jnp.transpose(w1, (0, 2, 3, 1, 4)).reshape(2, CMP_STRIDE, HEAD_DIM, 2 * hid)
    eye = jnp.eye(N_KV, dtype=F32)
    w1 = jnp.einsum("gh,csdn->csgdhn", eye, w1).reshape(2, CMP_STRIDE // CMP_STACK, CMP_STACK * K_LANES, N_KV * 2 * hid)
    w2 = jnp.einsum("gh,cne->cgnhe", eye, w_c2[l]).reshape(2, N_KV * hid, K_LANES)
    return {
        "g_attn": row(g_attn[l]), "wuv": wi[:, :o1].astype(BF16), "wq": wi[:, o1:o2].astype(BF16),
        "wkv": wi[:, o2:o3].astype(BF16), "wg": wg.astype(BF16),
        "g_sgu": row(g_sgu[l]), "g_out_a": row(g_out_a[l]), "g_out_b": row(g_out_b[l]),
        "w_s": w_s[l], "bs_full": jnp.repeat(b_s[l].T, LANES, axis=1),
        "ws0": row(jnp.repeat(w_s[l][:, 0, 0], LANES)), "bs0": row(jnp.repeat(b_s[l][:, 0], LANES)),
        "w1k": w1[0].astype(BF16), "w1v": w1[1].astype(BF16), "b_c1": b_c1[l], "w2": w2.astype(BF16),
        "wo_a": w_out[l][:a_width].astype(BF16), "wo_b": w_out[l][a_width:].astype(BF16),
        "g_ffn": row(g_ffn[l]), "w_gate": w_gate_up[l][:, :d_ff].astype(BF16),
        "w_up": w_gate_up[l][:, d_ff:].astype(BF16), "w_down": w_down[l].astype(BF16), "g_final": row(g_final),
    }


def kernel(x_prompt, x_sample, cache_cmp_kv, cache_slc_kv, state_win_kv, page_table, g_attn, w_in, g_sgu, w_s, b_s,
           w_c1, b_c1, w_c2, g_out_a, g_out_b, w_out, g_ffn, w_gate_up, w_down, g_final):
    depth = w_in.shape[0]
    b, t, d = x_prompt.shape
    db, t_s, _ = x_sample.shape
    assert depth == 1 and t_s == 1
    n_pages = page_table.shape[1]
    past = n_pages * PAGE_SIZE
    wb = state_win_kv.shape[2]
    assert wb == WINDOW and past % SLC_BLOCK == 0
    l = 0
    w = _prepare_weights(l, g_attn, w_in, g_sgu, w_s, b_s, w_c1, b_c1, w_c2, g_out_a, g_out_b, w_out,
                         g_ffn, w_gate_up, w_down, g_final)
    xp = x_prompt.reshape(b * t, d)
    mix_a, q, kvc, gates, kvc_t, kvs_t, kvw_t, kts, vs, ktw, vw = _in_projection(
        xp, jnp.arange(t, dtype=jnp.int32), w, prompt_shape=(b, t))
    kc = _compress_prompt(kvc.reshape(b, t, KV_LANES), w)
    mix_b = _nsa_prompt(q, gates, kc, kts, vs, ktw, vw, w, b, t)
    y_prompt = _output_ffn(xp, mix_a, mix_b, w, tm=512).reshape(b, t, d)

    xs = x_sample.reshape(db, d)
    pos_s = past + jnp.zeros((db,), jnp.int32)
    mix_a_s, q_s, kvc_s, kvs_s, kvw_s, gates_s, v_rows = _in_projection(xs, pos_s, w)
    kc_s = _compress_sample(_feature_major(cache_cmp_kv[l]), page_table, kvc_s, w)
    lp = -(-(past + t_s) // SLC_BLOCK) * SLC_BLOCK
    n_cmp_s = lp // CMP_STRIDE - 1
    n_blocks_s = (n_cmp_s + 1) * CMP_STRIDE // SLC_BLOCK
    o_cmp_s, idx_s = _select_sample(q_s, kc_s, past, n_cmp_s, n_blocks_s)
    mix_b_s, win_new_t = _attend_sample(q_s, gates_s, o_cmp_s, idx_s, _feature_major(cache_slc_kv[l]), page_table,
                                        kvs_s, _feature_major(state_win_kv[l]), kvw_s, w, past // SLC_BLOCK)
    y_sample = _output_ffn(xs, mix_a_s, mix_b_s.reshape(db, -1).astype(BF16), w, tm=db).reshape(db, t_s, d)

    kv_shape = (2, N_KV, HEAD_DIM)
    return (y_prompt, y_sample,
            _row_major(kvc_t)[None], _row_major(kvs_t)[None], _row_major(kvw_t[:, :, t - min(WINDOW, t):])[None],
            kvc_s.reshape(1, db, t_s, *kv_shape), kvs_s.reshape(1, db, t_s, *kv_shape),
            _row_major(win_new_t)[None], v_rows.reshape(1, db, t_s, -1))


def _feature_major(kv):
    n, rows = kv.shape[:2]
    return jnp.transpose(kv, (0, 2, 3, 4, 1)).reshape(n, KV_LANES, rows)


def _row_major(kv_t):
    n, _, rows = kv_t.shape
    return jnp.transpose(kv_t.reshape(n, 2, N_KV, HEAD_DIM, rows), (0, 4, 1, 2, 3))
```

```python
import functools

import jax
import jax.numpy as jnp
from jax import lax
from jax.experimental import pallas as pl
from jax.experimental.pallas import tpu as pltpu

F32 = jnp.float32
BF16 = jnp.bfloat16

A_GROUPS = 4
CHUNK = 128
HEAD_DIM = 64
N_KV = 2
ROT_DIM = HEAD_DIM // 4
ROPE_THETA = 500000.0
CMP_LEN = 32
CMP_STRIDE = 16
SLC_BLOCK = 64
N_SELECT = 16
WINDOW = 512
Q_BLOCK = 128
FORCE_BONUS = 1000.0
PAGE_SIZE = 128
NORM_EPS = 1e-6
MASKED = -1e30
LOG2_E = 1.4426950408889634
SEL_BONUS = 16384.0

LANES = 128
SUBLANES = 8
VMEM_LIMIT_BYTES = 56 * 1024 * 1024

KV_LANES = 2 * N_KV * HEAD_DIM
K_LANES = N_KV * HEAD_DIM
CMP_STACK = 2
SLC_TILE = 512
WIN_KEYS = WINDOW + Q_BLOCK


def _rms(x, g):
    return x * lax.rsqrt(jnp.mean(x * x, axis=-1, keepdims=True) + NORM_EPS) * g


def _dot(a, b):
    return jnp.dot(a, b, preferred_element_type=F32)


def _dot_nt(a, b, precision=None):
    return lax.dot_general(a, b, (((1,), (1,)), ((), ())), precision=precision,
                           preferred_element_type=F32)


def _rope(z, rc, rs1, rs2):
    return z * rc + pltpu.roll(z, LANES - ROT_DIM // 2, 1) * rs1 + pltpu.roll(z, ROT_DIM // 2, 1) * rs2


def _project(x_ref, gattn_ref, wuv_ref, wq_ref, wkv_ref, wg_ref, rc_ref, rs1_ref, rs2_ref):
    xn = _rms(x_ref[...], gattn_ref[...]).astype(BF16)
    rc, rs1, rs2 = rc_ref[...], rs1_ref[...], rs2_ref[...]
    zuv = _dot(xn, wuv_ref[...])
    a_width = zuv.shape[1] // 2
    zq = _dot(xn, wq_ref[...])
    q = jnp.concatenate([_rope(zq[:, i * LANES:(i + 1) * LANES], rc, rs1, rs2)
                         for i in range(zq.shape[1] // LANES)], axis=1)
    zkv = _dot(xn, wkv_ref[...])
    branches = []
    for br in range(3):
        k = _rope(zkv[:, br * KV_LANES:br * KV_LANES + K_LANES], rc, rs1, rs2)
        v = zkv[:, br * KV_LANES + K_LANES:(br + 1) * KV_LANES]
        branches.append((k, v))
    gates = jax.nn.sigmoid(_dot(xn, wg_ref[...]))
    return zuv[:, :a_width], zuv[:, a_width:], q, branches, gates


def _gmlp_norm_v(v, gsgu):
    v = jax.nn.gelu(v)
    return jnp.concatenate([_rms(v[:, g * LANES:(g + 1) * LANES], gsgu[:, g * LANES:(g + 1) * LANES])
                            for g in range(A_GROUPS)], axis=1)


def _inproj_prompt_kernel(x_ref, gattn_ref, wuv_ref, wq_ref, wkv_ref, wg_ref, rc_ref, rs1_ref, rs2_ref,
                          ws_ref, bs_ref, gsgu_ref, goa_ref,
                          mixa_ref, q_ref, kvc_ref, gate_ref, kvct_ref, kvst_ref, kvwt_ref,
                          kts_ref, vs_ref, ktw_ref, vw_ref):
    u, v, q, branches, gates = _project(x_ref, gattn_ref, wuv_ref, wq_ref, wkv_ref, wg_ref,
                                        rc_ref, rs1_ref, rs2_ref)
    tm = u.shape[0]
    u = jax.nn.gelu(u)
    vg = _gmlp_norm_v(v, gsgu_ref[...]).astype(BF16)
    row = lax.broadcasted_iota(jnp.int32, (CHUNK, CHUNK), 0)
    col = lax.broadcasted_iota(jnp.int32, (CHUNK, CHUNK), 1)
    bias = bs_ref[...]
    parts = []
    for g in range(A_GROUPS):
        w = jnp.where(row >= col, ws_ref[g], 0.0).astype(BF16)
        s = jnp.concatenate(
            [_dot(w, vg[c * CHUNK:(c + 1) * CHUNK, g * LANES:(g + 1) * LANES]) for c in range(tm // CHUNK)],
            axis=0)
        s = s + jnp.concatenate([bias[:, g * LANES:(g + 1) * LANES]] * (tm // CHUNK), axis=0)
        parts.append(u[:, g * LANES:(g + 1) * LANES] * s)
    mixa_ref[...] = _rms(jnp.concatenate(parts, axis=1), goa_ref[...]).astype(BF16)
    q_ref[...] = q
    gate_ref[...] = gates
    kvc_ref[...] = jnp.concatenate(branches[0], axis=1)
    kts = []
    for ref, (k, v_) in zip((kvct_ref, kvst_ref, kvwt_ref), branches):
        kt = k.T
        ref[0, :K_LANES, :] = kt
        ref[0, K_LANES:, :] = v_.T
        kts.append(kt)
    own = [lax.broadcasted_iota(jnp.int32, (1, K_LANES), 1) // HEAD_DIM == g for g in range(N_KV)]
    for kt_ref, va_ref, br in ((kts_ref, vs_ref, 1), (ktw_ref, vw_ref, 2)):
        kt_ref[0] = kts[br].astype(BF16)
        va_ref[...] = jnp.concatenate([jnp.where(m, branches[br][1], 1.0) for m in own], axis=1).astype(BF16)


def _inproj_sample_kernel(x_ref, gattn_ref, wuv_ref, wq_ref, wkv_ref, wg_ref, rc_ref, rs1_ref, rs2_ref,
                          ws0_ref, bs0_ref, gsgu_ref, goa_ref,
                          mixa_ref, q_ref, kvc_ref, kvs_ref, kvw_ref, gate_ref, vrow_ref):
    u, v, q, branches, gates = _project(x_ref, gattn_ref, wuv_ref, wq_ref, wkv_ref, wg_ref,
                                        rc_ref, rs1_ref, rs2_ref)
    vg = _gmlp_norm_v(v, gsgu_ref[...])
    o_a = jax.nn.gelu(u) * (vg * ws0_ref[...] + bs0_ref[...])
    mixa_ref[...] = _rms(o_a, goa_ref[...]).astype(BF16)
    vrow_ref[...] = vg
    q_ref[...] = q
    for ref, (k, v_) in zip((kvc_ref, kvs_ref, kvw_ref), branches):
        ref[...] = jnp.concatenate([k, v_], axis=1)
    gate_ref[...] = gates


def _full(shape):
    return pl.BlockSpec(shape, lambda *_: (0,) * len(shape))


def _params(*sem):
    return pltpu.CompilerParams(dimension_semantics=sem, vmem_limit_bytes=VMEM_LIMIT_BYTES)


def _in_projection(x2, pos, w, *, prompt_shape=None):
    n, d = x2.shape
    rc, rs1, rs2 = _rope_tables(pos)
    a_width = w["wuv"].shape[1] // 2
    b_width = w["wq"].shape[1]
    weights = [w["g_attn"], w["wuv"], w["wq"], w["wkv"], w["wg"]]
    wspecs = [_full(a.shape) for a in weights]
    tail = [w["g_sgu"], w["g_out_a"]]
    if prompt_shape is None:
        tm, grid = n, (1,)
        row = lambda i: (i, 0)
        rope_map = row
        gm = [w["ws0"], w["bs0"]]
    else:
        b, t = prompt_shape
        tm = 512
        assert t % tm == 0 and tm % CHUNK == 0
        tpb = t // tm
        grid = (b * tpb,)
        row = lambda i: (i, 0)
        rope_map = lambda i: (i % tpb, 0)
        gm = [w["w_s"], w["bs_full"]]
    rspec = pl.BlockSpec((tm, LANES), rope_map)
    in_specs = ([pl.BlockSpec((tm, d), row)] + wspecs + [rspec] * 3
                + [_full(a.shape) for a in gm] + [_full(a.shape) for a in tail])
    if prompt_shape is None:
        kern = _inproj_sample_kernel
        outs = [((n, a_width), BF16), ((n, b_width), F32), ((n, KV_LANES), F32), ((n, KV_LANES), F32),
                ((n, KV_LANES), F32), ((n, LANES), F32), ((n, a_width), F32)]
        out_specs = [pl.BlockSpec((tm, s[1]), row) for s, _ in outs]
    else:
        kern = _inproj_prompt_kernel
        kt_map = lambda i: (i // tpb, 0, i % tpb)
        outs = [((n, a_width), BF16), ((n, b_width), F32), ((n, KV_LANES), F32), ((n, LANES), F32)]
        out_specs = [pl.BlockSpec((tm, s[1]), row) for s, _ in outs]
        outs += [((b, KV_LANES, t), F32)] * 3
        out_specs += [pl.BlockSpec((1, KV_LANES, tm), kt_map)] * 3
        outs += [((b, K_LANES, t), BF16), ((n, N_KV * K_LANES), BF16)] * 2
        out_specs += [pl.BlockSpec((1, K_LANES, tm), kt_map), pl.BlockSpec((tm, N_KV * K_LANES), row)] * 2
    return pl.pallas_call(
        kern, grid=grid, in_specs=in_specs, out_specs=out_specs,
        out_shape=[jax.ShapeDtypeStruct(s, dt) for s, dt in outs],
        compiler_params=_params("parallel"), name="in_projection",
    )(x2, *weights, rc, rs1, rs2, *gm, *tail)


def _rope_tables(pos):
    half = ROT_DIM // 2
    inv = ROPE_THETA ** (-jnp.arange(half, dtype=F32) / half)
    ang = pos.astype(F32)[:, None] * inv[None, :]
    cos, sin = jnp.cos(ang), jnp.sin(ang)
    n = pos.shape[0]
    rest0 = jnp.zeros((n, HEAD_DIM - ROT_DIM), F32)
    zero = jnp.zeros((n, half), F32)
    rc = jnp.concatenate([cos, cos, rest0 + 1.0], axis=1)
    rs1 = jnp.concatenate([-sin, zero, rest0], axis=1)
    rs2 = jnp.concatenate([zero, sin, rest0], axis=1)
    return tuple(jnp.tile(a, (1, LANES // HEAD_DIM)) for a in (rc, rs1, rs2))


def _compress_partial(read_k, read_v, w1k_ref, w1v_ref):
    acc_k = acc_v = None
    stack = w1k_ref.shape[1] // K_LANES
    for i in range(CMP_STRIDE // stack):
        rows = range(i * stack, (i + 1) * stack)
        pk = _dot(jnp.concatenate([read_k(s).astype(BF16) for s in rows], axis=1), w1k_ref[i])
        pv = _dot(jnp.concatenate([read_v(s).astype(BF16) for s in rows], axis=1), w1v_ref[i])
        acc_k = pk if acc_k is None else acc_k + pk
        acc_v = pv if acc_v is None else acc_v + pv
    return acc_k, acc_v


def _compress_finish(fs_k, fs_v, b1_ref, w2_ref):
    hid = b1_ref.shape[1]
    outs = []
    for c, fs in enumerate((fs_k, fs_v)):
        hs = []
        for g in range(N_KV):
            first = fs[:, g * 2 * hid:g * 2 * hid + hid]
            second = fs[:, g * 2 * hid + hid:(g + 1) * 2 * hid]
            nxt = pltpu.roll(second, second.shape[0] - 1, 0)
            hs.append(jax.nn.silu(first + nxt + b1_ref[c:c + 1, :]))
        outs.append(_dot(jnp.concatenate(hs, axis=1).astype(BF16), w2_ref[c]))
    return jnp.concatenate(outs, axis=1)


def _compress_prompt_kernel(k_ref, v_ref, w1k_ref, w1v_ref, b1_ref, w2_ref, kck_ref, kcv_ref, kc_scr):
    nb = kcv_ref.shape[1]
    fs_k, fs_v = _compress_partial(lambda s: k_ref[0, pl.ds(s, nb, stride=CMP_STRIDE), :],
                                   lambda s: v_ref[0, pl.ds(s, nb, stride=CMP_STRIDE), :], w1k_ref, w1v_ref)
    kc_scr[...] = _compress_finish(fs_k, fs_v, b1_ref, w2_ref)
    kck_ref[0] = kc_scr[:, :K_LANES].T.astype(BF16)
    kcv_ref[0] = kc_scr[:, K_LANES:].astype(BF16)


def _compress_prompt(kvc, w):
    b, t, _ = kvc.shape
    nb = t // CMP_STRIDE
    weights = [w["w1k"], w["w1v"], w["b_c1"], w["w2"]]
    return pl.pallas_call(
        _compress_prompt_kernel, grid=(b,),
        in_specs=[pl.BlockSpec((1, t, K_LANES), lambda i: (i, 0, 0)), pl.BlockSpec((1, t, K_LANES), lambda i: (i, 0, 1))]
        + [_full(a.shape) for a in weights],
        out_specs=[pl.BlockSpec((1, K_LANES, nb), lambda i: (i, 0, 0)), pl.BlockSpec((1, nb, K_LANES), lambda i: (i, 0, 0))],
        out_shape=[jax.ShapeDtypeStruct((b, K_LANES, nb), BF16), jax.ShapeDtypeStruct((b, nb, K_LANES), BF16)],
        scratch_shapes=[pltpu.VMEM((nb, KV_LANES), F32)],
        compiler_params=_params("parallel"), name="compress_prompt",
    )(kvc, kvc, *weights)


def _stack_heads(q, g, q_per_kv):
    return jnp.concatenate([q[:, (g * q_per_kv + h) * HEAD_DIM:(g * q_per_kv + h + 1) * HEAD_DIM]
                            for h in range(q_per_kv)], axis=0)


def _select_blocks(score, n_sel):
    rows, n = score.shape
    lane = lax.broadcasted_iota(jnp.int32, (rows, n), 1).astype(F32)
    sel = jnp.zeros((rows, n), F32)
    picks = []
    x = score
    for _ in range(n_sel):
        m = jnp.max(x, axis=-1, keepdims=True)
        idx = jnp.min(jnp.where(x == m, lane, float(n)), axis=-1, keepdims=True)
        hit = lane == idx
        ok = m > 0.1 * MASKED
        sel = jnp.where(hit & ok, 1.0, sel)
        x = jnp.where(hit, -3e38, x)
        picks.append((idx, ok))
    return sel, picks


def _select_mask_t(xt, n_sel):
    n, cols = xt.shape
    tiles = [xt[t * SUBLANES:(t + 1) * SUBLANES] for t in range(n // SUBLANES)]
    row = lax.broadcasted_iota(jnp.int32, (SUBLANES, cols), 0)
    ahead = [jnp.zeros((SUBLANES, cols), F32) for _ in tiles]
    for i in range(n):
        xi = xt[i:i + 1, :]
        for t, x in enumerate(tiles):
            first, last = t * SUBLANES, (t + 1) * SUBLANES - 1
            if first > i:
                inc = jnp.where(xi >= x, 1.0, 0.0)
            elif last <= i:
                inc = jnp.where(xi > x, 1.0, 0.0)
            else:
                inc = jnp.where(row + first > i, jnp.where(xi >= x, 1.0, 0.0), jnp.where(xi > x, 1.0, 0.0))
            ahead[t] = ahead[t] + inc
    ahead = jnp.concatenate(ahead, axis=0)
    return jnp.where((ahead < n_sel) & (xt > 0.1 * MASKED), 1.0, 0.0)


def _block_scores(p_slc, blk_t, n_blocks, axis=1):
    sj = lax.broadcasted_iota(jnp.int32, p_slc.shape, axis)
    causal = (sj <= blk_t) & (sj < n_blocks)
    forced = causal & ((sj == 0) | (sj >= blk_t - 1))
    score = jnp.where(forced, p_slc + FORCE_BONUS, p_slc)
    return jnp.where(causal, score, MASKED)


def _nsa_prompt_kernel(q_ref, gate_ref, kck_ref, kcv_ref, kts_ref, vsa_ref, ktw_ref, vwa_ref, ovl_ref, exp_ref,
                       gsel_ref, gob_ref, out_ref, s_scr, mx_scr, acc_scr, *, n_cmp, n_heads):
    blk = pl.program_id(1)
    start = blk * Q_BLOCK
    q_per_kv = n_heads // N_KV
    q = q_ref[...] * (LOG2_E * HEAD_DIM ** -0.5)
    ncp = kcv_ref.shape[1]
    n_blocks = ovl_ref.shape[0]
    tpos = start + lax.broadcasted_iota(jnp.int32, (Q_BLOCK, 1), 0)
    cn = lax.broadcasted_iota(jnp.int32, (1, ncp), 1)
    cmp_mask = ((cn * CMP_STRIDE + CMP_LEN - 1 <= tpos) & (cn < n_cmp)).astype(F32)
    cmp_mask = jnp.concatenate([cmp_mask] * q_per_kv, axis=0) > 0.5
    grp_lanes = [slice(g * HEAD_DIM, (g + 1) * HEAD_DIM) for g in range(N_KV)]
    qgs = [_stack_heads(q, g, q_per_kv) for g in range(N_KV)]
    qbs = [x.astype(BF16) for x in qgs]

    def normalised(acc, g):
        return (acc / pltpu.roll(acc, HEAD_DIM, 1))[:, grp_lanes[g]]

    o_cmp, p_slc = [], []
    ovl = ovl_ref[...]
    for g in range(N_KV):
        s = jnp.where(cmp_mask, _dot(qbs[g], kck_ref[0, grp_lanes[g], :]), MASKED)
        e = jnp.where(cmp_mask, jnp.exp2(s - jnp.max(s, axis=-1, keepdims=True)), 0.0)
        p = e / jnp.maximum(jnp.sum(e, axis=-1, keepdims=True), 1e-30)
        o_cmp.append(_dot(p.astype(BF16), kcv_ref[0, :, grp_lanes[g]]))
        p_sum = p[:Q_BLOCK]
        for h in range(1, q_per_kv):
            p_sum = p_sum + p[h * Q_BLOCK:(h + 1) * Q_BLOCK]
        p_hi = p_sum.astype(BF16)
        p_lo = (p_sum - p_hi.astype(F32)).astype(BF16)
        p_slc.append(_dot_nt(ovl, p_hi) + _dot_nt(ovl, p_lo))
    blk_t = (start + lax.broadcasted_iota(jnp.int32, (1, Q_BLOCK), 1)) // SLC_BLOCK
    score = _block_scores(jnp.concatenate(p_slc, axis=1), jnp.concatenate([blk_t] * N_KV, axis=1), n_blocks, axis=0)
    sel = _select_mask_t(score, min(N_SELECT, n_blocks)) * SEL_BONUS
    sel = jnp.concatenate([sel, jnp.zeros((LANES - n_blocks, N_KV * Q_BLOCK), F32)], axis=0)
    q_aug = []
    for g in range(N_KV):
        sel_g = sel[:, g * Q_BLOCK:(g + 1) * Q_BLOCK].T[:, :n_blocks]
        q_aug.append(jnp.concatenate([qgs[g], jnp.concatenate([sel_g] * q_per_kv, axis=0)], axis=1).astype(BF16))

    ws = pl.multiple_of(jnp.maximum(start - WINDOW, 0), LANES)
    dpos = tpos - (ws + lax.broadcasted_iota(jnp.int32, (1, WIN_KEYS), 1))
    win_bias = jnp.where((dpos >= 0) & (dpos < WINDOW), 0.0, MASKED)
    win_bias = jnp.concatenate([win_bias] * q_per_kv, axis=0)
    o_win = []
    for g in range(N_KV):
        s = _dot(qbs[g], ktw_ref[0, grp_lanes[g], pl.ds(ws, WIN_KEYS)]) + win_bias
        p = jnp.exp2(s - jnp.max(s, axis=-1, keepdims=True))
        o_win.append(normalised(_dot(p.astype(BF16), vwa_ref[pl.ds(ws, WIN_KEYS), g * K_LANES:(g + 1) * K_LANES]), g))

    last = (start + Q_BLOCK - 1) // SLC_TILE
    lane_tiles = SLC_TILE // LANES
    mx_scr[...] = jnp.full(mx_scr.shape, MASKED, F32)
    acc_scr[...] = jnp.zeros(acc_scr.shape, F32)

    def score_tile(kt, masked):
        off = pl.multiple_of(kt * SLC_TILE, SLC_TILE)
        for g in range(N_KV):
            keys = jnp.concatenate([kts_ref[0, grp_lanes[g], pl.ds(off, SLC_TILE)],
                                    exp_ref[:, pl.ds(off, SLC_TILE)]], axis=0)
            s = _dot(q_aug[g], keys)
            if masked:
                row = lax.broadcasted_iota(jnp.int32, (q_per_kv * Q_BLOCK, 1), 0) % Q_BLOCK
                s = jnp.where(off + lax.broadcasted_iota(jnp.int32, (1, SLC_TILE), 1) <= start + row, s, MASKED)
            s_scr[g, :, pl.ds(off, SLC_TILE)] = s
            m = s[:, :LANES]
            for i in range(1, lane_tiles):
                m = jnp.maximum(m, s[:, i * LANES:(i + 1) * LANES])
            mx_scr[g] = jnp.maximum(mx_scr[g], m)

    def full_tile(kt, carry):
        score_tile(kt, False)
        return carry

    lax.fori_loop(0, last, full_tile, 0)
    score_tile(last, True)
    for g in range(N_KV):
        mx_scr[g] = jnp.broadcast_to(jnp.max(mx_scr[g], axis=-1, keepdims=True), mx_scr.shape[1:])

    def value_tile(kt, carry):
        off = pl.multiple_of(kt * SLC_TILE, SLC_TILE)
        for g in range(N_KV):
            p = jnp.exp2(s_scr[g, :, pl.ds(off, SLC_TILE)] - jnp.concatenate([mx_scr[g]] * lane_tiles, axis=1))
            acc_scr[g] += _dot(p.astype(BF16), vsa_ref[pl.ds(off, SLC_TILE), g * K_LANES:(g + 1) * K_LANES])
        return carry

    lax.fori_loop(0, last + 1, value_tile, 0)
    o_slc = [normalised(acc_scr[g], g) for g in range(N_KV)]

    gates = gate_ref[...]
    g_hi = gates.astype(BF16)
    g_lo = (gates - g_hi.astype(F32)).astype(BF16)
    out = None
    for j, branch in enumerate((o_cmp, o_slc, o_win)):
        spread = _dot(g_hi, gsel_ref[j]) + _dot(g_lo, gsel_ref[j])
        o = jnp.concatenate([branch[g][h * Q_BLOCK:(h + 1) * Q_BLOCK] for g in range(N_KV) for h in range(q_per_kv)],
                            axis=1)
        out = spread * o if out is None else out + spread * o
    out_ref[...] = _rms(out, gob_ref[...]).astype(BF16)


def _nsa_prompt(q, gates, kck_t, kcv, kts, vsa, ktw, vwa, w, b, t):
    n, b_width = q.shape
    n_heads = b_width // HEAD_DIM
    assert t % SLC_TILE == 0 and t >= WIN_KEYS
    nqb = t // Q_BLOCK
    ncp = kcv.shape[1]
    n_cmp = ncp - 1
    n_blocks = (n_cmp + 1) * CMP_STRIDE // SLC_BLOCK
    assert n_blocks % SUBLANES == 0 and n_blocks <= LANES
    ovl = _overlap_matrix(ncp, n_blocks, n_blocks).T.astype(BF16)
    key_blk = jnp.arange(t, dtype=jnp.int32)[None, :] // SLC_BLOCK
    expand = (key_blk == jnp.arange(n_blocks, dtype=jnp.int32)[:, None]).astype(BF16)
    col = jnp.arange(LANES, dtype=jnp.int32)[None, :, None]
    head = jnp.arange(b_width, dtype=jnp.int32)[None, None, :] // HEAD_DIM
    gate_sel = (col == head * 3 + jnp.arange(3, dtype=jnp.int32)[:, None, None]).astype(BF16)
    tok = lambda i, j: (i * nqb + j, 0)
    seq3 = lambda i, j: (i, 0, 0)
    seq2 = lambda i, j: (i, 0)
    rows = (n_heads // N_KV) * Q_BLOCK
    return pl.pallas_call(
        functools.partial(_nsa_prompt_kernel, n_cmp=n_cmp, n_heads=n_heads),
        grid=(b, nqb),
        in_specs=[pl.BlockSpec((Q_BLOCK, b_width), tok), pl.BlockSpec((Q_BLOCK, LANES), tok),
                  pl.BlockSpec((1, K_LANES, ncp), seq3), pl.BlockSpec((1, ncp, K_LANES), seq3),
                  pl.BlockSpec((1, K_LANES, t), seq3), pl.BlockSpec((t, N_KV * K_LANES), seq2),
                  pl.BlockSpec((1, K_LANES, t), seq3), pl.BlockSpec((t, N_KV * K_LANES), seq2),
                  _full(ovl.shape), _full(expand.shape), _full(gate_sel.shape), _full(w["g_out_b"].shape)],
        out_specs=pl.BlockSpec((Q_BLOCK, b_width), tok),
        out_shape=jax.ShapeDtypeStruct((n, b_width), BF16),
        scratch_shapes=[pltpu.VMEM((N_KV, rows, t), F32), pltpu.VMEM((N_KV, rows, LANES), F32),
                        pltpu.VMEM((N_KV, rows, K_LANES), F32)],
        compiler_params=_params("parallel", "arbitrary"), name="nsa_prompt",
    )(q, gates, kck_t, kcv, kts, vsa, ktw, vwa, ovl, expand, gate_sel, w["g_out_b"])


def _overlap_matrix(rows, cols, n_blocks):
    ci = jnp.arange(rows, dtype=jnp.int32)[:, None]
    sj = jnp.arange(cols, dtype=jnp.int32)[None, :]
    hit = (ci * CMP_STRIDE < (sj + 1) * SLC_BLOCK) & (ci * CMP_STRIDE + CMP_LEN > sj * SLC_BLOCK) & (sj < n_blocks)
    return hit.astype(F32)


def _compress_sample_kernel(pt_ref, *refs, pages_per_step, n_steps):
    del pt_ref
    pages = refs[:pages_per_step]
    new_ref, w1k_ref, w1v_ref, b1_ref, w2_ref, kc_ref, fsk_scr, fsv_scr = refs[pages_per_step:pages_per_step + 8]
    chunk_scr = refs[pages_per_step + 8:]
    n_chunks = len(chunk_scr) // 2
    chunk_pages = pages_per_step // n_chunks
    j = pl.program_id(1)
    blocks_per_page = PAGE_SIZE // CMP_STRIDE
    chunk_blocks = chunk_pages * blocks_per_page
    for c in range(n_chunks):
        xk_scr, xv_scr = chunk_scr[2 * c], chunk_scr[2 * c + 1]
        for i in range(chunk_pages):
            page = pages[c * chunk_pages + i]
            xk_scr[pl.ds(i * PAGE_SIZE, PAGE_SIZE), :] = page[0, :K_LANES, :].T
            xv_scr[pl.ds(i * PAGE_SIZE, PAGE_SIZE), :] = page[0, K_LANES:, :].T
    for c in range(n_chunks):
        xk_scr, xv_scr = chunk_scr[2 * c], chunk_scr[2 * c + 1]
        fs_k, fs_v = _compress_partial(lambda s: xk_scr[pl.ds(s, chunk_blocks, stride=CMP_STRIDE), :],
                                       lambda s: xv_scr[pl.ds(s, chunk_blocks, stride=CMP_STRIDE), :],
                                       w1k_ref, w1v_ref)
        off = pl.multiple_of((j * n_chunks + c) * chunk_blocks, chunk_blocks)
        fsk_scr[pl.ds(off, chunk_blocks), :] = fs_k
        fsv_scr[pl.ds(off, chunk_blocks), :] = fs_v

    @pl.when(j == n_steps - 1)
    def _():
        past_blocks = n_steps * pages_per_step * blocks_per_page
        tail = fsk_scr.shape[0] - past_blocks
        new = new_ref[0]
        is_first = lax.broadcasted_iota(jnp.int32, (tail, 1), 0) == 0
        nk = _dot(new[:, :K_LANES].astype(BF16), w1k_ref[0, :K_LANES, :])
        nv = _dot(new[:, K_LANES:].astype(BF16), w1v_ref[0, :K_LANES, :])
        fsk_scr[pl.ds(past_blocks, tail), :] = jnp.where(is_first, nk, 0.0)
        fsv_scr[pl.ds(past_blocks, tail), :] = jnp.where(is_first, nv, 0.0)
        kc_ref[0] = _compress_finish(fsk_scr[...], fsv_scr[...], b1_ref, w2_ref)


def _compress_sample(cache_cmp_t, page_table, new_rows, w):
    db, n_pages = page_table.shape
    pages_per_step, n_chunks = 64, 2
    assert n_pages % pages_per_step == 0
    n_steps = n_pages // pages_per_step
    blocks_per_page = PAGE_SIZE // CMP_STRIDE
    past_blocks = n_pages * blocks_per_page
    nbp = past_blocks + SUBLANES
    weights = [w["w1k"], w["w1v"], w["b_c1"], w["w2"]]
    hid2 = w["w1k"].shape[2]
    page_spec = lambda i: pl.BlockSpec((1, KV_LANES, PAGE_SIZE),
                                       lambda b, j, pt, i=i: (pt[b, j * pages_per_step + i], 0, 0))
    chunk_rows = pages_per_step // n_chunks * PAGE_SIZE
    grid_spec = pltpu.PrefetchScalarGridSpec(
        num_scalar_prefetch=1, grid=(db, n_steps),
        in_specs=[page_spec(i) for i in range(pages_per_step)]
        + [pl.BlockSpec((1, 1, KV_LANES), lambda b, j, pt: (b, 0, 0))]
        + [pl.BlockSpec(a.shape, lambda b, j, pt, nd=a.ndim: (0,) * nd) for a in weights],
        out_specs=pl.BlockSpec((1, nbp, KV_LANES), lambda b, j, pt: (b, 0, 0)),
        scratch_shapes=[pltpu.VMEM((nbp, hid2), F32), pltpu.VMEM((nbp, hid2), F32)]
        + [pltpu.VMEM((chunk_rows, K_LANES), F32)] * (2 * n_chunks))
    return pl.pallas_call(
        functools.partial(_compress_sample_kernel, pages_per_step=pages_per_step, n_steps=n_steps),
        grid_spec=grid_spec, out_shape=jax.ShapeDtypeStruct((db, nbp, KV_LANES), F32),
        compiler_params=_params("parallel", "arbitrary"), name="compress_sample",
    )(page_table, *([cache_cmp_t] * pages_per_step), new_rows[:, None, :], *weights)


def _select_sample_kernel(q_ref, kc_ref, ocmp_ref, psum_ref, *, pos, n_cmp, n_heads):
    q_per_kv = n_heads // N_KV
    q = q_ref[0] * (LOG2_E * HEAD_DIM ** -0.5)
    kc = kc_ref[0]
    ncp = kc.shape[0]
    cn = lax.broadcasted_iota(jnp.int32, (1, ncp), 1)
    mask = (cn * CMP_STRIDE + CMP_LEN - 1 <= pos) & (cn < n_cmp)
    o_rows, p_rows = [], []
    for g in range(N_KV):
        qg = _stack_heads(q, g, q_per_kv).astype(BF16)
        s = jnp.where(mask, _dot_nt(qg, kc[:, g * HEAD_DIM:(g + 1) * HEAD_DIM].astype(BF16)), MASKED)
        e = jnp.where(mask, jnp.exp2(s - jnp.max(s, axis=-1, keepdims=True)), 0.0)
        p = e / jnp.maximum(jnp.sum(e, axis=-1, keepdims=True), 1e-30)
        o_rows.append(_dot(p.astype(BF16), kc[:, K_LANES + g * HEAD_DIM:K_LANES + (g + 1) * HEAD_DIM].astype(BF16)))
        p_rows.append(jnp.sum(p, axis=0, keepdims=True))
    ocmp_ref[0] = jnp.concatenate(o_rows, axis=0)
    psum_ref[0] = jnp.concatenate(p_rows, axis=0)


def _pick_sample_kernel(psum_ref, ovl_ref, idx_ref, *, pos, n_blocks):
    p = psum_ref[...]
    p_hi = p.astype(BF16)
    p_lo = (p - p_hi.astype(F32)).astype(BF16)
    ovl = ovl_ref[...]
    score = _block_scores(_dot(p_hi, ovl) + _dot(p_lo, ovl), pos // SLC_BLOCK, n_blocks)
    _, picks = _select_blocks(score, min(N_SELECT, n_blocks))
    lane = lax.broadcasted_iota(jnp.int32, idx_ref.shape, 1)
    out = jnp.full(idx_ref.shape, -1.0, F32)
    for i, (idx, ok) in enumerate(picks):
        out = jnp.where((lane == i) & ok, idx, out)
    idx_ref[...] = out.astype(jnp.int32)


def _select_sample(q, kc, pos, n_cmp, n_blocks):
    db, b_width = q.shape
    n_heads = b_width // HEAD_DIM
    ncp = kc.shape[1]
    o_cmp, p_sum = pl.pallas_call(
        functools.partial(_select_sample_kernel, pos=pos, n_cmp=n_cmp, n_heads=n_heads),
        grid=(db,),
        in_specs=[pl.BlockSpec((1, 1, b_width), lambda i: (i, 0, 0)),
                  pl.BlockSpec((1, ncp, KV_LANES), lambda i: (i, 0, 0))],
        out_specs=[pl.BlockSpec((1, n_heads, HEAD_DIM), lambda i: (i, 0, 0)),
                   pl.BlockSpec((1, N_KV, ncp), lambda i: (i, 0, 0))],
        out_shape=[jax.ShapeDtypeStruct((db, n_heads, HEAD_DIM), F32),
                   jax.ShapeDtypeStruct((db, N_KV, ncp), F32)],
        compiler_params=_params("parallel"), name="select_sample",
    )(q[:, None, :], kc)
    nsp = -(-n_blocks // LANES) * LANES
    ovl = _overlap_matrix(ncp, nsp, n_blocks).astype(BF16)
    idx = pl.pallas_call(
        functools.partial(_pick_sample_kernel, pos=pos, n_blocks=n_blocks),
        grid=(1,),
        in_specs=[_full((db * N_KV, ncp)), _full(ovl.shape)],
        out_specs=_full((db * N_KV, LANES)),
        out_shape=jax.ShapeDtypeStruct((db * N_KV, LANES), jnp.int32),
        compiler_params=_params("arbitrary"), name="pick_sample",
    )(p_sum.reshape(db * N_KV, ncp), ovl)
    return o_cmp, idx.reshape(db, N_KV, LANES)


def _attend_sample_kernel(page_ref, *refs, n_sel, past_blocks, n_heads):
    del page_ref
    n_slots = N_KV * n_sel
    pages = refs[:n_slots]
    (q_ref, gate_ref, ocmp_ref, idx_ref, newslc_ref, win_ref, newwin_ref, exp_ref, gob_ref,
     out_ref, winout_ref) = refs[n_slots:]
    q_per_kv = n_heads // N_KV
    q = q_ref[0] * (HEAD_DIM ** -0.5)
    lane = lax.broadcasted_iota(jnp.int32, (1, K_LANES), 1)
    wb = win_ref.shape[2]
    is_last = lax.broadcasted_iota(jnp.int32, (1, wb), 1) == wb - 1
    win = jnp.where(is_last, newwin_ref[0], pltpu.roll(win_ref[0], wb - 1, 1))
    winout_ref[0] = win
    win_k = win[:K_LANES].astype(BF16)
    win_v = win[K_LANES:].astype(BF16)
    new_slc = newslc_ref[0]
    idx = idx_ref[0].astype(F32)
    n_keys = n_sel * PAGE_SIZE
    key_half = (lax.broadcasted_iota(jnp.int32, (1, n_keys), 1) % PAGE_SIZE) // SLC_BLOCK
    o_slc, o_win = [], []
    for g in range(N_KV):
        qpad = jnp.concatenate(
            [jnp.where(lane // HEAD_DIM == g,
                       jnp.concatenate([q[:, (g * q_per_kv + h) * HEAD_DIM:(g * q_per_kv + h + 1) * HEAD_DIM]] * N_KV,
                                       axis=1), 0.0)
             for h in range(q_per_kv)], axis=0)
        qpb = qpad.astype(BF16)
        kt = jnp.concatenate([pages[g * n_sel + i][0, :K_LANES, :] for i in range(n_sel)], axis=1).astype(BF16)
        vt = jnp.concatenate([pages[g * n_sel + i][0, K_LANES:, :] for i in range(n_sel)], axis=1).astype(BF16)
        s = _dot(qpb, kt)
        idg = idx[g:g + 1, :]
        idk = jnp.dot(idg, exp_ref[...], precision=lax.Precision.HIGHEST, preferred_element_type=F32)
        parity = idk - 2.0 * jnp.floor(idk * 0.5)
        key_ok = (idk >= 0.0) & (idk < past_blocks) & (parity == key_half.astype(F32))
        s = jnp.where(key_ok, s, MASKED)
        has_new = jnp.max(jnp.where(idg == past_blocks, 1.0, 0.0), axis=-1, keepdims=True) > 0.5
        s_new = jnp.sum(qpad * new_slc[:, :K_LANES], axis=-1, keepdims=True)
        s_new = jnp.where(has_new, s_new, MASKED)
        m = jnp.maximum(jnp.max(s, axis=-1, keepdims=True), s_new)
        e = jnp.where(key_ok, jnp.exp(s - m), 0.0)
        e_new = jnp.where(has_new, jnp.exp(s_new - m), 0.0)
        den = jnp.maximum(jnp.sum(e, axis=-1, keepdims=True) + e_new, 1e-30)
        o = (_dot_nt(e.astype(BF16), vt) + e_new * new_slc[:, K_LANES:]) / den
        o_slc.append(o[:, g * HEAD_DIM:(g + 1) * HEAD_DIM])
        s = _dot(qpb, win_k)
        e = jnp.exp(s - jnp.max(s, axis=-1, keepdims=True))
        o = _dot_nt(e.astype(BF16), win_v) / jnp.sum(e, axis=-1, keepdims=True)
        o_win.append(o[:, g * HEAD_DIM:(g + 1) * HEAD_DIM])
    gates = gate_ref[0]
    o = (gates[:, 0:1] * ocmp_ref[0] + gates[:, 1:2] * jnp.concatenate(o_slc, axis=0)
         + gates[:, 2:3] * jnp.concatenate(o_win, axis=0))
    ms = jnp.sum(jnp.sum(o * o, axis=-1, keepdims=True), axis=0, keepdims=True) / (n_heads * HEAD_DIM)
    out_ref[0] = o * lax.rsqrt(ms + NORM_EPS) * gob_ref[...]


def _attend_sample(q, gates, o_cmp, idx, cache_slc_t, page_table, new_slc, win_t, new_win, w, past_blocks):
    db, b_width = q.shape
    n_heads = b_width // HEAD_DIM
    n_sel = min(N_SELECT, past_blocks + 1)
    sub_per_page = PAGE_SIZE // SLC_BLOCK
    jp = jnp.clip(idx[:, :, :n_sel], 0, past_blocks - 1)
    page = jnp.take_along_axis(page_table, (jp // sub_per_page).reshape(db, -1), axis=1).astype(jnp.int32)
    n_keys = n_sel * PAGE_SIZE
    expand = (jnp.arange(n_keys, dtype=jnp.int32)[None, :] // PAGE_SIZE
              == jnp.arange(LANES, dtype=jnp.int32)[:, None]).astype(F32)
    gob = w["g_out_b"].reshape(n_heads, HEAD_DIM)
    wb = win_t.shape[2]
    per_seq = lambda shape: pl.BlockSpec((1,) + shape, lambda b, pg: (b, 0, 0))
    page_spec = lambda i: pl.BlockSpec((1, KV_LANES, PAGE_SIZE), lambda b, pg, i=i: (pg[b, i], 0, 0))
    grid_spec = pltpu.PrefetchScalarGridSpec(
        num_scalar_prefetch=1, grid=(db,),
        in_specs=[page_spec(i) for i in range(N_KV * n_sel)]
        + [per_seq((1, b_width)), per_seq((n_heads, 3)), per_seq((n_heads, HEAD_DIM)), per_seq((N_KV, LANES)),
           per_seq((1, KV_LANES)), per_seq((KV_LANES, wb)), per_seq((KV_LANES, 1)),
           pl.BlockSpec(expand.shape, lambda b, pg: (0, 0)), pl.BlockSpec(gob.shape, lambda b, pg: (0, 0))],
        out_specs=[per_seq((n_heads, HEAD_DIM)), per_seq((KV_LANES, wb))])
    return pl.pallas_call(
        functools.partial(_attend_sample_kernel, n_sel=n_sel, past_blocks=past_blocks, n_heads=n_heads),
        grid_spec=grid_spec,
        out_shape=[jax.ShapeDtypeStruct((db, n_heads, HEAD_DIM), F32),
                   jax.ShapeDtypeStruct((db, KV_LANES, wb), F32)],
        compiler_params=_params("parallel"), name="attend_sample",
    )(page, *([cache_slc_t] * (N_KV * n_sel)), q[:, None, :], gates[:, :n_heads * 3].reshape(db, n_heads, 3), o_cmp,
      idx, new_slc[:, None, :], win_t, new_win[:, :, None], expand, gob)


def _ffn_kernel(x_ref, ma_ref, mb_ref, woa_ref, wob_ref, gffn_ref, wgate_ref, wup_ref, wdown_ref, gfin_ref, y_ref,
                acc_scr, xn_scr):
    c = pl.program_id(1)

    @pl.when(c == 0)
    def _():
        x = x_ref[...] + _dot(ma_ref[...], woa_ref[...]) + _dot(mb_ref[...], wob_ref[...])
        acc_scr[...] = x
        xn_scr[...] = _rms(x, gffn_ref[...]).astype(BF16)

    xn = xn_scr[...]
    hid = jax.nn.silu(_dot(xn, wgate_ref[...])) * _dot(xn, wup_ref[...])
    acc_scr[...] += _dot(hid.astype(BF16), wdown_ref[...])

    @pl.when(c == pl.num_programs(1) - 1)
    def _():
        y_ref[...] = _rms(acc_scr[...], gfin_ref[...])


def _output_ffn(x2, mix_a, mix_b, w, tm):
    n, d = x2.shape
    d_ff = w["w_down"].shape[0]
    ff_chunks = 2
    step = d_ff // ff_chunks
    assert n % tm == 0 and d_ff % ff_chunks == 0 and step % LANES == 0
    row = lambda i, c: (i, 0)
    fixed = lambda a: pl.BlockSpec(a.shape, lambda i, c: (0, 0))
    return pl.pallas_call(
        _ffn_kernel,
        grid=(n // tm, ff_chunks),
        in_specs=[pl.BlockSpec((tm, d), row), pl.BlockSpec((tm, mix_a.shape[1]), row),
                  pl.BlockSpec((tm, mix_b.shape[1]), row), fixed(w["wo_a"]), fixed(w["wo_b"]), fixed(w["g_ffn"]),
                  pl.BlockSpec((d, step), lambda i, c: (0, c)), pl.BlockSpec((d, step), lambda i, c: (0, c)),
                  pl.BlockSpec((step, d), lambda i, c: (c, 0)), fixed(w["g_final"])],
        out_specs=pl.BlockSpec((tm, d), row),
        out_shape=jax.ShapeDtypeStruct((n, d), F32),
        scratch_shapes=[pltpu.VMEM((tm, d), F32), pltpu.VMEM((tm, d), BF16)],
        compiler_params=_params("parallel", "arbitrary"), name="output_ffn",
    )(x2, mix_a, mix_b, w["wo_a"], w["wo_b"], w["g_ffn"], w["w_gate"], w["w_up"], w["w_down"], w["g_final"])


def _prepare_weights(l, g_attn, w_in, g_sgu, w_s, b_s, w_c1, b_c1, w_c2, g_out_a, g_out_b, w_out,
                     g_ffn, w_gate_up, w_down, g_final):
    a_width = g_sgu.shape[1]
    b_width = g_out_b.shape[1]
    n_heads = b_width // HEAD_DIM
    d_ff = w_down.shape[1]
    hid = b_c1.shape[2]
    o1 = 2 * a_width
    o2 = o1 + b_width
    o3 = o2 + 3 * KV_LANES
    wi = w_in[l]
    wg = jnp.pad(wi[:, o3:], ((0, 0), (0, LANES - 3 * n_heads)))
    row = lambda a: a.reshape(1, -1)
    w1 = w_c1[l].reshape(2, 2, CMP_STRIDE, HEAD_DIM, hid)
    w1 = jnp.transpose(w1, (0, 2, 3, 1, 4)).reshape(2, CMP_STRIDE, HEAD_DIM, 2 * hid)
    eye = jnp.eye(N_KV, dtype=F32)
    w1 = jnp.einsum("gh,csdn->csgdhn", eye, w1).reshape(2, CMP_STRIDE // CMP_STACK, CMP_STACK * K_LANES, N_KV * 2 * hid)
    w2 = jnp.einsum("gh,cne->cgnhe", eye, w_c2[l]).reshape(2, N_KV * hid, K_LANES)
    return {
        "g_attn": row(g_attn[l]), "wuv": wi[:, :o1].astype(BF16), "wq": wi[:, o1:o2].astype(BF16),
        "wkv": wi[:, o2:o3].astype(BF16), "wg": wg.astype(BF16),
        "g_sgu": row(g_sgu[l]), "g_out_a": row(g_out_a[l]), "g_out_b": row(g_out_b[l]),
        "w_s": w_s[l], "bs_full": jnp.repeat(b_s[l].T, LANES, axis=1),
        "ws0": row(jnp.repeat(w_s[l][:, 0, 0], LANES)), "bs0": row(jnp.repeat(b_s[l][:, 0], LANES)),
        "w1k": w1[0].astype(BF16), "w1v": w1[1].astype(BF16), "b_c1": b_c1[l], "w2": w2.astype(BF16),
        "wo_a": w_out[l][:a_width].astype(BF16), "wo_b": w_out[l][a_width:].astype(BF16),
        "g_ffn": row(g_ffn[l]), "w_gate": w_gate_up[l][:, :d_ff].astype(BF16),
        "w_up": w_gate_up[l][:, d_ff:].astype(BF16), "w_down": w_down[l].astype(BF16), "g_final": row(g_final),
    }


def kernel(x_prompt, x_sample, cache_cmp_kv, cache_slc_kv, state_win_kv, page_table, g_attn, w_in, g_sgu, w_s, b_s,
           w_c1, b_c1, w_c2, g_out_a, g_out_b, w_out, g_ffn, w_gate_up, w_down, g_final):
    depth = w_in.shape[0]
    b, t, d = x_prompt.shape
    db, t_s, _ = x_sample.shape
    assert depth == 1 and t_s == 1
    n_pages = page_table.shape[1]
    past = n_pages * PAGE_SIZE
    wb = state_win_kv.shape[2]
    assert wb == WINDOW and past % SLC_BLOCK == 0
    l = 0
    w = _prepare_weights(l, g_attn, w_in, g_sgu, w_s, b_s, w_c1, b_c1, w_c2, g_out_a, g_out_b, w_out,
                         g_ffn, w_gate_up, w_down, g_final)
    xp = x_prompt.reshape(b * t, d)
    mix_a, q, kvc, gates, kvc_t, kvs_t, kvw_t, kts, vs, ktw, vw = _in_projection(
        xp, jnp.arange(t, dtype=jnp.int32), w, prompt_shape=(b, t))
    kck_t, kcv = _compress_prompt(kvc.reshape(b, t, KV_LANES), w)
    mix_b = _nsa_prompt(q, gates, kck_t, kcv, kts, vs, ktw, vw, w, b, t)
    y_prompt = _output_ffn(xp, mix_a, mix_b, w, tm=512).reshape(b, t, d)

    xs = x_sample.reshape(db, d)
    pos_s = past + jnp.zeros((db,), jnp.int32)
    mix_a_s, q_s, kvc_s, kvs_s, kvw_s, gates_s, v_rows = _in_projection(xs, pos_s, w)
    kc_s = _compress_sample(_feature_major(cache_cmp_kv[l]), page_table, kvc_s, w)
    lp = -(-(past + t_s) // SLC_BLOCK) * SLC_BLOCK
    n_cmp_s = lp // CMP_STRIDE - 1
    n_blocks_s = (n_cmp_s + 1) * CMP_STRIDE // SLC_BLOCK
    o_cmp_s, idx_s = _select_sample(q_s, kc_s, past, n_cmp_s, n_blocks_s)
    mix_b_s, win_new_t = _attend_sample(q_s, gates_s, o_cmp_s, idx_s, _feature_major(cache_slc_kv[l]), page_table,
                                        kvs_s, _feature_major(state_win_kv[l]), kvw_s, w, past // SLC_BLOCK)
    y_sample = _output_ffn(xs, mix_a_s, mix_b_s.reshape(db, -1).astype(BF16), w, tm=db).reshape(db, t_s, d)

    kv_shape = (2, N_KV, HEAD_DIM)
    return (y_prompt, y_sample,
            _row_major(kvc_t)[None], _row_major(kvs_t)[None], _row_major(kvw_t[:, :, t - min(WINDOW, t):])[None],
            kvc_s.reshape(1, db, t_s, *kv_shape), kvs_s.reshape(1, db, t_s, *kv_shape),
            _row_major(win_new_t)[None], v_rows.reshape(1, db, t_s, -1))


def _feature_major(kv):
    n, rows = kv.shape[:2]
    return jnp.transpose(kv, (0, 2, 3, 4, 1)).reshape(n, KV_LANES, rows)


def _row_major(kv_t):
    n, _, rows = kv_t.shape
    return jnp.transpose(kv_t.reshape(n, 2, N_KV, HEAD_DIM, rows), (0, 4, 1, 2, 3))
```

```python
import functools

import jax
import jax.numpy as jnp
from jax import lax
from jax.experimental import pallas as pl
from jax.experimental.pallas import tpu as pltpu

F32 = jnp.float32
BF16 = jnp.bfloat16

A_GROUPS = 4
CHUNK = 128
HEAD_DIM = 64
N_KV = 2
ROT_DIM = HEAD_DIM // 4
ROPE_THETA = 500000.0
CMP_LEN = 32
CMP_STRIDE = 16
SLC_BLOCK = 64
N_SELECT = 16
WINDOW = 512
Q_BLOCK = 128
FORCE_BONUS = 1000.0
PAGE_SIZE = 128
NORM_EPS = 1e-6
MASKED = -1e30
LOG2_E = 1.4426950408889634
SEL_BONUS = 16384.0

LANES = 128
SUBLANES = 8
VMEM_LIMIT_BYTES = 56 * 1024 * 1024

KV_LANES = 2 * N_KV * HEAD_DIM
K_LANES = N_KV * HEAD_DIM
CMP_STACK = 2
SLC_TILE = 512
WIN_KEYS = WINDOW + Q_BLOCK


def _rms(x, g):
    return x * lax.rsqrt(jnp.mean(x * x, axis=-1, keepdims=True) + NORM_EPS) * g


def _dot(a, b):
    return jnp.dot(a, b, preferred_element_type=F32)


def _dot_nt(a, b, precision=None):
    return lax.dot_general(a, b, (((1,), (1,)), ((), ())), precision=precision,
                           preferred_element_type=F32)


def _rope(z, rc, rs1, rs2):
    return z * rc + pltpu.roll(z, LANES - ROT_DIM // 2, 1) * rs1 + pltpu.roll(z, ROT_DIM // 2, 1) * rs2


def _project(x_ref, gattn_ref, wuv_ref, wq_ref, wkv_ref, wg_ref, rc_ref, rs1_ref, rs2_ref):
    xn = _rms(x_ref[...], gattn_ref[...]).astype(BF16)
    rc, rs1, rs2 = rc_ref[...], rs1_ref[...], rs2_ref[...]
    zuv = _dot(xn, wuv_ref[...])
    a_width = zuv.shape[1] // 2
    zq = _dot(xn, wq_ref[...])
    q = jnp.concatenate([_rope(zq[:, i * LANES:(i + 1) * LANES], rc, rs1, rs2)
                         for i in range(zq.shape[1] // LANES)], axis=1)
    zkv = _dot(xn, wkv_ref[...])
    branches = []
    for br in range(3):
        k = _rope(zkv[:, br * KV_LANES:br * KV_LANES + K_LANES], rc, rs1, rs2)
        v = zkv[:, br * KV_LANES + K_LANES:(br + 1) * KV_LANES]
        branches.append((k, v))
    gates = jax.nn.sigmoid(_dot(xn, wg_ref[...]))
    return zuv[:, :a_width], zuv[:, a_width:], q, branches, gates


def _gmlp_norm_v(v, gsgu):
    v = jax.nn.gelu(v)
    return jnp.concatenate([_rms(v[:, g * LANES:(g + 1) * LANES], gsgu[:, g * LANES:(g + 1) * LANES])
                            for g in range(A_GROUPS)], axis=1)


def _inproj_prompt_kernel(x_ref, gattn_ref, wuv_ref, wq_ref, wkv_ref, wg_ref, rc_ref, rs1_ref, rs2_ref,
                          ws_ref, bs_ref, gsgu_ref, goa_ref,
                          mixa_ref, q_ref, kvc_ref, gate_ref, kvct_ref, kvst_ref, kvwt_ref,
                          kts_ref, vs_ref, ktw_ref, vw_ref):
    u, v, q, branches, gates = _project(x_ref, gattn_ref, wuv_ref, wq_ref, wkv_ref, wg_ref,
                                        rc_ref, rs1_ref, rs2_ref)
    tm = u.shape[0]
    u = jax.nn.gelu(u)
    vg = _gmlp_norm_v(v, gsgu_ref[...]).astype(BF16)
    row = lax.broadcasted_iota(jnp.int32, (CHUNK, CHUNK), 0)
    col = lax.broadcasted_iota(jnp.int32, (CHUNK, CHUNK), 1)
    bias = bs_ref[...]
    parts = []
    for g in range(A_GROUPS):
        w = jnp.where(row >= col, ws_ref[g], 0.0).astype(BF16)
        s = jnp.concatenate(
            [_dot(w, vg[c * CHUNK:(c + 1) * CHUNK, g * LANES:(g + 1) * LANES]) for c in range(tm // CHUNK)],
            axis=0)
        s = s + jnp.concatenate([bias[:, g * LANES:(g + 1) * LANES]] * (tm // CHUNK), axis=0)
        parts.append(u[:, g * LANES:(g + 1) * LANES] * s)
    mixa_ref[...] = _rms(jnp.concatenate(parts, axis=1), goa_ref[...]).astype(BF16)
    q_ref[...] = q
    gate_ref[...] = gates
    kvc_ref[...] = jnp.concatenate(branches[0], axis=1)
    kts = []
    for ref, (k, v_) in zip((kvct_ref, kvst_ref, kvwt_ref), branches):
        kt = k.T
        ref[0, :K_LANES, :] = kt
        ref[0, K_LANES:, :] = v_.T
        kts.append(kt)
    own = [lax.broadcasted_iota(jnp.int32, (1, K_LANES), 1) // HEAD_DIM == g for g in range(N_KV)]
    for kt_ref, va_ref, br in ((kts_ref, vs_ref, 1), (ktw_ref, vw_ref, 2)):
        kt_ref[0] = kts[br].astype(BF16)
        va_ref[...] = jnp.concatenate([jnp.where(m, branches[br][1], 1.0) for m in own], axis=1).astype(BF16)


def _inproj_sample_kernel(x_ref, gattn_ref, wuv_ref, wq_ref, wkv_ref, wg_ref, rc_ref, rs1_ref, rs2_ref,
                          ws0_ref, bs0_ref, gsgu_ref, goa_ref,
                          mixa_ref, q_ref, kvc_ref, kvs_ref, kvw_ref, gate_ref, vrow_ref):
    u, v, q, branches, gates = _project(x_ref, gattn_ref, wuv_ref, wq_ref, wkv_ref, wg_ref,
                                        rc_ref, rs1_ref, rs2_ref)
    vg = _gmlp_norm_v(v, gsgu_ref[...])
    o_a = jax.nn.gelu(u) * (vg * ws0_ref[...] + bs0_ref[...])
    mixa_ref[...] = _rms(o_a, goa_ref[...]).astype(BF16)
    vrow_ref[...] = vg
    q_ref[...] = q
    for ref, (k, v_) in zip((kvc_ref, kvs_ref, kvw_ref), branches):
        ref[...] = jnp.concatenate([k, v_], axis=1)
    gate_ref[...] = gates


def _full(shape):
    return pl.BlockSpec(shape, lambda *_: (0,) * len(shape))


def _params(*sem):
    return pltpu.CompilerParams(dimension_semantics=sem, vmem_limit_bytes=VMEM_LIMIT_BYTES)


def _in_projection(x2, pos, w, *, prompt_shape=None):
    n, d = x2.shape
    rc, rs1, rs2 = _rope_tables(pos)
    a_width = w["wuv"].shape[1] // 2
    b_width = w["wq"].shape[1]
    weights = [w["g_attn"], w["wuv"], w["wq"], w["wkv"], w["wg"]]
    wspecs = [_full(a.shape) for a in weights]
    tail = [w["g_sgu"], w["g_out_a"]]
    if prompt_shape is None:
        tm, grid = n, (1,)
        row = lambda i: (i, 0)
        rope_map = row
        gm = [w["ws0"], w["bs0"]]
    else:
        b, t = prompt_shape
        tm = 512
        assert t % tm == 0 and tm % CHUNK == 0
        tpb = t // tm
        grid = (b * tpb,)
        row = lambda i: (i, 0)
        rope_map = lambda i: (i % tpb, 0)
        gm = [w["w_s"], w["bs_full"]]
    rspec = pl.BlockSpec((tm, LANES), rope_map)
    in_specs = ([pl.BlockSpec((tm, d), row)] + wspecs + [rspec] * 3
                + [_full(a.shape) for a in gm] + [_full(a.shape) for a in tail])
    if prompt_shape is None:
        kern = _inproj_sample_kernel
        outs = [((n, a_width), BF16), ((n, b_width), F32), ((n, KV_LANES), F32), ((n, KV_LANES), F32),
                ((n, KV_LANES), F32), ((n, LANES), F32), ((n, a_width), F32)]
        out_specs = [pl.BlockSpec((tm, s[1]), row) for s, _ in outs]
    else:
        kern = _inproj_prompt_kernel
        kt_map = lambda i: (i // tpb, 0, i % tpb)
        outs = [((n, a_width), BF16), ((n, b_width), F32), ((n, KV_LANES), F32), ((n, LANES), F32)]
        out_specs = [pl.BlockSpec((tm, s[1]), row) for s, _ in outs]
        outs += [((b, KV_LANES, t), F32)] * 3
        out_specs += [pl.BlockSpec((1, KV_LANES, tm), kt_map)] * 3
        outs += [((b, K_LANES, t), BF16), ((n, N_KV * K_LANES), BF16)] * 2
        out_specs += [pl.BlockSpec((1, K_LANES, tm), kt_map), pl.BlockSpec((tm, N_KV * K_LANES), row)] * 2
    return pl.pallas_call(
        kern, grid=grid, in_specs=in_specs, out_specs=out_specs,
        out_shape=[jax.ShapeDtypeStruct(s, dt) for s, dt in outs],
        compiler_params=_params("parallel"), name="in_projection",
    )(x2, *weights, rc, rs1, rs2, *gm, *tail)


def _rope_tables(pos):
    half = ROT_DIM // 2
    inv = ROPE_THETA ** (-jnp.arange(half, dtype=F32) / half)
    ang = pos.astype(F32)[:, None] * inv[None, :]
    cos, sin = jnp.cos(ang), jnp.sin(ang)
    n = pos.shape[0]
    rest0 = jnp.zeros((n, HEAD_DIM - ROT_DIM), F32)
    zero = jnp.zeros((n, half), F32)
    rc = jnp.concatenate([cos, cos, rest0 + 1.0], axis=1)
    rs1 = jnp.concatenate([-sin, zero, rest0], axis=1)
    rs2 = jnp.concatenate([zero, sin, rest0], axis=1)
    return tuple(jnp.tile(a, (1, LANES // HEAD_DIM)) for a in (rc, rs1, rs2))


def _compress_partial(read_k, read_v, w1k_ref, w1v_ref):
    acc_k = acc_v = None
    stack = w1k_ref.shape[1] // K_LANES
    for i in range(CMP_STRIDE // stack):
        rows = range(i * stack, (i + 1) * stack)
        pk = _dot(jnp.concatenate([read_k(s).astype(BF16) for s in rows], axis=1), w1k_ref[i])
        pv = _dot(jnp.concatenate([read_v(s).astype(BF16) for s in rows], axis=1), w1v_ref[i])
        acc_k = pk if acc_k is None else acc_k + pk
        acc_v = pv if acc_v is None else acc_v + pv
    return acc_k, acc_v


def _compress_finish(fs_k, fs_v, b1_ref, w2_ref):
    hid = b1_ref.shape[1]
    outs = []
    for c, fs in enumerate((fs_k, fs_v)):
        hs = []
        for g in range(N_KV):
            first = fs[:, g * 2 * hid:g * 2 * hid + hid]
            second = fs[:, g * 2 * hid + hid:(g + 1) * 2 * hid]
            nxt = pltpu.roll(second, second.shape[0] - 1, 0)
            hs.append(jax.nn.silu(first + nxt + b1_ref[c:c + 1, :]))
        outs.append(_dot(jnp.concatenate(hs, axis=1).astype(BF16), w2_ref[c]))
    return jnp.concatenate(outs, axis=1)


def _compress_prompt_kernel(k_ref, v_ref, w1k_ref, w1v_ref, b1_ref, w2_ref, kck_ref, kcv_ref, kc_scr):
    nb = kcv_ref.shape[1]
    fs_k, fs_v = _compress_partial(lambda s: k_ref[0, pl.ds(s, nb, stride=CMP_STRIDE), :],
                                   lambda s: v_ref[0, pl.ds(s, nb, stride=CMP_STRIDE), :], w1k_ref, w1v_ref)
    kc_scr[...] = _compress_finish(fs_k, fs_v, b1_ref, w2_ref)
    kck_ref[0] = kc_scr[:, :K_LANES].T.astype(BF16)
    kcv_ref[0] = kc_scr[:, K_LANES:].astype(BF16)


def _compress_prompt(kvc, w):
    b, t, _ = kvc.shape
    nb = t // CMP_STRIDE
    weights = [w["w1k"], w["w1v"], w["b_c1"], w["w2"]]
    return pl.pallas_call(
        _compress_prompt_kernel, grid=(b,),
        in_specs=[pl.BlockSpec((1, t, K_LANES), lambda i: (i, 0, 0)), pl.BlockSpec((1, t, K_LANES), lambda i: (i, 0, 1))]
        + [_full(a.shape) for a in weights],
        out_specs=[pl.BlockSpec((1, K_LANES, nb), lambda i: (i, 0, 0)), pl.BlockSpec((1, nb, K_LANES), lambda i: (i, 0, 0))],
        out_shape=[jax.ShapeDtypeStruct((b, K_LANES, nb), BF16), jax.ShapeDtypeStruct((b, nb, K_LANES), BF16)],
        scratch_shapes=[pltpu.VMEM((nb, KV_LANES), F32)],
        compiler_params=_params("parallel"), name="compress_prompt",
    )(kvc, kvc, *weights)


def _stack_heads(q, g, q_per_kv):
    return jnp.concatenate([q[:, (g * q_per_kv + h) * HEAD_DIM:(g * q_per_kv + h + 1) * HEAD_DIM]
                            for h in range(q_per_kv)], axis=0)


def _select_blocks(score, n_sel):
    rows, n = score.shape
    lane = lax.broadcasted_iota(jnp.int32, (rows, n), 1).astype(F32)
    sel = jnp.zeros((rows, n), F32)
    picks = []
    x = score
    for _ in range(n_sel):
        m = jnp.max(x, axis=-1, keepdims=True)
        idx = jnp.min(jnp.where(x == m, lane, float(n)), axis=-1, keepdims=True)
        hit = lane == idx
        ok = m > 0.1 * MASKED
        sel = jnp.where(hit & ok, 1.0, sel)
        x = jnp.where(hit, -3e38, x)
        picks.append((idx, ok))
    return sel, picks


def _select_mask_t(xt, n_sel):
    n, cols = xt.shape
    tiles = [xt[t * SUBLANES:(t + 1) * SUBLANES] for t in range(n // SUBLANES)]
    row = lax.broadcasted_iota(jnp.int32, (SUBLANES, cols), 0)
    ahead = [jnp.zeros((SUBLANES, cols), F32) for _ in tiles]
    for i in range(n):
        xi = xt[i:i + 1, :]
        for t, x in enumerate(tiles):
            first, last = t * SUBLANES, (t + 1) * SUBLANES - 1
            if first > i:
                inc = jnp.where(xi >= x, 1.0, 0.0)
            elif last <= i:
                inc = jnp.where(xi > x, 1.0, 0.0)
            else:
                inc = jnp.where(row + first > i, jnp.where(xi >= x, 1.0, 0.0), jnp.where(xi > x, 1.0, 0.0))
            ahead[t] = ahead[t] + inc
    ahead = jnp.concatenate(ahead, axis=0)
    return jnp.where((ahead < n_sel) & (xt > 0.1 * MASKED), 1.0, 0.0)


def _block_scores(p_slc, blk_t, n_blocks, axis=1):
    sj = lax.broadcasted_iota(jnp.int32, p_slc.shape, axis)
    causal = (sj <= blk_t) & (sj < n_blocks)
    forced = causal & ((sj == 0) | (sj >= blk_t - 1))
    score = jnp.where(forced, p_slc + FORCE_BONUS, p_slc)
    return jnp.where(causal, score, MASKED)


def _nsa_prompt_kernel(q_ref, gate_ref, kck_ref, kcv_ref, kts_ref, vsa_ref, ktw_ref, vwa_ref, ovl_ref, exp_ref,
                       gsel_ref, gob_ref, out_ref, s_scr, mx_scr, acc_scr, *, n_cmp, n_heads):
    blk = pl.program_id(1)
    start = blk * Q_BLOCK
    q_per_kv = n_heads // N_KV
    q = q_ref[...] * (LOG2_E * HEAD_DIM ** -0.5)
    ncp = kcv_ref.shape[1]
    n_blocks = ovl_ref.shape[0]
    tpos = start + lax.broadcasted_iota(jnp.int32, (Q_BLOCK, 1), 0)
    cn = lax.broadcasted_iota(jnp.int32, (1, ncp), 1)
    cmp_mask = ((cn * CMP_STRIDE + CMP_LEN - 1 <= tpos) & (cn < n_cmp)).astype(F32)
    cmp_mask = jnp.concatenate([cmp_mask] * (N_KV * q_per_kv), axis=0) > 0.5
    grp_lanes = [slice(g * HEAD_DIM, (g + 1) * HEAD_DIM) for g in range(N_KV)]
    qgs = [_stack_heads(q, g, q_per_kv) for g in range(N_KV)]
    qbs = [x.astype(BF16) for x in qgs]

    def normalised(acc, g):
        return (acc / pltpu.roll(acc, HEAD_DIM, 1))[:, grp_lanes[g]]

    o_cmp, p_slc = [], []
    ovl = ovl_ref[...]
    grp_rows = [slice(g * q_per_kv * Q_BLOCK, (g + 1) * q_per_kv * Q_BLOCK) for g in range(N_KV)]
    s = jnp.concatenate([_dot(qbs[g], kck_ref[0, grp_lanes[g], :]) for g in range(N_KV)], axis=0)
    s = jnp.where(cmp_mask, s, MASKED)
    e = jnp.where(cmp_mask, jnp.exp2(s - jnp.max(s, axis=-1, keepdims=True)), 0.0)
    p_all = e / jnp.maximum(jnp.sum(e, axis=-1, keepdims=True), 1e-30)
    for g in range(N_KV):
        p = p_all[grp_rows[g]]
        o_cmp.append(_dot(p.astype(BF16), kcv_ref[0, :, grp_lanes[g]]))
        p_sum = p[:Q_BLOCK]
        for h in range(1, q_per_kv):
            p_sum = p_sum + p[h * Q_BLOCK:(h + 1) * Q_BLOCK]
        p_hi = p_sum.astype(BF16)
        p_lo = (p_sum - p_hi.astype(F32)).astype(BF16)
        p_slc.append(_dot_nt(ovl, p_hi) + _dot_nt(ovl, p_lo))
    blk_t = (start + lax.broadcasted_iota(jnp.int32, (1, Q_BLOCK), 1)) // SLC_BLOCK
    score = _block_scores(jnp.concatenate(p_slc, axis=1), jnp.concatenate([blk_t] * N_KV, axis=1), n_blocks, axis=0)
    sel = _select_mask_t(score, min(N_SELECT, n_blocks)) * SEL_BONUS
    sel = jnp.concatenate([sel, jnp.zeros((LANES - n_blocks, N_KV * Q_BLOCK), F32)], axis=0)
    q_aug = []
    for g in range(N_KV):
        sel_g = sel[:, g * Q_BLOCK:(g + 1) * Q_BLOCK].T[:, :n_blocks]
        q_aug.append(jnp.concatenate([qgs[g], jnp.concatenate([sel_g] * q_per_kv, axis=0)], axis=1).astype(BF16))

    ws = pl.multiple_of(jnp.maximum(start - WINDOW, 0), LANES)
    dpos = tpos - (ws + lax.broadcasted_iota(jnp.int32, (1, WIN_KEYS), 1))
    win_bias = jnp.where((dpos >= 0) & (dpos < WINDOW), 0.0, MASKED)
    win_bias = jnp.concatenate([win_bias] * (N_KV * q_per_kv), axis=0)
    s = jnp.concatenate([_dot(qbs[g], ktw_ref[0, grp_lanes[g], pl.ds(ws, WIN_KEYS)]) for g in range(N_KV)], axis=0)
    s = s + win_bias
    p_all = jnp.exp2(s - jnp.max(s, axis=-1, keepdims=True)).astype(BF16)
    o_win = [normalised(_dot(p_all[grp_rows[g]], vwa_ref[pl.ds(ws, WIN_KEYS), g * K_LANES:(g + 1) * K_LANES]), g)
             for g in range(N_KV)]

    last = (start + Q_BLOCK - 1) // SLC_TILE
    lane_tiles = SLC_TILE // LANES
    mx_scr[...] = jnp.full(mx_scr.shape, MASKED, F32)
    acc_scr[...] = jnp.zeros(acc_scr.shape, F32)

    def score_tile(kt, masked):
        off = pl.multiple_of(kt * SLC_TILE, SLC_TILE)
        for g in range(N_KV):
            keys = jnp.concatenate([kts_ref[0, grp_lanes[g], pl.ds(off, SLC_TILE)],
                                    exp_ref[:, pl.ds(off, SLC_TILE)]], axis=0)
            s = _dot(q_aug[g], keys)
            if masked:
                row = lax.broadcasted_iota(jnp.int32, (q_per_kv * Q_BLOCK, 1), 0) % Q_BLOCK
                s = jnp.where(off + lax.broadcasted_iota(jnp.int32, (1, SLC_TILE), 1) <= start + row, s, MASKED)
            s_scr[g, :, pl.ds(off, SLC_TILE)] = s
            m = s[:, :LANES]
            for i in range(1, lane_tiles):
                m = jnp.maximum(m, s[:, i * LANES:(i + 1) * LANES])
            mx_scr[g] = jnp.maximum(mx_scr[g], m)

    def pair_loop(n, tile):
        def pair(i, carry):
            tile(2 * i)
            tile(2 * i + 1)
            return carry

        lax.fori_loop(0, n // 2, pair, 0)

        @pl.when(n % 2 == 1)
        def _():
            tile(n - 1)

    pair_loop(last, lambda kt: score_tile(kt, False))
    score_tile(last, True)
    for g in range(N_KV):
        mx_scr[g] = jnp.broadcast_to(jnp.max(mx_scr[g], axis=-1, keepdims=True), mx_scr.shape[1:])

    def value_tile(kt):
        off = pl.multiple_of(kt * SLC_TILE, SLC_TILE)
        for g in range(N_KV):
            p = jnp.exp2(s_scr[g, :, pl.ds(off, SLC_TILE)] - jnp.concatenate([mx_scr[g]] * lane_tiles, axis=1))
            acc_scr[g] += _dot(p.astype(BF16), vsa_ref[pl.ds(off, SLC_TILE), g * K_LANES:(g + 1) * K_LANES])

    pair_loop(last + 1, value_tile)
    o_slc = [normalised(acc_scr[g], g) for g in range(N_KV)]

    gates = gate_ref[...]
    g_hi = gates.astype(BF16)
    g_lo = (gates - g_hi.astype(F32)).astype(BF16)
    out = None
    for j, branch in enumerate((o_cmp, o_slc, o_win)):
        spread = _dot(g_hi, gsel_ref[j]) + _dot(g_lo, gsel_ref[j])
        o = jnp.concatenate([branch[g][h * Q_BLOCK:(h + 1) * Q_BLOCK] for g in range(N_KV) for h in range(q_per_kv)],
                            axis=1)
        out = spread * o if out is None else out + spread * o
    out_ref[...] = _rms(out, gob_ref[...]).astype(BF16)


def _nsa_prompt(q, gates, kck_t, kcv, kts, vsa, ktw, vwa, w, b, t):
    n, b_width = q.shape
    n_heads = b_width // HEAD_DIM
    assert t % SLC_TILE == 0 and t >= WIN_KEYS
    nqb = t // Q_BLOCK
    ncp = kcv.shape[1]
    n_cmp = ncp - 1
    n_blocks = (n_cmp + 1) * CMP_STRIDE // SLC_BLOCK
    assert n_blocks % SUBLANES == 0 and n_blocks <= LANES
    ovl = _overlap_matrix(ncp, n_blocks, n_blocks).T.astype(BF16)
    key_blk = jnp.arange(t, dtype=jnp.int32)[None, :] // SLC_BLOCK
    expand = (key_blk == jnp.arange(n_blocks, dtype=jnp.int32)[:, None]).astype(BF16)
    col = jnp.arange(LANES, dtype=jnp.int32)[None, :, None]
    head = jnp.arange(b_width, dtype=jnp.int32)[None, None, :] // HEAD_DIM
    gate_sel = (col == head * 3 + jnp.arange(3, dtype=jnp.int32)[:, None, None]).astype(BF16)
    tok = lambda i, j: (i * nqb + j, 0)
    seq3 = lambda i, j: (i, 0, 0)
    seq2 = lambda i, j: (i, 0)
    rows = (n_heads // N_KV) * Q_BLOCK
    return pl.pallas_call(
        functools.partial(_nsa_prompt_kernel, n_cmp=n_cmp, n_heads=n_heads),
        grid=(b, nqb),
        in_specs=[pl.BlockSpec((Q_BLOCK, b_width), tok), pl.BlockSpec((Q_BLOCK, LANES), tok),
                  pl.BlockSpec((1, K_LANES, ncp), seq3), pl.BlockSpec((1, ncp, K_LANES), seq3),
                  pl.BlockSpec((1, K_LANES, t), seq3), pl.BlockSpec((t, N_KV * K_LANES), seq2),
                  pl.BlockSpec((1, K_LANES, t), seq3), pl.BlockSpec((t, N_KV * K_LANES), seq2),
                  _full(ovl.shape), _full(expand.shape), _full(gate_sel.shape), _full(w["g_out_b"].shape)],
        out_specs=pl.BlockSpec((Q_BLOCK, b_width), tok),
        out_shape=jax.ShapeDtypeStruct((n, b_width), BF16),
        scratch_shapes=[pltpu.VMEM((N_KV, rows, t), F32), pltpu.VMEM((N_KV, rows, LANES), F32),
                        pltpu.VMEM((N_KV, rows, K_LANES), F32)],
        compiler_params=_params("parallel", "arbitrary"), name="nsa_prompt",
    )(q, gates, kck_t, kcv, kts, vsa, ktw, vwa, ovl, expand, gate_sel, w["g_out_b"])


def _overlap_matrix(rows, cols, n_blocks):
    ci = jnp.arange(rows, dtype=jnp.int32)[:, None]
    sj = jnp.arange(cols, dtype=jnp.int32)[None, :]
    hit = (ci * CMP_STRIDE < (sj + 1) * SLC_BLOCK) & (ci * CMP_STRIDE + CMP_LEN > sj * SLC_BLOCK) & (sj < n_blocks)
    return hit.astype(F32)


def _compress_sample_kernel(pt_ref, *refs, pages_per_step, n_steps):
    del pt_ref
    pages = refs[:pages_per_step]
    new_ref, w1k_ref, w1v_ref, b1_ref, w2_ref, kc_ref, fsk_scr, fsv_scr = refs[pages_per_step:pages_per_step + 8]
    chunk_scr = refs[pages_per_step + 8:]
    n_chunks = len(chunk_scr) // 2
    chunk_pages = pages_per_step // n_chunks
    j = pl.program_id(1)
    blocks_per_page = PAGE_SIZE // CMP_STRIDE
    chunk_blocks = chunk_pages * blocks_per_page
    for c in range(n_chunks):
        xk_scr, xv_scr = chunk_scr[2 * c], chunk_scr[2 * c + 1]
        for i in range(chunk_pages):
            page = pages[c * chunk_pages + i]
            xk_scr[pl.ds(i * PAGE_SIZE, PAGE_SIZE), :] = page[0, :K_LANES, :].T
            xv_scr[pl.ds(i * PAGE_SIZE, PAGE_SIZE), :] = page[0, K_LANES:, :].T
    for c in range(n_chunks):
        xk_scr, xv_scr = chunk_scr[2 * c], chunk_scr[2 * c + 1]
        fs_k, fs_v = _compress_partial(lambda s: xk_scr[pl.ds(s, chunk_blocks, stride=CMP_STRIDE), :],
                                       lambda s: xv_scr[pl.ds(s, chunk_blocks, stride=CMP_STRIDE), :],
                                       w1k_ref, w1v_ref)
        off = pl.multiple_of((j * n_chunks + c) * chunk_blocks, chunk_blocks)
        fsk_scr[pl.ds(off, chunk_blocks), :] = fs_k
        fsv_scr[pl.ds(off, chunk_blocks), :] = fs_v

    @pl.when(j == n_steps - 1)
    def _():
        past_blocks = n_steps * pages_per_step * blocks_per_page
        tail = fsk_scr.shape[0] - past_blocks
        new = new_ref[0]
        is_first = lax.broadcasted_iota(jnp.int32, (tail, 1), 0) == 0
        nk = _dot(new[:, :K_LANES].astype(BF16), w1k_ref[0, :K_LANES, :])
        nv = _dot(new[:, K_LANES:].astype(BF16), w1v_ref[0, :K_LANES, :])
        fsk_scr[pl.ds(past_blocks, tail), :] = jnp.where(is_first, nk, 0.0)
        fsv_scr[pl.ds(past_blocks, tail), :] = jnp.where(is_first, nv, 0.0)
        kc_ref[0] = _compress_finish(fsk_scr[...], fsv_scr[...], b1_ref, w2_ref)


def _compress_sample(cache_cmp_t, page_table, new_rows, w):
    db, n_pages = page_table.shape
    pages_per_step, n_chunks = 64, 2
    assert n_pages % pages_per_step == 0
    n_steps = n_pages // pages_per_step
    blocks_per_page = PAGE_SIZE // CMP_STRIDE
    past_blocks = n_pages * blocks_per_page
    nbp = past_blocks + SUBLANES
    weights = [w["w1k"], w["w1v"], w["b_c1"], w["w2"]]
    hid2 = w["w1k"].shape[2]
    page_spec = lambda i: pl.BlockSpec((1, KV_LANES, PAGE_SIZE),
                                       lambda b, j, pt, i=i: (pt[b, j * pages_per_step + i], 0, 0))
    chunk_rows = pages_per_step // n_chunks * PAGE_SIZE
    grid_spec = pltpu.PrefetchScalarGridSpec(
        num_scalar_prefetch=1, grid=(db, n_steps),
        in_specs=[page_spec(i) for i in range(pages_per_step)]
        + [pl.BlockSpec((1, 1, KV_LANES), lambda b, j, pt: (b, 0, 0))]
        + [pl.BlockSpec(a.shape, lambda b, j, pt, nd=a.ndim: (0,) * nd) for a in weights],
        out_specs=pl.BlockSpec((1, nbp, KV_LANES), lambda b, j, pt: (b, 0, 0)),
        scratch_shapes=[pltpu.VMEM((nbp, hid2), F32), pltpu.VMEM((nbp, hid2), F32)]
        + [pltpu.VMEM((chunk_rows, K_LANES), F32)] * (2 * n_chunks))
    return pl.pallas_call(
        functools.partial(_compress_sample_kernel, pages_per_step=pages_per_step, n_steps=n_steps),
        grid_spec=grid_spec, out_shape=jax.ShapeDtypeStruct((db, nbp, KV_LANES), F32),
        compiler_params=_params("parallel", "arbitrary"), name="compress_sample",
    )(page_table, *([cache_cmp_t] * pages_per_step), new_rows[:, None, :], *weights)


def _select_sample_kernel(q_ref, kc_ref, ocmp_ref, psum_ref, *, pos, n_cmp, n_heads):
    q_per_kv = n_heads // N_KV
    q = q_ref[0] * (LOG2_E * HEAD_DIM ** -0.5)
    kc = kc_ref[0]
    ncp = kc.shape[0]
    cn = lax.broadcasted_iota(jnp.int32, (1, ncp), 1)
    mask = (cn * CMP_STRIDE + CMP_LEN - 1 <= pos) & (cn < n_cmp)
    o_rows, p_rows = [], []
    for g in range(N_KV):
        qg = _stack_heads(q, g, q_per_kv).astype(BF16)
        s = jnp.where(mask, _dot_nt(qg, kc[:, g * HEAD_DIM:(g + 1) * HEAD_DIM].astype(BF16)), MASKED)
        e = jnp.where(mask, jnp.exp2(s - jnp.max(s, axis=-1, keepdims=True)), 0.0)
        p = e / jnp.maximum(jnp.sum(e, axis=-1, keepdims=True), 1e-30)
        o_rows.append(_dot(p.astype(BF16), kc[:, K_LANES + g * HEAD_DIM:K_LANES + (g + 1) * HEAD_DIM].astype(BF16)))
        p_rows.append(jnp.sum(p, axis=0, keepdims=True))
    ocmp_ref[0] = jnp.concatenate(o_rows, axis=0)
    psum_ref[0] = jnp.concatenate(p_rows, axis=0)


def _pick_sample_kernel(psum_ref, ovl_ref, idx_ref, *, pos, n_blocks):
    p = psum_ref[...]
    p_hi = p.astype(BF16)
    p_lo = (p - p_hi.astype(F32)).astype(BF16)
    ovl = ovl_ref[...]
    score = _block_scores(_dot(p_hi, ovl) + _dot(p_lo, ovl), pos // SLC_BLOCK, n_blocks)
    _, picks = _select_blocks(score, min(N_SELECT, n_blocks))
    lane = lax.broadcasted_iota(jnp.int32, idx_ref.shape, 1)
    out = jnp.full(idx_ref.shape, -1.0, F32)
    for i, (idx, ok) in enumerate(picks):
        out = jnp.where((lane == i) & ok, idx, out)
    idx_ref[...] = out.astype(jnp.int32)


def _select_sample(q, kc, pos, n_cmp, n_blocks):
    db, b_width = q.shape
    n_heads = b_width // HEAD_DIM
    ncp = kc.shape[1]
    o_cmp, p_sum = pl.pallas_call(
        functools.partial(_select_sample_kernel, pos=pos, n_cmp=n_cmp, n_heads=n_heads),
        grid=(db,),
        in_specs=[pl.BlockSpec((1, 1, b_width), lambda i: (i, 0, 0)),
                  pl.BlockSpec((1, ncp, KV_LANES), lambda i: (i, 0, 0))],
        out_specs=[pl.BlockSpec((1, n_heads, HEAD_DIM), lambda i: (i, 0, 0)),
                   pl.BlockSpec((1, N_KV, ncp), lambda i: (i, 0, 0))],
        out_shape=[jax.ShapeDtypeStruct((db, n_heads, HEAD_DIM), F32),
                   jax.ShapeDtypeStruct((db, N_KV, ncp), F32)],
        compiler_params=_params("parallel"), name="select_sample",
    )(q[:, None, :], kc)
    nsp = -(-n_blocks // LANES) * LANES
    ovl = _overlap_matrix(ncp, nsp, n_blocks).astype(BF16)
    idx = pl.pallas_call(
        functools.partial(_pick_sample_kernel, pos=pos, n_blocks=n_blocks),
        grid=(1,),
        in_specs=[_full((db * N_KV, ncp)), _full(ovl.shape)],
        out_specs=_full((db * N_KV, LANES)),
        out_shape=jax.ShapeDtypeStruct((db * N_KV, LANES), jnp.int32),
        compiler_params=_params("arbitrary"), name="pick_sample",
    )(p_sum.reshape(db * N_KV, ncp), ovl)
    return o_cmp, idx.reshape(db, N_KV, LANES)


def _attend_sample_kernel(page_ref, *refs, n_sel, past_blocks, n_heads):
    del page_ref
    n_slots = N_KV * n_sel
    pages = refs[:n_slots]
    (q_ref, gate_ref, ocmp_ref, idx_ref, newslc_ref, win_ref, newwin_ref, exp_ref, gob_ref,
     out_ref, winout_ref) = refs[n_slots:]
    q_per_kv = n_heads // N_KV
    q = q_ref[0] * (HEAD_DIM ** -0.5)
    lane = lax.broadcasted_iota(jnp.int32, (1, K_LANES), 1)
    wb = win_ref.shape[2]
    is_last = lax.broadcasted_iota(jnp.int32, (1, wb), 1) == wb - 1
    win = jnp.where(is_last, newwin_ref[0], pltpu.roll(win_ref[0], wb - 1, 1))
    winout_ref[0] = win
    win_k = win[:K_LANES].astype(BF16)
    win_v = win[K_LANES:].astype(BF16)
    new_slc = newslc_ref[0]
    idx = idx_ref[0].astype(F32)
    n_keys = n_sel * PAGE_SIZE
    key_half = (lax.broadcasted_iota(jnp.int32, (1, n_keys), 1) % PAGE_SIZE) // SLC_BLOCK
    o_slc, o_win = [], []
    for g in range(N_KV):
        qpad = jnp.concatenate(
            [jnp.where(lane // HEAD_DIM == g,
                       jnp.concatenate([q[:, (g * q_per_kv + h) * HEAD_DIM:(g * q_per_kv + h + 1) * HEAD_DIM]] * N_KV,
                                       axis=1), 0.0)
             for h in range(q_per_kv)], axis=0)
        qpb = qpad.astype(BF16)
        kt = jnp.concatenate([pages[g * n_sel + i][0, :K_LANES, :] for i in range(n_sel)], axis=1).astype(BF16)
        vt = jnp.concatenate([pages[g * n_sel + i][0, K_LANES:, :] for i in range(n_sel)], axis=1).astype(BF16)
        s = _dot(qpb, kt)
        idg = idx[g:g + 1, :]
        idk = jnp.dot(idg, exp_ref[...], precision=lax.Precision.HIGHEST, preferred_element_type=F32)
        parity = idk - 2.0 * jnp.floor(idk * 0.5)
        key_ok = (idk >= 0.0) & (idk < past_blocks) & (parity == key_half.astype(F32))
        s = jnp.where(key_ok, s, MASKED)
        has_new = jnp.max(jnp.where(idg == past_blocks, 1.0, 0.0), axis=-1, keepdims=True) > 0.5
        s_new = jnp.sum(qpad * new_slc[:, :K_LANES], axis=-1, keepdims=True)
        s_new = jnp.where(has_new, s_new, MASKED)
        m = jnp.maximum(jnp.max(s, axis=-1, keepdims=True), s_new)
        e = jnp.where(key_ok, jnp.exp(s - m), 0.0)
        e_new = jnp.where(has_new, jnp.exp(s_new - m), 0.0)
        den = jnp.maximum(jnp.sum(e, axis=-1, keepdims=True) + e_new, 1e-30)
        o = (_dot_nt(e.astype(BF16), vt) + e_new * new_slc[:, K_LANES:]) / den
        o_slc.append(o[:, g * HEAD_DIM:(g + 1) * HEAD_DIM])
        s = _dot(qpb, win_k)
        e = jnp.exp(s - jnp.max(s, axis=-1, keepdims=True))
        o = _dot_nt(e.astype(BF16), win_v) / jnp.sum(e, axis=-1, keepdims=True)
        o_win.append(o[:, g * HEAD_DIM:(g + 1) * HEAD_DIM])
    gates = gate_ref[0]
    o = (gates[:, 0:1] * ocmp_ref[0] + gates[:, 1:2] * jnp.concatenate(o_slc, axis=0)
         + gates[:, 2:3] * jnp.concatenate(o_win, axis=0))
    ms = jnp.sum(jnp.sum(o * o, axis=-1, keepdims=True), axis=0, keepdims=True) / (n_heads * HEAD_DIM)
    out_ref[0] = o * lax.rsqrt(ms + NORM_EPS) * gob_ref[...]


def _attend_sample(q, gates, o_cmp, idx, cache_slc_t, page_table, new_slc, win_t, new_win, w, past_blocks):
    db, b_width = q.shape
    n_heads = b_width // HEAD_DIM
    n_sel = min(N_SELECT, past_blocks + 1)
    sub_per_page = PAGE_SIZE // SLC_BLOCK
    jp = jnp.clip(idx[:, :, :n_sel], 0, past_blocks - 1)
    page = jnp.take_along_axis(page_table, (jp // sub_per_page).reshape(db, -1), axis=1).astype(jnp.int32)
    n_keys = n_sel * PAGE_SIZE
    expand = (jnp.arange(n_keys, dtype=jnp.int32)[None, :] // PAGE_SIZE
              == jnp.arange(LANES, dtype=jnp.int32)[:, None]).astype(F32)
    gob = w["g_out_b"].reshape(n_heads, HEAD_DIM)
    wb = win_t.shape[2]
    per_seq = lambda shape: pl.BlockSpec((1,) + shape, lambda b, pg: (b, 0, 0))
    page_spec = lambda i: pl.BlockSpec((1, KV_LANES, PAGE_SIZE), lambda b, pg, i=i: (pg[b, i], 0, 0))
    grid_spec = pltpu.PrefetchScalarGridSpec(
        num_scalar_prefetch=1, grid=(db,),
        in_specs=[page_spec(i) for i in range(N_KV * n_sel)]
        + [per_seq((1, b_width)), per_seq((n_heads, 3)), per_seq((n_heads, HEAD_DIM)), per_seq((N_KV, LANES)),
           per_seq((1, KV_LANES)), per_seq((KV_LANES, wb)), per_seq((KV_LANES, 1)),
           pl.BlockSpec(expand.shape, lambda b, pg: (0, 0)), pl.BlockSpec(gob.shape, lambda b, pg: (0, 0))],
        out_specs=[per_seq((n_heads, HEAD_DIM)), per_seq((KV_LANES, wb))])
    return pl.pallas_call(
        functools.partial(_attend_sample_kernel, n_sel=n_sel, past_blocks=past_blocks, n_heads=n_heads),
        grid_spec=grid_spec,
        out_shape=[jax.ShapeDtypeStruct((db, n_heads, HEAD_DIM), F32),
                   jax.ShapeDtypeStruct((db, KV_LANES, wb), F32)],
        compiler_params=_params("parallel"), name="attend_sample",
    )(page, *([cache_slc_t] * (N_KV * n_sel)), q[:, None, :], gates[:, :n_heads * 3].reshape(db, n_heads, 3), o_cmp,
      idx, new_slc[:, None, :], win_t, new_win[:, :, None], expand, gob)


def _ffn_kernel(x_ref, ma_ref, mb_ref, woa_ref, wob_ref, gffn_ref, wgate_ref, wup_ref, wdown_ref, gfin_ref, y_ref,
                acc_scr, xn_scr):
    c = pl.program_id(1)

    @pl.when(c == 0)
    def _():
        x = x_ref[...] + _dot(ma_ref[...], woa_ref[...]) + _dot(mb_ref[...], wob_ref[...])
        acc_scr[...] = x
        xn_scr[...] = _rms(x, gffn_ref[...]).astype(BF16)

    xn = xn_scr[...]
    hid = jax.nn.silu(_dot(xn, wgate_ref[...])) * _dot(xn, wup_ref[...])
    acc_scr[...] += _dot(hid.astype(BF16), wdown_ref[...])

    @pl.when(c == pl.num_programs(1) - 1)
    def _():
        y_ref[...] = _rms(acc_scr[...], gfin_ref[...])


def _output_ffn(x2, mix_a, mix_b, w, tm):
    n, d = x2.shape
    d_ff = w["w_down"].shape[0]
    ff_chunks = 2
    step = d_ff // ff_chunks
    assert n % tm == 0 and d_ff % ff_chunks == 0 and step % LANES == 0
    row = lambda i, c: (i, 0)
    fixed = lambda a: pl.BlockSpec(a.shape, lambda i, c: (0, 0))
    return pl.pallas_call(
        _ffn_kernel,
        grid=(n // tm, ff_chunks),
        in_specs=[pl.BlockSpec((tm, d), row), pl.BlockSpec((tm, mix_a.shape[1]), row),
                  pl.BlockSpec((tm, mix_b.shape[1]), row), fixed(w["wo_a"]), fixed(w["wo_b"]), fixed(w["g_ffn"]),
                  pl.BlockSpec((d, step), lambda i, c: (0, c)), pl.BlockSpec((d, step), lambda i, c: (0, c)),
                  pl.BlockSpec((step, d), lambda i, c: (c, 0)), fixed(w["g_final"])],
        out_specs=pl.BlockSpec((tm, d), row),
        out_shape=jax.ShapeDtypeStruct((n, d), F32),
        scratch_shapes=[pltpu.VMEM((tm, d), F32), pltpu.VMEM((tm, d), BF16)],
        compiler_params=_params("parallel", "arbitrary"), name="output_ffn",
    )(x2, mix_a, mix_b, w["wo_a"], w["wo_b"], w["g_ffn"], w["w_gate"], w["w_up"], w["w_down"], w["g_final"])


def _prepare_weights(l, g_attn, w_in, g_sgu, w_s, b_s, w_c1, b_c1, w_c2, g_out_a, g_out_b, w_out,
                     g_ffn, w_gate_up, w_down, g_final):
    a_width = g_sgu.shape[1]
    b_width = g_out_b.shape[1]
    n_heads = b_width // HEAD_DIM
    d_ff = w_down.shape[1]
    hid = b_c1.shape[2]
    o1 = 2 * a_width
    o2 = o1 + b_width
    o3 = o2 + 3 * KV_LANES
    wi = w_in[l]
    wg = jnp.pad(wi[:, o3:], ((0, 0), (0, LANES - 3 * n_heads)))
    row = lambda a: a.reshape(1, -1)
    w1 = w_c1[l].reshape(2, 2, CMP_STRIDE, HEAD_DIM, hid)
    w1 = jnp.transpose(w1, (0, 2, 3, 1, 4)).reshape(2, CMP_STRIDE, HEAD_DIM, 2 * hid)
    eye = jnp.eye(N_KV, dtype=F32)
    w1 = jnp.einsum("gh,csdn->csgdhn", eye, w1).reshape(2, CMP_STRIDE // CMP_STACK, CMP_STACK * K_LANES, N_KV * 2 * hid)
    w2 = jnp.einsum("gh,cne->cgnhe", eye, w_c2[l]).reshape(2, N_KV * hid, K_LANES)
    return {
        "g_attn": row(g_attn[l]), "wuv": wi[:, :o1].astype(BF16), "wq": wi[:, o1:o2].astype(BF16),
        "wkv": wi[:, o2:o3].astype(BF16), "wg": wg.astype(BF16),
        "g_sgu": row(g_sgu[l]), "g_out_a": row(g_out_a[l]), "g_out_b": row(g_out_b[l]),
        "w_s": w_s[l], "bs_full": jnp.repeat(b_s[l].T, LANES, axis=1),
        "ws0": row(jnp.repeat(w_s[l][:, 0, 0], LANES)), "bs0": row(jnp.repeat(b_s[l][:, 0], LANES)),
        "w1k": w1[0].astype(BF16), "w1v": w1[1].astype(BF16), "b_c1": b_c1[l], "w2": w2.astype(BF16),
        "wo_a": w_out[l][:a_width].astype(BF16), "wo_b": w_out[l][a_width:].astype(BF16),
        "g_ffn": row(g_ffn[l]), "w_gate": w_gate_up[l][:, :d_ff].astype(BF16),
        "w_up": w_gate_up[l][:, d_ff:].astype(BF16), "w_down": w_down[l].astype(BF16), "g_final": row(g_final),
    }


def kernel(x_prompt, x_sample, cache_cmp_kv, cache_slc_kv, state_win_kv, page_table, g_attn, w_in, g_sgu, w_s, b_s,
           w_c1, b_c1, w_c2, g_out_a, g_out_b, w_out, g_ffn, w_gate_up, w_down, g_final):
    depth = w_in.shape[0]
    b, t, d = x_prompt.shape
    db, t_s, _ = x_sample.shape
    assert depth == 1 and t_s == 1
    n_pages = page_table.shape[1]
    past = n_pages * PAGE_SIZE
    wb = state_win_kv.shape[2]
    assert wb == WINDOW and past % SLC_BLOCK == 0
    l = 0
    w = _prepare_weights(l, g_attn, w_in, g_sgu, w_s, b_s, w_c1, b_c1, w_c2, g_out_a, g_out_b, w_out,
                         g_ffn, w_gate_up, w_down, g_final)
    xp = x_prompt.reshape(b * t, d)
    mix_a, q, kvc, gates, kvc_t, kvs_t, kvw_t, kts, vs, ktw, vw = _in_projection(
        xp, jnp.arange(t, dtype=jnp.int32), w, prompt_shape=(b, t))
    kck_t, kcv = _compress_prompt(kvc.reshape(b, t, KV_LANES), w)
    mix_b = _nsa_prompt(q, gates, kck_t, kcv, kts, vs, ktw, vw, w, b, t)
    y_prompt = _output_ffn(xp, mix_a, mix_b, w, tm=512).reshape(b, t, d)

    xs = x_sample.reshape(db, d)
    pos_s = past + jnp.zeros((db,), jnp.int32)
    mix_a_s, q_s, kvc_s, kvs_s, kvw_s, gates_s, v_rows = _in_projection(xs, pos_s, w)
    kc_s = _compress_sample(_feature_major(cache_cmp_kv[l]), page_table, kvc_s, w)
    lp = -(-(past + t_s) // SLC_BLOCK) * SLC_BLOCK
    n_cmp_s = lp // CMP_STRIDE - 1
    n_blocks_s = (n_cmp_s + 1) * CMP_STRIDE // SLC_BLOCK
    o_cmp_s, idx_s = _select_sample(q_s, kc_s, past, n_cmp_s, n_blocks_s)
    mix_b_s, win_new_t = _attend_sample(q_s, gates_s, o_cmp_s, idx_s, _feature_major(cache_slc_kv[l]), page_table,
                                        kvs_s, _feature_major(state_win_kv[l]), kvw_s, w, past // SLC_BLOCK)
    y_sample = _output_ffn(xs, mix_a_s, mix_b_s.reshape(db, -1).astype(BF16), w, tm=db).reshape(db, t_s, d)

    kv_shape = (2, N_KV, HEAD_DIM)
    return (y_prompt, y_sample,
            _row_major(kvc_t)[None], _row_major(kvs_t)[None], _row_major(kvw_t[:, :, t - min(WINDOW, t):])[None],
            kvc_s.reshape(1, db, t_s, *kv_shape), kvs_s.reshape(1, db, t_s, *kv_shape),
            _row_major(win_new_t)[None], v_rows.reshape(1, db, t_s, -1))


def _feature_major(kv):
    n, rows = kv.shape[:2]
    return jnp.transpose(kv, (0, 2, 3, 4, 1)).reshape(n, KV_LANES, rows)


def _row_major(kv_t):
    n, _, rows = kv_t.shape
    return jnp.transpose(kv_t.reshape(n, 2, N_KV, HEAD_DIM, rows), (0, 4, 1, 2, 3))
```

```python
import functools

import jax
import jax.numpy as jnp
from jax import lax
from jax.experimental import pallas as pl
from jax.experimental.pallas import tpu as pltpu

F32 = jnp.float32
BF16 = jnp.bfloat16

A_GROUPS = 4
CHUNK = 128
HEAD_DIM = 64
N_KV = 2
ROT_DIM = HEAD_DIM // 4
ROPE_THETA = 500000.0
CMP_LEN = 32
CMP_STRIDE = 16
SLC_BLOCK = 64
N_SELECT = 16
WINDOW = 512
Q_BLOCK = 128
FORCE_BONUS = 1000.0
PAGE_SIZE = 128
NORM_EPS = 1e-6
MASKED = -1e30
LOG2_E = 1.4426950408889634
SEL_BONUS = 16384.0

LANES = 128
SUBLANES = 8
VMEM_LIMIT_BYTES = 56 * 1024 * 1024

KV_LANES = 2 * N_KV * HEAD_DIM
K_LANES = N_KV * HEAD_DIM
CMP_STACK = 2
SLC_TILE = 512
TILES_PER_TRIP = 4
WIN_KEYS = WINDOW + Q_BLOCK


def _rms(x, g):
    return x * lax.rsqrt(jnp.mean(x * x, axis=-1, keepdims=True) + NORM_EPS) * g


def _dot(a, b):
    return jnp.dot(a, b, preferred_element_type=F32)


def _dot_nt(a, b, precision=None):
    return lax.dot_general(a, b, (((1,), (1,)), ((), ())), precision=precision,
                           preferred_element_type=F32)


def _rope(z, rc, rs1, rs2):
    return z * rc + pltpu.roll(z, LANES - ROT_DIM // 2, 1) * rs1 + pltpu.roll(z, ROT_DIM // 2, 1) * rs2


def _project(x_ref, gattn_ref, wuv_ref, wq_ref, wkv_ref, wg_ref, rc_ref, rs1_ref, rs2_ref):
    xn = _rms(x_ref[...], gattn_ref[...]).astype(BF16)
    rc, rs1, rs2 = rc_ref[...], rs1_ref[...], rs2_ref[...]
    zuv = _dot(xn, wuv_ref[...])
    a_width = zuv.shape[1] // 2
    zq = _dot(xn, wq_ref[...])
    q = jnp.concatenate([_rope(zq[:, i * LANES:(i + 1) * LANES], rc, rs1, rs2)
                         for i in range(zq.shape[1] // LANES)], axis=1)
    zkv = _dot(xn, wkv_ref[...])
    branches = []
    for br in range(3):
        k = _rope(zkv[:, br * KV_LANES:br * KV_LANES + K_LANES], rc, rs1, rs2)
        v = zkv[:, br * KV_LANES + K_LANES:(br + 1) * KV_LANES]
        branches.append((k, v))
    gates = jax.nn.sigmoid(_dot(xn, wg_ref[...]))
    return zuv[:, :a_width], zuv[:, a_width:], q, branches, gates


def _gmlp_norm_v(v, gsgu):
    v = jax.nn.gelu(v)
    return jnp.concatenate([_rms(v[:, g * LANES:(g + 1) * LANES], gsgu[:, g * LANES:(g + 1) * LANES])
                            for g in range(A_GROUPS)], axis=1)


def _inproj_prompt_kernel(x_ref, gattn_ref, wuv_ref, wq_ref, wkv_ref, wg_ref, rc_ref, rs1_ref, rs2_ref,
                          ws_ref, bs_ref, gsgu_ref, goa_ref,
                          mixa_ref, q_ref, kvc_ref, gate_ref, kvct_ref, kvst_ref, kvwt_ref,
                          kts_ref, vs_ref, ktw_ref, vw_ref):
    u, v, q, branches, gates = _project(x_ref, gattn_ref, wuv_ref, wq_ref, wkv_ref, wg_ref,
                                        rc_ref, rs1_ref, rs2_ref)
    tm = u.shape[0]
    u = jax.nn.gelu(u)
    vg = _gmlp_norm_v(v, gsgu_ref[...]).astype(BF16)
    row = lax.broadcasted_iota(jnp.int32, (CHUNK, CHUNK), 0)
    col = lax.broadcasted_iota(jnp.int32, (CHUNK, CHUNK), 1)
    bias = bs_ref[...]
    parts = []
    for g in range(A_GROUPS):
        w = jnp.where(row >= col, ws_ref[g], 0.0).astype(BF16)
        s = jnp.concatenate(
            [_dot(w, vg[c * CHUNK:(c + 1) * CHUNK, g * LANES:(g + 1) * LANES]) for c in range(tm // CHUNK)],
            axis=0)
        s = s + jnp.concatenate([bias[:, g * LANES:(g + 1) * LANES]] * (tm // CHUNK), axis=0)
        parts.append(u[:, g * LANES:(g + 1) * LANES] * s)
    mixa_ref[...] = _rms(jnp.concatenate(parts, axis=1), goa_ref[...]).astype(BF16)
    q_ref[...] = q
    gate_ref[...] = gates
    kvc_ref[...] = jnp.concatenate(branches[0], axis=1)
    kts = []
    for ref, (k, v_) in zip((kvct_ref, kvst_ref, kvwt_ref), branches):
        kt = k.T
        ref[0, :K_LANES, :] = kt
        ref[0, K_LANES:, :] = v_.T
        kts.append(kt)
    own = [lax.broadcasted_iota(jnp.int32, (1, K_LANES), 1) // HEAD_DIM == g for g in range(N_KV)]
    for kt_ref, va_ref, br in ((kts_ref, vs_ref, 1), (ktw_ref, vw_ref, 2)):
        kt_ref[0] = kts[br].astype(BF16)
        va_ref[...] = jnp.concatenate([jnp.where(m, branches[br][1], 1.0) for m in own], axis=1).astype(BF16)


def _inproj_sample_kernel(x_ref, gattn_ref, wuv_ref, wq_ref, wkv_ref, wg_ref, rc_ref, rs1_ref, rs2_ref,
                          ws0_ref, bs0_ref, gsgu_ref, goa_ref,
                          mixa_ref, q_ref, kvc_ref, kvs_ref, kvw_ref, gate_ref, vrow_ref):
    u, v, q, branches, gates = _project(x_ref, gattn_ref, wuv_ref, wq_ref, wkv_ref, wg_ref,
                                        rc_ref, rs1_ref, rs2_ref)
    vg = _gmlp_norm_v(v, gsgu_ref[...])
    o_a = jax.nn.gelu(u) * (vg * ws0_ref[...] + bs0_ref[...])
    mixa_ref[...] = _rms(o_a, goa_ref[...]).astype(BF16)
    vrow_ref[...] = vg
    q_ref[...] = q
    for ref, (k, v_) in zip((kvc_ref, kvs_ref, kvw_ref), branches):
        ref[...] = jnp.concatenate([k, v_], axis=1)
    gate_ref[...] = gates


def _full(shape):
    return pl.BlockSpec(shape, lambda *_: (0,) * len(shape))


def _params(*sem):
    return pltpu.CompilerParams(dimension_semantics=sem, vmem_limit_bytes=VMEM_LIMIT_BYTES)


def _in_projection(x2, pos, w, *, prompt_shape=None):
    n, d = x2.shape
    rc, rs1, rs2 = _rope_tables(pos)
    a_width = w["wuv"].shape[1] // 2
    b_width = w["wq"].shape[1]
    weights = [w["g_attn"], w["wuv"], w["wq"], w["wkv"], w["wg"]]
    wspecs = [_full(a.shape) for a in weights]
    tail = [w["g_sgu"], w["g_out_a"]]
    if prompt_shape is None:
        tm, grid = n, (1,)
        row = lambda i: (i, 0)
        rope_map = row
        gm = [w["ws0"], w["bs0"]]
    else:
        b, t = prompt_shape
        tm = 512
        assert t % tm == 0 and tm % CHUNK == 0
        tpb = t // tm
        grid = (b * tpb,)
        row = lambda i: (i, 0)
        rope_map = lambda i: (i % tpb, 0)
        gm = [w["w_s"], w["bs_full"]]
    rspec = pl.BlockSpec((tm, LANES), rope_map)
    in_specs = ([pl.BlockSpec((tm, d), row)] + wspecs + [rspec] * 3
                + [_full(a.shape) for a in gm] + [_full(a.shape) for a in tail])
    if prompt_shape is None:
        kern = _inproj_sample_kernel
        outs = [((n, a_width), BF16), ((n, b_width), F32), ((n, KV_LANES), F32), ((n, KV_LANES), F32),
                ((n, KV_LANES), F32), ((n, LANES), F32), ((n, a_width), F32)]
        out_specs = [pl.BlockSpec((tm, s[1]), row) for s, _ in outs]
    else:
        kern = _inproj_prompt_kernel
        kt_map = lambda i: (i // tpb, 0, i % tpb)
        outs = [((n, a_width), BF16), ((n, b_width), F32), ((n, KV_LANES), F32), ((n, LANES), F32)]
        out_specs = [pl.BlockSpec((tm, s[1]), row) for s, _ in outs]
        outs += [((b, KV_LANES, t), F32)] * 3
        out_specs += [pl.BlockSpec((1, KV_LANES, tm), kt_map)] * 3
        outs += [((b, K_LANES, t), BF16), ((n, N_KV * K_LANES), BF16)] * 2
        out_specs += [pl.BlockSpec((1, K_LANES, tm), kt_map), pl.BlockSpec((tm, N_KV * K_LANES), row)] * 2
    return pl.pallas_call(
        kern, grid=grid, in_specs=in_specs, out_specs=out_specs,
        out_shape=[jax.ShapeDtypeStruct(s, dt) for s, dt in outs],
        compiler_params=_params("parallel"), name="in_projection",
    )(x2, *weights, rc, rs1, rs2, *gm, *tail)


def _rope_tables(pos):
    half = ROT_DIM // 2
    inv = ROPE_THETA ** (-jnp.arange(half, dtype=F32) / half)
    ang = pos.astype(F32)[:, None] * inv[None, :]
    cos, sin = jnp.cos(ang), jnp.sin(ang)
    n = pos.shape[0]
    rest0 = jnp.zeros((n, HEAD_DIM - ROT_DIM), F32)
    zero = jnp.zeros((n, half), F32)
    rc = jnp.concatenate([cos, cos, rest0 + 1.0], axis=1)
    rs1 = jnp.concatenate([-sin, zero, rest0], axis=1)
    rs2 = jnp.concatenate([zero, sin, rest0], axis=1)
    return tuple(jnp.tile(a, (1, LANES // HEAD_DIM)) for a in (rc, rs1, rs2))


def _compress_partial(read_k, read_v, w1k_ref, w1v_ref):
    acc_k = acc_v = None
    stack = w1k_ref.shape[1] // K_LANES
    for i in range(CMP_STRIDE // stack):
        rows = range(i * stack, (i + 1) * stack)
        pk = _dot(jnp.concatenate([read_k(s).astype(BF16) for s in rows], axis=1), w1k_ref[i])
        pv = _dot(jnp.concatenate([read_v(s).astype(BF16) for s in rows], axis=1), w1v_ref[i])
        acc_k = pk if acc_k is None else acc_k + pk
        acc_v = pv if acc_v is None else acc_v + pv
    return acc_k, acc_v


def _compress_finish(fs_k, fs_v, b1_ref, w2_ref):
    hid = b1_ref.shape[1]
    outs = []
    for c, fs in enumerate((fs_k, fs_v)):
        hs = []
        for g in range(N_KV):
            first = fs[:, g * 2 * hid:g * 2 * hid + hid]
            second = fs[:, g * 2 * hid + hid:(g + 1) * 2 * hid]
            nxt = pltpu.roll(second, second.shape[0] - 1, 0)
            hs.append(jax.nn.silu(first + nxt + b1_ref[c:c + 1, :]))
        outs.append(_dot(jnp.concatenate(hs, axis=1).astype(BF16), w2_ref[c]))
    return jnp.concatenate(outs, axis=1)


def _compress_prompt_kernel(k_ref, v_ref, w1k_ref, w1v_ref, b1_ref, w2_ref, kck_ref, kcv_ref, kc_scr):
    nb = kcv_ref.shape[1]
    fs_k, fs_v = _compress_partial(lambda s: k_ref[0, pl.ds(s, nb, stride=CMP_STRIDE), :],
                                   lambda s: v_ref[0, pl.ds(s, nb, stride=CMP_STRIDE), :], w1k_ref, w1v_ref)
    kc_scr[...] = _compress_finish(fs_k, fs_v, b1_ref, w2_ref)
    kck_ref[0] = kc_scr[:, :K_LANES].T.astype(BF16)
    kcv_ref[0] = kc_scr[:, K_LANES:].astype(BF16)


def _compress_prompt(kvc, w):
    b, t, _ = kvc.shape
    nb = t // CMP_STRIDE
    weights = [w["w1k"], w["w1v"], w["b_c1"], w["w2"]]
    return pl.pallas_call(
        _compress_prompt_kernel, grid=(b,),
        in_specs=[pl.BlockSpec((1, t, K_LANES), lambda i: (i, 0, 0)), pl.BlockSpec((1, t, K_LANES), lambda i: (i, 0, 1))]
        + [_full(a.shape) for a in weights],
        out_specs=[pl.BlockSpec((1, K_LANES, nb), lambda i: (i, 0, 0)), pl.BlockSpec((1, nb, K_LANES), lambda i: (i, 0, 0))],
        out_shape=[jax.ShapeDtypeStruct((b, K_LANES, nb), BF16), jax.ShapeDtypeStruct((b, nb, K_LANES), BF16)],
        scratch_shapes=[pltpu.VMEM((nb, KV_LANES), F32)],
        compiler_params=_params("parallel"), name="compress_prompt",
    )(kvc, kvc, *weights)


def _stack_heads(q, g, q_per_kv):
    return jnp.concatenate([q[:, (g * q_per_kv + h) * HEAD_DIM:(g * q_per_kv + h + 1) * HEAD_DIM]
                            for h in range(q_per_kv)], axis=0)


def _select_blocks(score, n_sel):
    rows, n = score.shape
    lane = lax.broadcasted_iota(jnp.int32, (rows, n), 1).astype(F32)
    sel = jnp.zeros((rows, n), F32)
    picks = []
    x = score
    for _ in range(n_sel):
        m = jnp.max(x, axis=-1, keepdims=True)
        idx = jnp.min(jnp.where(x == m, lane, float(n)), axis=-1, keepdims=True)
        hit = lane == idx
        ok = m > 0.1 * MASKED
        sel = jnp.where(hit & ok, 1.0, sel)
        x = jnp.where(hit, -3e38, x)
        picks.append((idx, ok))
    return sel, picks


def _select_mask_t(xt, n_sel):
    n, cols = xt.shape
    tiles = [xt[t * SUBLANES:(t + 1) * SUBLANES] for t in range(n // SUBLANES)]
    row = lax.broadcasted_iota(jnp.int32, (SUBLANES, cols), 0)
    ahead = [jnp.zeros((SUBLANES, cols), F32) for _ in tiles]
    for i in range(n):
        xi = xt[i:i + 1, :]
        for t, x in enumerate(tiles):
            first, last = t * SUBLANES, (t + 1) * SUBLANES - 1
            if first > i:
                inc = jnp.where(xi >= x, 1.0, 0.0)
            elif last <= i:
                inc = jnp.where(xi > x, 1.0, 0.0)
            else:
                inc = jnp.where(row + first > i, jnp.where(xi >= x, 1.0, 0.0), jnp.where(xi > x, 1.0, 0.0))
            ahead[t] = ahead[t] + inc
    ahead = jnp.concatenate(ahead, axis=0)
    return jnp.where((ahead < n_sel) & (xt > 0.1 * MASKED), 1.0, 0.0)


def _block_scores(p_slc, blk_t, n_blocks, axis=1):
    sj = lax.broadcasted_iota(jnp.int32, p_slc.shape, axis)
    causal = (sj <= blk_t) & (sj < n_blocks)
    forced = causal & ((sj == 0) | (sj >= blk_t - 1))
    score = jnp.where(forced, p_slc + FORCE_BONUS, p_slc)
    return jnp.where(causal, score, MASKED)


def _nsa_prompt_kernel(q_ref, gate_ref, kck_ref, kcv_ref, kts_ref, vsa_ref, ktw_ref, vwa_ref, ovl_ref, exp_ref,
                       gsel_ref, gob_ref, out_ref, s_scr, mx_scr, acc_scr, *, n_cmp, n_heads):
    blk = pl.program_id(1)
    start = blk * Q_BLOCK
    q_per_kv = n_heads // N_KV
    q = q_ref[...] * (LOG2_E * HEAD_DIM ** -0.5)
    ncp = kcv_ref.shape[1]
    n_blocks = ovl_ref.shape[0]
    tpos = start + lax.broadcasted_iota(jnp.int32, (Q_BLOCK, 1), 0)
    cn = lax.broadcasted_iota(jnp.int32, (1, ncp), 1)
    cmp_mask = ((cn * CMP_STRIDE + CMP_LEN - 1 <= tpos) & (cn < n_cmp)).astype(F32)
    cmp_mask = jnp.concatenate([cmp_mask] * (N_KV * q_per_kv), axis=0) > 0.5
    grp_lanes = [slice(g * HEAD_DIM, (g + 1) * HEAD_DIM) for g in range(N_KV)]
    qgs = [_stack_heads(q, g, q_per_kv) for g in range(N_KV)]
    qbs = [x.astype(BF16) for x in qgs]

    def normalised(acc, g):
        return (acc / pltpu.roll(acc, HEAD_DIM, 1))[:, grp_lanes[g]]

    o_cmp, p_slc = [], []
    ovl = ovl_ref[...]
    grp_rows = [slice(g * q_per_kv * Q_BLOCK, (g + 1) * q_per_kv * Q_BLOCK) for g in range(N_KV)]
    s = jnp.concatenate([_dot(qbs[g], kck_ref[0, grp_lanes[g], :]) for g in range(N_KV)], axis=0)
    s = jnp.where(cmp_mask, s, MASKED)
    e = jnp.where(cmp_mask, jnp.exp2(s - jnp.max(s, axis=-1, keepdims=True)), 0.0)
    p_all = e / jnp.maximum(jnp.sum(e, axis=-1, keepdims=True), 1e-30)
    for g in range(N_KV):
        p = p_all[grp_rows[g]]
        o_cmp.append(_dot(p.astype(BF16), kcv_ref[0, :, grp_lanes[g]]))
        p_sum = p[:Q_BLOCK]
        for h in range(1, q_per_kv):
            p_sum = p_sum + p[h * Q_BLOCK:(h + 1) * Q_BLOCK]
        p_hi = p_sum.astype(BF16)
        p_lo = (p_sum - p_hi.astype(F32)).astype(BF16)
        p_slc.append(_dot_nt(ovl, p_hi) + _dot_nt(ovl, p_lo))
    blk_t = (start + lax.broadcasted_iota(jnp.int32, (1, Q_BLOCK), 1)) // SLC_BLOCK
    score = _block_scores(jnp.concatenate(p_slc, axis=1), jnp.concatenate([blk_t] * N_KV, axis=1), n_blocks, axis=0)
    sel = _select_mask_t(score, min(N_SELECT, n_blocks)) * SEL_BONUS
    sel = jnp.concatenate([sel, jnp.zeros((LANES - n_blocks, N_KV * Q_BLOCK), F32)], axis=0)
    q_aug = []
    for g in range(N_KV):
        sel_g = sel[:, g * Q_BLOCK:(g + 1) * Q_BLOCK].T[:, :n_blocks]
        q_aug.append(jnp.concatenate([qgs[g], jnp.concatenate([sel_g] * q_per_kv, axis=0)], axis=1).astype(BF16))

    ws = pl.multiple_of(jnp.maximum(start - WINDOW, 0), LANES)
    dpos = tpos - (ws + lax.broadcasted_iota(jnp.int32, (1, WIN_KEYS), 1))
    win_bias = jnp.where((dpos >= 0) & (dpos < WINDOW), 0.0, MASKED)
    win_bias = jnp.concatenate([win_bias] * (N_KV * q_per_kv), axis=0)
    s = jnp.concatenate([_dot(qbs[g], ktw_ref[0, grp_lanes[g], pl.ds(ws, WIN_KEYS)]) for g in range(N_KV)], axis=0)
    s = s + win_bias
    p_all = jnp.exp2(s - jnp.max(s, axis=-1, keepdims=True)).astype(BF16)
    o_win = [normalised(_dot(p_all[grp_rows[g]], vwa_ref[pl.ds(ws, WIN_KEYS), g * K_LANES:(g + 1) * K_LANES]), g)
             for g in range(N_KV)]

    last = (start + Q_BLOCK - 1) // SLC_TILE
    lane_tiles = SLC_TILE // LANES
    mx_scr[...] = jnp.full(mx_scr.shape, MASKED, F32)
    acc_scr[...] = jnp.zeros(acc_scr.shape, F32)

    def score_tile(kt, masked):
        off = pl.multiple_of(kt * SLC_TILE, SLC_TILE)
        for g in range(N_KV):
            keys = jnp.concatenate([kts_ref[0, grp_lanes[g], pl.ds(off, SLC_TILE)],
                                    exp_ref[:, pl.ds(off, SLC_TILE)]], axis=0)
            s = _dot(q_aug[g], keys)
            if masked:
                row = lax.broadcasted_iota(jnp.int32, (q_per_kv * Q_BLOCK, 1), 0) % Q_BLOCK
                s = jnp.where(off + lax.broadcasted_iota(jnp.int32, (1, SLC_TILE), 1) <= start + row, s, MASKED)
            s_scr[g, :, pl.ds(off, SLC_TILE)] = s
            m = s[:, :LANES]
            for i in range(1, lane_tiles):
                m = jnp.maximum(m, s[:, i * LANES:(i + 1) * LANES])
            mx_scr[g] = jnp.maximum(mx_scr[g], m)

    def pair_loop(n, tile):
        def trip(i, carry):
            for k in range(TILES_PER_TRIP):
                tile(TILES_PER_TRIP * i + k)
            return carry

        lax.fori_loop(0, n // TILES_PER_TRIP, trip, 0)
        done = n // TILES_PER_TRIP * TILES_PER_TRIP
        size = TILES_PER_TRIP // 2
        while size:
            @pl.when((n - done) & size != 0)
            def _(done=done, size=size):
                for k in range(size):
                    tile(done + k)

            done = done + ((n - done) & size)
            size //= 2

    pair_loop(last, lambda kt: score_tile(kt, False))
    score_tile(last, True)
    for g in range(N_KV):
        mx_scr[g] = jnp.broadcast_to(jnp.max(mx_scr[g], axis=-1, keepdims=True), mx_scr.shape[1:])

    def value_tile(kt):
        off = pl.multiple_of(kt * SLC_TILE, SLC_TILE)
        for g in range(N_KV):
            p = jnp.exp2(s_scr[g, :, pl.ds(off, SLC_TILE)] - jnp.concatenate([mx_scr[g]] * lane_tiles, axis=1))
            acc_scr[g] += _dot(p.astype(BF16), vsa_ref[pl.ds(off, SLC_TILE), g * K_LANES:(g + 1) * K_LANES])

    pair_loop(last + 1, value_tile)
    o_slc = [normalised(acc_scr[g], g) for g in range(N_KV)]

    gates = gate_ref[...]
    g_hi = gates.astype(BF16)
    g_lo = (gates - g_hi.astype(F32)).astype(BF16)
    out = None
    for j, branch in enumerate((o_cmp, o_slc, o_win)):
        spread = _dot(g_hi, gsel_ref[j]) + _dot(g_lo, gsel_ref[j])
        o = jnp.concatenate([branch[g][h * Q_BLOCK:(h + 1) * Q_BLOCK] for g in range(N_KV) for h in range(q_per_kv)],
                            axis=1)
        out = spread * o if out is None else out + spread * o
    out_ref[...] = _rms(out, gob_ref[...]).astype(BF16)


def _nsa_prompt(q, gates, kck_t, kcv, kts, vsa, ktw, vwa, w, b, t):
    n, b_width = q.shape
    n_heads = b_width // HEAD_DIM
    assert t % SLC_TILE == 0 and t >= WIN_KEYS
    nqb = t // Q_BLOCK
    ncp = kcv.shape[1]
    n_cmp = ncp - 1
    n_blocks = (n_cmp + 1) * CMP_STRIDE // SLC_BLOCK
    assert n_blocks % SUBLANES == 0 and n_blocks <= LANES
    ovl = _overlap_matrix(ncp, n_blocks, n_blocks).T.astype(BF16)
    key_blk = jnp.arange(t, dtype=jnp.int32)[None, :] // SLC_BLOCK
    expand = (key_blk == jnp.arange(n_blocks, dtype=jnp.int32)[:, None]).astype(BF16)
    col = jnp.arange(LANES, dtype=jnp.int32)[None, :, None]
    head = jnp.arange(b_width, dtype=jnp.int32)[None, None, :] // HEAD_DIM
    gate_sel = (col == head * 3 + jnp.arange(3, dtype=jnp.int32)[:, None, None]).astype(BF16)
    tok = lambda i, j: (i * nqb + j, 0)
    seq3 = lambda i, j: (i, 0, 0)
    seq2 = lambda i, j: (i, 0)
    rows = (n_heads // N_KV) * Q_BLOCK
    return pl.pallas_call(
        functools.partial(_nsa_prompt_kernel, n_cmp=n_cmp, n_heads=n_heads),
        grid=(b, nqb),
        in_specs=[pl.BlockSpec((Q_BLOCK, b_width), tok), pl.BlockSpec((Q_BLOCK, LANES), tok),
                  pl.BlockSpec((1, K_LANES, ncp), seq3), pl.BlockSpec((1, ncp, K_LANES), seq3),
                  pl.BlockSpec((1, K_LANES, t), seq3), pl.BlockSpec((t, N_KV * K_LANES), seq2),
                  pl.BlockSpec((1, K_LANES, t), seq3), pl.BlockSpec((t, N_KV * K_LANES), seq2),
                  _full(ovl.shape), _full(expand.shape), _full(gate_sel.shape), _full(w["g_out_b"].shape)],
        out_specs=pl.BlockSpec((Q_BLOCK, b_width), tok),
        out_shape=jax.ShapeDtypeStruct((n, b_width), BF16),
        scratch_shapes=[pltpu.VMEM((N_KV, rows, t), F32), pltpu.VMEM((N_KV, rows, LANES), F32),
                        pltpu.VMEM((N_KV, rows, K_LANES), F32)],
        compiler_params=_params("parallel", "arbitrary"), name="nsa_prompt",
    )(q, gates, kck_t, kcv, kts, vsa, ktw, vwa, ovl, expand, gate_sel, w["g_out_b"])


def _overlap_matrix(rows, cols, n_blocks):
    ci = jnp.arange(rows, dtype=jnp.int32)[:, None]
    sj = jnp.arange(cols, dtype=jnp.int32)[None, :]
    hit = (ci * CMP_STRIDE < (sj + 1) * SLC_BLOCK) & (ci * CMP_STRIDE + CMP_LEN > sj * SLC_BLOCK) & (sj < n_blocks)
    return hit.astype(F32)


def _compress_sample_kernel(pt_ref, *refs, pages_per_step, n_chunks, n_steps):
    del pt_ref
    pages = refs[:pages_per_step]
    new_ref, w1k_ref, w1v_ref, b1_ref, w2_ref, kc_ref, fsk_scr, fsv_scr = refs[pages_per_step:pages_per_step + 8]
    chunk_scr = refs[pages_per_step + 8:]
    chunk_pages = pages_per_step // n_chunks
    j = pl.program_id(1)
    blocks_per_page = PAGE_SIZE // CMP_STRIDE
    chunk_blocks = chunk_pages * blocks_per_page

    def scratch(c):
        return chunk_scr[2 * (c % 2)], chunk_scr[2 * (c % 2) + 1]

    def transpose_chunk(c):
        xk_scr, xv_scr = scratch(c)
        for i in range(chunk_pages):
            page = pages[c * chunk_pages + i]
            xk_scr[pl.ds(i * PAGE_SIZE, PAGE_SIZE), :] = page[0, :K_LANES, :].T
            xv_scr[pl.ds(i * PAGE_SIZE, PAGE_SIZE), :] = page[0, K_LANES:, :].T

    def project_chunk(c):
        xk_scr, xv_scr = scratch(c)
        fs_k, fs_v = _compress_partial(lambda s: xk_scr[pl.ds(s, chunk_blocks, stride=CMP_STRIDE), :],
                                       lambda s: xv_scr[pl.ds(s, chunk_blocks, stride=CMP_STRIDE), :],
                                       w1k_ref, w1v_ref)
        off = pl.multiple_of((j * n_chunks + c) * chunk_blocks, chunk_blocks)
        fsk_scr[pl.ds(off, chunk_blocks), :] = fs_k
        fsv_scr[pl.ds(off, chunk_blocks), :] = fs_v

    transpose_chunk(0)
    for c in range(n_chunks):
        if c + 1 < n_chunks:
            transpose_chunk(c + 1)
        project_chunk(c)

    @pl.when(j == n_steps - 1)
    def _():
        past_blocks = n_steps * pages_per_step * blocks_per_page
        tail = fsk_scr.shape[0] - past_blocks
        new = new_ref[0]
        is_first = lax.broadcasted_iota(jnp.int32, (tail, 1), 0) == 0
        nk = _dot(new[:, :K_LANES].astype(BF16), w1k_ref[0, :K_LANES, :])
        nv = _dot(new[:, K_LANES:].astype(BF16), w1v_ref[0, :K_LANES, :])
        fsk_scr[pl.ds(past_blocks, tail), :] = jnp.where(is_first, nk, 0.0)
        fsv_scr[pl.ds(past_blocks, tail), :] = jnp.where(is_first, nv, 0.0)
        kc_ref[0] = _compress_finish(fsk_scr[...], fsv_scr[...], b1_ref, w2_ref)


def _compress_sample(cache_cmp_t, page_table, new_rows, w):
    db, n_pages = page_table.shape
    pages_per_step, n_chunks = 128, 4
    assert n_pages % pages_per_step == 0
    n_steps = n_pages // pages_per_step
    blocks_per_page = PAGE_SIZE // CMP_STRIDE
    past_blocks = n_pages * blocks_per_page
    nbp = past_blocks + SUBLANES
    weights = [w["w1k"], w["w1v"], w["b_c1"], w["w2"]]
    hid2 = w["w1k"].shape[2]
    page_spec = lambda i: pl.BlockSpec((1, KV_LANES, PAGE_SIZE),
                                       lambda b, j, pt, i=i: (pt[b, j * pages_per_step + i], 0, 0))
    chunk_rows = pages_per_step // n_chunks * PAGE_SIZE
    grid_spec = pltpu.PrefetchScalarGridSpec(
        num_scalar_prefetch=1, grid=(db, n_steps),
        in_specs=[page_spec(i) for i in range(pages_per_step)]
        + [pl.BlockSpec((1, 1, KV_LANES), lambda b, j, pt: (b, 0, 0))]
        + [pl.BlockSpec(a.shape, lambda b, j, pt, nd=a.ndim: (0,) * nd) for a in weights],
        out_specs=pl.BlockSpec((1, nbp, KV_LANES), lambda b, j, pt: (b, 0, 0)),
        scratch_shapes=[pltpu.VMEM((nbp, hid2), F32), pltpu.VMEM((nbp, hid2), F32)]
        + [pltpu.VMEM((chunk_rows, K_LANES), F32)] * 4)
    return pl.pallas_call(
        functools.partial(_compress_sample_kernel, pages_per_step=pages_per_step, n_chunks=n_chunks,
                          n_steps=n_steps),
        grid_spec=grid_spec, out_shape=jax.ShapeDtypeStruct((db, nbp, KV_LANES), F32),
        compiler_params=_params("parallel", "arbitrary"), name="compress_sample",
    )(page_table, *([cache_cmp_t] * pages_per_step), new_rows[:, None, :], *weights)


def _select_sample_kernel(q_ref, kc_ref, ocmp_ref, psum_ref, *, pos, n_cmp, n_heads):
    q_per_kv = n_heads // N_KV
    q = q_ref[0] * (LOG2_E * HEAD_DIM ** -0.5)
    kc = kc_ref[0]
    ncp = kc.shape[0]
    cn = lax.broadcasted_iota(jnp.int32, (1, ncp), 1)
    mask = (cn * CMP_STRIDE + CMP_LEN - 1 <= pos) & (cn < n_cmp)
    o_rows, p_rows = [], []
    for g in range(N_KV):
        qg = _stack_heads(q, g, q_per_kv).astype(BF16)
        s = jnp.where(mask, _dot_nt(qg, kc[:, g * HEAD_DIM:(g + 1) * HEAD_DIM].astype(BF16)), MASKED)
        e = jnp.where(mask, jnp.exp2(s - jnp.max(s, axis=-1, keepdims=True)), 0.0)
        p = e / jnp.maximum(jnp.sum(e, axis=-1, keepdims=True), 1e-30)
        o_rows.append(_dot(p.astype(BF16), kc[:, K_LANES + g * HEAD_DIM:K_LANES + (g + 1) * HEAD_DIM].astype(BF16)))
        p_rows.append(jnp.sum(p, axis=0, keepdims=True))
    ocmp_ref[0] = jnp.concatenate(o_rows, axis=0)
    psum_ref[0] = jnp.concatenate(p_rows, axis=0)


def _pick_sample_kernel(psum_ref, ovl_ref, idx_ref, *, pos, n_blocks):
    p = psum_ref[...]
    p_hi = p.astype(BF16)
    p_lo = (p - p_hi.astype(F32)).astype(BF16)
    ovl = ovl_ref[...]
    score = _block_scores(_dot(p_hi, ovl) + _dot(p_lo, ovl), pos // SLC_BLOCK, n_blocks)
    _, picks = _select_blocks(score, min(N_SELECT, n_blocks))
    lane = lax.broadcasted_iota(jnp.int32, idx_ref.shape, 1)
    out = jnp.full(idx_ref.shape, -1.0, F32)
    for i, (idx, ok) in enumerate(picks):
        out = jnp.where((lane == i) & ok, idx, out)
    idx_ref[...] = out.astype(jnp.int32)


def _select_sample(q, kc, pos, n_cmp, n_blocks):
    db, b_width = q.shape
    n_heads = b_width // HEAD_DIM
    ncp = kc.shape[1]
    o_cmp, p_sum = pl.pallas_call(
        functools.partial(_select_sample_kernel, pos=pos, n_cmp=n_cmp, n_heads=n_heads),
        grid=(db,),
        in_specs=[pl.BlockSpec((1, 1, b_width), lambda i: (i, 0, 0)),
                  pl.BlockSpec((1, ncp, KV_LANES), lambda i: (i, 0, 0))],
        out_specs=[pl.BlockSpec((1, n_heads, HEAD_DIM), lambda i: (i, 0, 0)),
                   pl.BlockSpec((1, N_KV, ncp), lambda i: (i, 0, 0))],
        out_shape=[jax.ShapeDtypeStruct((db, n_heads, HEAD_DIM), F32),
                   jax.ShapeDtypeStruct((db, N_KV, ncp), F32)],
        compiler_params=_params("parallel"), name="select_sample",
    )(q[:, None, :], kc)
    nsp = -(-n_blocks // LANES) * LANES
    ovl = _overlap_matrix(ncp, nsp, n_blocks).astype(BF16)
    idx = pl.pallas_call(
        functools.partial(_pick_sample_kernel, pos=pos, n_blocks=n_blocks),
        grid=(1,),
        in_specs=[_full((db * N_KV, ncp)), _full(ovl.shape)],
        out_specs=_full((db * N_KV, LANES)),
        out_shape=jax.ShapeDtypeStruct((db * N_KV, LANES), jnp.int32),
        compiler_params=_params("arbitrary"), name="pick_sample",
    )(p_sum.reshape(db * N_KV, ncp), ovl)
    return o_cmp, idx.reshape(db, N_KV, LANES)


def _attend_sample_kernel(page_ref, *refs, n_sel, past_blocks, n_heads):
    del page_ref
    n_slots = N_KV * n_sel
    pages = refs[:n_slots]
    (q_ref, gate_ref, ocmp_ref, idx_ref, newslc_ref, win_ref, newwin_ref, exp_ref, gob_ref,
     out_ref, winout_ref) = refs[n_slots:]
    q_per_kv = n_heads // N_KV
    q = q_ref[0] * (HEAD_DIM ** -0.5)
    lane = lax.broadcasted_iota(jnp.int32, (1, K_LANES), 1)
    wb = win_ref.shape[2]
    is_last = lax.broadcasted_iota(jnp.int32, (1, wb), 1) == wb - 1
    win = jnp.where(is_last, newwin_ref[0], pltpu.roll(win_ref[0], wb - 1, 1))
    winout_ref[0] = win
    win_k = win[:K_LANES].astype(BF16)
    win_v = win[K_LANES:].astype(BF16)
    new_slc = newslc_ref[0]
    idx = idx_ref[0].astype(F32)
    n_keys = n_sel * PAGE_SIZE
    key_half = (lax.broadcasted_iota(jnp.int32, (1, n_keys), 1) % PAGE_SIZE) // SLC_BLOCK
    o_slc, o_win = [], []
    for g in range(N_KV):
        qpad = jnp.concatenate(
            [jnp.where(lane // HEAD_DIM == g,
                       jnp.concatenate([q[:, (g * q_per_kv + h) * HEAD_DIM:(g * q_per_kv + h + 1) * HEAD_DIM]] * N_KV,
                                       axis=1), 0.0)
             for h in range(q_per_kv)], axis=0)
        qpb = qpad.astype(BF16)
        kt = jnp.concatenate([pages[g * n_sel + i][0, :K_LANES, :] for i in range(n_sel)], axis=1).astype(BF16)
        vt = jnp.concatenate([pages[g * n_sel + i][0, K_LANES:, :] for i in range(n_sel)], axis=1).astype(BF16)
        s = _dot(qpb, kt)
        idg = idx[g:g + 1, :]
        idk = jnp.dot(idg, exp_ref[...], precision=lax.Precision.HIGHEST, preferred_element_type=F32)
        parity = idk - 2.0 * jnp.floor(idk * 0.5)
        key_ok = (idk >= 0.0) & (idk < past_blocks) & (parity == key_half.astype(F32))
        s = jnp.where(key_ok, s, MASKED)
        has_new = jnp.max(jnp.where(idg == past_blocks, 1.0, 0.0), axis=-1, keepdims=True) > 0.5
        s_new = jnp.sum(qpad * new_slc[:, :K_LANES], axis=-1, keepdims=True)
        s_new = jnp.where(has_new, s_new, MASKED)
        m = jnp.maximum(jnp.max(s, axis=-1, keepdims=True), s_new)
        e = jnp.where(key_ok, jnp.exp(s - m), 0.0)
        e_new = jnp.where(has_new, jnp.exp(s_new - m), 0.0)
        den = jnp.maximum(jnp.sum(e, axis=-1, keepdims=True) + e_new, 1e-30)
        o = (_dot_nt(e.astype(BF16), vt) + e_new * new_slc[:, K_LANES:]) / den
        o_slc.append(o[:, g * HEAD_DIM:(g + 1) * HEAD_DIM])
        s = _dot(qpb, win_k)
        e = jnp.exp(s - jnp.max(s, axis=-1, keepdims=True))
        o = _dot_nt(e.astype(BF16), win_v) / jnp.sum(e, axis=-1, keepdims=True)
        o_win.append(o[:, g * HEAD_DIM:(g + 1) * HEAD_DIM])
    gates = gate_ref[0]
    o = (gates[:, 0:1] * ocmp_ref[0] + gates[:, 1:2] * jnp.concatenate(o_slc, axis=0)
         + gates[:, 2:3] * jnp.concatenate(o_win, axis=0))
    ms = jnp.sum(jnp.sum(o * o, axis=-1, keepdims=True), axis=0, keepdims=True) / (n_heads * HEAD_DIM)
    out_ref[0] = o * lax.rsqrt(ms + NORM_EPS) * gob_ref[...]


def _attend_sample(q, gates, o_cmp, idx, cache_slc_t, page_table, new_slc, win_t, new_win, w, past_blocks):
    db, b_width = q.shape
    n_heads = b_width // HEAD_DIM
    n_sel = min(N_SELECT, past_blocks + 1)
    sub_per_page = PAGE_SIZE // SLC_BLOCK
    jp = jnp.clip(idx[:, :, :n_sel], 0, past_blocks - 1)
    page = jnp.take_along_axis(page_table, (jp // sub_per_page).reshape(db, -1), axis=1).astype(jnp.int32)
    n_keys = n_sel * PAGE_SIZE
    expand = (jnp.arange(n_keys, dtype=jnp.int32)[None, :] // PAGE_SIZE
              == jnp.arange(LANES, dtype=jnp.int32)[:, None]).astype(F32)
    gob = w["g_out_b"].reshape(n_heads, HEAD_DIM)
    wb = win_t.shape[2]
    per_seq = lambda shape: pl.BlockSpec((1,) + shape, lambda b, pg: (b, 0, 0))
    page_spec = lambda i: pl.BlockSpec((1, KV_LANES, PAGE_SIZE), lambda b, pg, i=i: (pg[b, i], 0, 0))
    grid_spec = pltpu.PrefetchScalarGridSpec(
        num_scalar_prefetch=1, grid=(db,),
        in_specs=[page_spec(i) for i in range(N_KV * n_sel)]
        + [per_seq((1, b_width)), per_seq((n_heads, 3)), per_seq((n_heads, HEAD_DIM)), per_seq((N_KV, LANES)),
           per_seq((1, KV_LANES)), per_seq((KV_LANES, wb)), per_seq((KV_LANES, 1)),
           pl.BlockSpec(expand.shape, lambda b, pg: (0, 0)), pl.BlockSpec(gob.shape, lambda b, pg: (0, 0))],
        out_specs=[per_seq((n_heads, HEAD_DIM)), per_seq((KV_LANES, wb))])
    return pl.pallas_call(
        functools.partial(_attend_sample_kernel, n_sel=n_sel, past_blocks=past_blocks, n_heads=n_heads),
        grid_spec=grid_spec,
        out_shape=[jax.ShapeDtypeStruct((db, n_heads, HEAD_DIM), F32),
                   jax.ShapeDtypeStruct((db, KV_LANES, wb), F32)],
        compiler_params=_params("parallel"), name="attend_sample",
    )(page, *([cache_slc_t] * (N_KV * n_sel)), q[:, None, :], gates[:, :n_heads * 3].reshape(db, n_heads, 3), o_cmp,
      idx, new_slc[:, None, :], win_t, new_win[:, :, None], expand, gob)


def _ffn_kernel(x_ref, ma_ref, mb_ref, woa_ref, wob_ref, gffn_ref, wgate_ref, wup_ref, wdown_ref, gfin_ref, y_ref,
                acc_scr, xn_scr):
    c = pl.program_id(1)

    @pl.when(c == 0)
    def _():
        x = x_ref[...] + _dot(ma_ref[...], woa_ref[...]) + _dot(mb_ref[...], wob_ref[...])
        acc_scr[...] = x
        xn_scr[...] = _rms(x, gffn_ref[...]).astype(BF16)

    xn = xn_scr[...]
    hid = jax.nn.silu(_dot(xn, wgate_ref[...])) * _dot(xn, wup_ref[...])
    acc_scr[...] += _dot(hid.astype(BF16), wdown_ref[...])

    @pl.when(c == pl.num_programs(1) - 1)
    def _():
        y_ref[...] = _rms(acc_scr[...], gfin_ref[...])


def _output_ffn(x2, mix_a, mix_b, w, tm):
    n, d = x2.shape
    d_ff = w["w_down"].shape[0]
    ff_chunks = 2
    step = d_ff // ff_chunks
    assert n % tm == 0 and d_ff % ff_chunks == 0 and step % LANES == 0
    row = lambda i, c: (i, 0)
    fixed = lambda a: pl.BlockSpec(a.shape, lambda i, c: (0, 0))
    return pl.pallas_call(
        _ffn_kernel,
        grid=(n // tm, ff_chunks),
        in_specs=[pl.BlockSpec((tm, d), row), pl.BlockSpec((tm, mix_a.shape[1]), row),
                  pl.BlockSpec((tm, mix_b.shape[1]), row), fixed(w["wo_a"]), fixed(w["wo_b"]), fixed(w["g_ffn"]),
                  pl.BlockSpec((d, step), lambda i, c: (0, c)), pl.BlockSpec((d, step), lambda i, c: (0, c)),
                  pl.BlockSpec((step, d), lambda i, c: (c, 0)), fixed(w["g_final"])],
        out_specs=pl.BlockSpec((tm, d), row),
        out_shape=jax.ShapeDtypeStruct((n, d), F32),
        scratch_shapes=[pltpu.VMEM((tm, d), F32), pltpu.VMEM((tm, d), BF16)],
        compiler_params=_params("parallel", "arbitrary"), name="output_ffn",
    )(x2, mix_a, mix_b, w["wo_a"], w["wo_b"], w["g_ffn"], w["w_gate"], w["w_up"], w["w_down"], w["g_final"])


def _prepare_weights(l, g_attn, w_in, g_sgu, w_s, b_s, w_c1, b_c1, w_c2, g_out_a, g_out_b, w_out,
                     g_ffn, w_gate_up, w_down, g_final):
    a_width = g_sgu.shape[1]
    b_width = g_out_b.shape[1]
    n_heads = b_width // HEAD_DIM
    d_ff = w_down.shape[1]
    hid = b_c1.shape[2]
    o1 = 2 * a_width
    o2 = o1 + b_width
    o3 = o2 + 3 * KV_LANES
    wi = w_in[l]
    wg = jnp.pad(wi[:, o3:], ((0, 0), (0, LANES - 3 * n_heads)))
    row = lambda a: a.reshape(1, -1)
    w1 = w_c1[l].reshape(2, 2, CMP_STRIDE, HEAD_DIM, hid)
    w1 = jnp.transpose(w1, (0, 2, 3, 1, 4)).reshape(2, CMP_STRIDE, HEAD_DIM, 2 * hid)
    eye = jnp.eye(N_KV, dtype=F32)
    w1 = jnp.einsum("gh,csdn->csgdhn", eye, w1).reshape(2, CMP_STRIDE // CMP_STACK, CMP_STACK * K_LANES, N_KV * 2 * hid)
    w2 = jnp.einsum("gh,cne->cgnhe", eye, w_c2[l]).reshape(2, N_KV * hid, K_LANES)
    return {
        "g_attn": row(g_attn[l]), "wuv": wi[:, :o1].astype(BF16), "wq": wi[:, o1:o2].astype(BF16),
        "wkv": wi[:, o2:o3].astype(BF16), "wg": wg.astype(BF16),
        "g_sgu": row(g_sgu[l]), "g_out_a": row(g_out_a[l]), "g_out_b": row(g_out_b[l]),
        "w_s": w_s[l], "bs_full": jnp.repeat(b_s[l].T, LANES, axis=1),
        "ws0": row(jnp.repeat(w_s[l][:, 0, 0], LANES)), "bs0": row(jnp.repeat(b_s[l][:, 0], LANES)),
        "w1k": w1[0].astype(BF16), "w1v": w1[1].astype(BF16), "b_c1": b_c1[l], "w2": w2.astype(BF16),
        "wo_a": w_out[l][:a_width].astype(BF16), "wo_b": w_out[l][a_width:].astype(BF16),
        "g_ffn": row(g_ffn[l]), "w_gate": w_gate_up[l][:, :d_ff].astype(BF16),
        "w_up": w_gate_up[l][:, d_ff:].astype(BF16), "w_down": w_down[l].astype(BF16), "g_final": row(g_final),
    }


def kernel(x_prompt, x_sample, cache_cmp_kv, cache_slc_kv, state_win_kv, page_table, g_attn, w_in, g_sgu, w_s, b_s,
           w_c1, b_c1, w_c2, g_out_a, g_out_b, w_out, g_ffn, w_gate_up, w_down, g_final):
    depth = w_in.shape[0]
    b, t, d = x_prompt.shape
    db, t_s, _ = x_sample.shape
    assert depth == 1 and t_s == 1
    n_pages = page_table.shape[1]
    past = n_pages * PAGE_SIZE
    wb = state_win_kv.shape[2]
    assert wb == WINDOW and past % SLC_BLOCK == 0
    l = 0
    w = _prepare_weights(l, g_attn, w_in, g_sgu, w_s, b_s, w_c1, b_c1, w_c2, g_out_a, g_out_b, w_out,
                         g_ffn, w_gate_up, w_down, g_final)
    xp = x_prompt.reshape(b * t, d)
    mix_a, q, kvc, gates, kvc_t, kvs_t, kvw_t, kts, vs, ktw, vw = _in_projection(
        xp, jnp.arange(t, dtype=jnp.int32), w, prompt_shape=(b, t))
    kck_t, kcv = _compress_prompt(kvc.reshape(b, t, KV_LANES), w)
    mix_b = _nsa_prompt(q, gates, kck_t, kcv, kts, vs, ktw, vw, w, b, t)
    y_prompt = _output_ffn(xp, mix_a, mix_b, w, tm=512).reshape(b, t, d)

    xs = x_sample.reshape(db, d)
    pos_s = past + jnp.zeros((db,), jnp.int32)
    mix_a_s, q_s, kvc_s, kvs_s, kvw_s, gates_s, v_rows = _in_projection(xs, pos_s, w)
    kc_s = _compress_sample(_feature_major(cache_cmp_kv[l]), page_table, kvc_s, w)
    lp = -(-(past + t_s) // SLC_BLOCK) * SLC_BLOCK
    n_cmp_s = lp // CMP_STRIDE - 1
    n_blocks_s = (n_cmp_s + 1) * CMP_STRIDE // SLC_BLOCK
    o_cmp_s, idx_s = _select_sample(q_s, kc_s, past, n_cmp_s, n_blocks_s)
    mix_b_s, win_new_t = _attend_sample(q_s, gates_s, o_cmp_s, idx_s, _feature_major(cache_slc_kv[l]), page_table,
                                        kvs_s, _feature_major(state_win_kv[l]), kvw_s, w, past // SLC_BLOCK)
    y_sample = _output_ffn(xs, mix_a_s, mix_b_s.reshape(db, -1).astype(BF16), w, tm=db).reshape(db, t_s, d)

    kv_shape = (2, N_KV, HEAD_DIM)
    return (y_prompt, y_sample,
            _row_major(kvc_t)[None], _row_major(kvs_t)[None], _row_major(kvw_t[:, :, t - min(WINDOW, t):])[None],
            kvc_s.reshape(1, db, t_s, *kv_shape), kvs_s.reshape(1, db, t_s, *kv_shape),
            _row_major(win_new_t)[None], v_rows.reshape(1, db, t_s, -1))


def _feature_major(kv):
    n, rows = kv.shape[:2]
    return jnp.transpose(kv, (0, 2, 3, 4, 1)).reshape(n, KV_LANES, rows)


def _row_major(kv_t):
    n, _, rows = kv_t.shape
    return jnp.transpose(kv_t.reshape(n, 2, N_KV, HEAD_DIM, rows), (0, 4, 1, 2, 3))
```

```python
import functools

import jax
import jax.numpy as jnp
from jax import lax
from jax.experimental import pallas as pl
from jax.experimental.pallas import tpu as pltpu

F32 = jnp.float32
BF16 = jnp.bfloat16

A_GROUPS = 4
CHUNK = 128
HEAD_DIM = 64
N_KV = 2
ROT_DIM = HEAD_DIM // 4
ROPE_THETA = 500000.0
CMP_LEN = 32
CMP_STRIDE = 16
SLC_BLOCK = 64
N_SELECT = 16
WINDOW = 512
Q_BLOCK = 128
FORCE_BONUS = 1000.0
PAGE_SIZE = 128
NORM_EPS = 1e-6
MASKED = -1e30
LOG2_E = 1.4426950408889634
SEL_BONUS = 16384.0

LANES = 128
SUBLANES = 8
VMEM_LIMIT_BYTES = 56 * 1024 * 1024

PROJ_ROWS = 512
FFN_ROWS = 512
FFN_CHUNKS = 2
CMP_PAGES_PER_STEP = 128
CMP_PAGE_CHUNKS = 4

KV_LANES = 2 * N_KV * HEAD_DIM
K_LANES = N_KV * HEAD_DIM
CMP_STACK = 2
SLC_TILE = 512
TILES_PER_TRIP = 4
WIN_KEYS = WINDOW + Q_BLOCK


def _rms(x, g):
    return x * lax.rsqrt(jnp.mean(x * x, axis=-1, keepdims=True) + NORM_EPS) * g


def _dot(a, b):
    return jnp.dot(a, b, preferred_element_type=F32)


def _dot_nt(a, b, precision=None):
    return lax.dot_general(a, b, (((1,), (1,)), ((), ())), precision=precision,
                           preferred_element_type=F32)


def _rope(z, rc, rs1, rs2):
    return z * rc + pltpu.roll(z, LANES - ROT_DIM // 2, 1) * rs1 + pltpu.roll(z, ROT_DIM // 2, 1) * rs2


def _project(x_ref, gattn_ref, wuv_ref, wq_ref, wkv_ref, wg_ref, rc_ref, rs1_ref, rs2_ref):
    xn = _rms(x_ref[...], gattn_ref[...]).astype(BF16)
    rc, rs1, rs2 = rc_ref[...], rs1_ref[...], rs2_ref[...]
    zuv = _dot(xn, wuv_ref[...])
    a_width = zuv.shape[1] // 2
    zq = _dot(xn, wq_ref[...])
    q = jnp.concatenate([_rope(zq[:, i * LANES:(i + 1) * LANES], rc, rs1, rs2)
                         for i in range(zq.shape[1] // LANES)], axis=1)
    zkv = _dot(xn, wkv_ref[...])
    branches = []
    for br in range(3):
        k = _rope(zkv[:, br * KV_LANES:br * KV_LANES + K_LANES], rc, rs1, rs2)
        v = zkv[:, br * KV_LANES + K_LANES:(br + 1) * KV_LANES]
        branches.append((k, v))
    gates = jax.nn.sigmoid(_dot(xn, wg_ref[...]))
    return zuv[:, :a_width], zuv[:, a_width:], q, branches, gates


def _gmlp_norm_v(v, gsgu):
    v = jax.nn.gelu(v)
    return jnp.concatenate([_rms(v[:, g * LANES:(g + 1) * LANES], gsgu[:, g * LANES:(g + 1) * LANES])
                            for g in range(A_GROUPS)], axis=1)


def _inproj_prompt_kernel(x_ref, gattn_ref, wuv_ref, wq_ref, wkv_ref, wg_ref, rc_ref, rs1_ref, rs2_ref,
                          ws_ref, bs_ref, gsgu_ref, goa_ref,
                          mixa_ref, q_ref, kvc_ref, gate_ref, kvct_ref, kvst_ref, kvwt_ref,
                          kts_ref, vs_ref, ktw_ref, vw_ref):
    u, v, q, branches, gates = _project(x_ref, gattn_ref, wuv_ref, wq_ref, wkv_ref, wg_ref,
                                        rc_ref, rs1_ref, rs2_ref)
    tm = u.shape[0]
    u = jax.nn.gelu(u)
    vg = _gmlp_norm_v(v, gsgu_ref[...]).astype(BF16)
    row = lax.broadcasted_iota(jnp.int32, (CHUNK, CHUNK), 0)
    col = lax.broadcasted_iota(jnp.int32, (CHUNK, CHUNK), 1)
    bias = bs_ref[...]
    parts = []
    for g in range(A_GROUPS):
        w = jnp.where(row >= col, ws_ref[g], 0.0).astype(BF16)
        s = jnp.concatenate(
            [_dot(w, vg[c * CHUNK:(c + 1) * CHUNK, g * LANES:(g + 1) * LANES]) for c in range(tm // CHUNK)],
            axis=0)
        s = s + jnp.concatenate([bias[:, g * LANES:(g + 1) * LANES]] * (tm // CHUNK), axis=0)
        parts.append(u[:, g * LANES:(g + 1) * LANES] * s)
    mixa_ref[...] = _rms(jnp.concatenate(parts, axis=1), goa_ref[...]).astype(BF16)
    q_ref[...] = (q * (LOG2_E * HEAD_DIM ** -0.5)).astype(BF16)
    gate_ref[...] = gates
    kvc_ref[...] = jnp.concatenate(branches[0], axis=1)
    kts = []
    for ref, (k, v_) in zip((kvct_ref, kvst_ref, kvwt_ref), branches):
        kt = k.T
        ref[0, :K_LANES, :] = kt
        ref[0, K_LANES:, :] = v_.T
        kts.append(kt)
    own = [lax.broadcasted_iota(jnp.int32, (1, K_LANES), 1) // HEAD_DIM == g for g in range(N_KV)]
    for kt_ref, va_ref, br in ((kts_ref, vs_ref, 1), (ktw_ref, vw_ref, 2)):
        kt_ref[0] = kts[br].astype(BF16)
        va_ref[...] = jnp.concatenate([jnp.where(m, branches[br][1], 1.0) for m in own], axis=1).astype(BF16)


def _inproj_sample_kernel(x_ref, gattn_ref, wuv_ref, wq_ref, wkv_ref, wg_ref, rc_ref, rs1_ref, rs2_ref,
                          ws0_ref, bs0_ref, gsgu_ref, goa_ref,
                          mixa_ref, q_ref, kvc_ref, kvs_ref, kvw_ref, gate_ref, vrow_ref):
    u, v, q, branches, gates = _project(x_ref, gattn_ref, wuv_ref, wq_ref, wkv_ref, wg_ref,
                                        rc_ref, rs1_ref, rs2_ref)
    vg = _gmlp_norm_v(v, gsgu_ref[...])
    o_a = jax.nn.gelu(u) * (vg * ws0_ref[...] + bs0_ref[...])
    mixa_ref[...] = _rms(o_a, goa_ref[...]).astype(BF16)
    vrow_ref[...] = vg
    q_ref[...] = q
    for ref, (k, v_) in zip((kvc_ref, kvs_ref, kvw_ref), branches):
        ref[...] = jnp.concatenate([k, v_], axis=1)
    gate_ref[...] = gates


def _full(shape):
    return pl.BlockSpec(shape, lambda *_: (0,) * len(shape))


def _params(*sem):
    return pltpu.CompilerParams(dimension_semantics=sem, vmem_limit_bytes=VMEM_LIMIT_BYTES)


def _in_projection(x2, pos, w, *, prompt_shape=None):
    n, d = x2.shape
    rc, rs1, rs2 = _rope_tables(pos)
    a_width = w["wuv"].shape[1] // 2
    b_width = w["wq"].shape[1]
    weights = [w["g_attn"], w["wuv"], w["wq"], w["wkv"], w["wg"]]
    wspecs = [_full(a.shape) for a in weights]
    tail = [w["g_sgu"], w["g_out_a"]]
    if prompt_shape is None:
        tm, grid = n, (1,)
        row = lambda i: (i, 0)
        rope_map = row
        gm = [w["ws0"], w["bs0"]]
    else:
        b, t = prompt_shape
        tm = PROJ_ROWS
        assert t % tm == 0 and tm % CHUNK == 0
        tpb = t // tm
        grid = (b * tpb,)
        row = lambda i: (i, 0)
        rope_map = lambda i: (i % tpb, 0)
        gm = [w["w_s"], w["bs_full"]]
    rspec = pl.BlockSpec((tm, LANES), rope_map)
    in_specs = ([pl.BlockSpec((tm, d), row)] + wspecs + [rspec] * 3
                + [_full(a.shape) for a in gm] + [_full(a.shape) for a in tail])
    if prompt_shape is None:
        kern = _inproj_sample_kernel
        outs = [((n, a_width), BF16), ((n, b_width), F32), ((n, KV_LANES), F32), ((n, KV_LANES), F32),
                ((n, KV_LANES), F32), ((n, LANES), F32), ((n, a_width), F32)]
        out_specs = [pl.BlockSpec((tm, s[1]), row) for s, _ in outs]
    else:
        kern = _inproj_prompt_kernel
        kt_map = lambda i: (i // tpb, 0, i % tpb)
        outs = [((n, a_width), BF16), ((n, b_width), BF16), ((n, KV_LANES), F32), ((n, LANES), F32)]
        out_specs = [pl.BlockSpec((tm, s[1]), row) for s, _ in outs]
        outs += [((b, KV_LANES, t), F32)] * 3
        out_specs += [pl.BlockSpec((1, KV_LANES, tm), kt_map)] * 3
        outs += [((b, K_LANES, t), BF16), ((n, N_KV * K_LANES), BF16)] * 2
        out_specs += [pl.BlockSpec((1, K_LANES, tm), kt_map), pl.BlockSpec((tm, N_KV * K_LANES), row)] * 2
    return pl.pallas_call(
        kern, grid=grid, in_specs=in_specs, out_specs=out_specs,
        out_shape=[jax.ShapeDtypeStruct(s, dt) for s, dt in outs],
        compiler_params=_params("parallel"), name="in_projection",
    )(x2, *weights, rc, rs1, rs2, *gm, *tail)


def _rope_tables(pos):
    half = ROT_DIM // 2
    inv = ROPE_THETA ** (-jnp.arange(half, dtype=F32) / half)
    ang = pos.astype(F32)[:, None] * inv[None, :]
    cos, sin = jnp.cos(ang), jnp.sin(ang)
    n = pos.shape[0]
    rest0 = jnp.zeros((n, HEAD_DIM - ROT_DIM), F32)
    zero = jnp.zeros((n, half), F32)
    rc = jnp.concatenate([cos, cos, rest0 + 1.0], axis=1)
    rs1 = jnp.concatenate([-sin, zero, rest0], axis=1)
    rs2 = jnp.concatenate([zero, sin, rest0], axis=1)
    return tuple(jnp.tile(a, (1, LANES // HEAD_DIM)) for a in (rc, rs1, rs2))


def _compress_partial(read_k, read_v, w1k_ref, w1v_ref):
    acc_k = acc_v = None
    stack = w1k_ref.shape[1] // K_LANES
    for i in range(CMP_STRIDE // stack):
        rows = range(i * stack, (i + 1) * stack)
        pk = _dot(jnp.concatenate([read_k(s).astype(BF16) for s in rows], axis=1), w1k_ref[i])
        pv = _dot(jnp.concatenate([read_v(s).astype(BF16) for s in rows], axis=1), w1v_ref[i])
        acc_k = pk if acc_k is None else acc_k + pk
        acc_v = pv if acc_v is None else acc_v + pv
    return acc_k, acc_v


def _compress_finish(fs_k, fs_v, b1_ref, w2_ref):
    hid = b1_ref.shape[1]
    outs = []
    for c, fs in enumerate((fs_k, fs_v)):
        hs = []
        for g in range(N_KV):
            first = fs[:, g * 2 * hid:g * 2 * hid + hid]
            second = fs[:, g * 2 * hid + hid:(g + 1) * 2 * hid]
            nxt = pltpu.roll(second, second.shape[0] - 1, 0)
            hs.append(jax.nn.silu(first + nxt + b1_ref[c:c + 1, :]))
        outs.append(_dot(jnp.concatenate(hs, axis=1).astype(BF16), w2_ref[c]))
    return jnp.concatenate(outs, axis=1)


def _compress_prompt_kernel(k_ref, v_ref, w1k_ref, w1v_ref, b1_ref, w2_ref, kck_ref, kcv_ref, kc_scr):
    nb = kcv_ref.shape[1]
    fs_k, fs_v = _compress_partial(lambda s: k_ref[0, pl.ds(s, nb, stride=CMP_STRIDE), :],
                                   lambda s: v_ref[0, pl.ds(s, nb, stride=CMP_STRIDE), :], w1k_ref, w1v_ref)
    kc_scr[...] = _compress_finish(fs_k, fs_v, b1_ref, w2_ref)
    kck_ref[0] = kc_scr[:, :K_LANES].T.astype(BF16)
    kcv_ref[0] = kc_scr[:, K_LANES:].astype(BF16)


def _compress_prompt(kvc, w):
    b, t, _ = kvc.shape
    nb = t // CMP_STRIDE
    weights = [w["w1k"], w["w1v"], w["b_c1"], w["w2"]]
    return pl.pallas_call(
        _compress_prompt_kernel, grid=(b,),
        in_specs=[pl.BlockSpec((1, t, K_LANES), lambda i: (i, 0, 0)), pl.BlockSpec((1, t, K_LANES), lambda i: (i, 0, 1))]
        + [_full(a.shape) for a in weights],
        out_specs=[pl.BlockSpec((1, K_LANES, nb), lambda i: (i, 0, 0)), pl.BlockSpec((1, nb, K_LANES), lambda i: (i, 0, 0))],
        out_shape=[jax.ShapeDtypeStruct((b, K_LANES, nb), BF16), jax.ShapeDtypeStruct((b, nb, K_LANES), BF16)],
        scratch_shapes=[pltpu.VMEM((nb, KV_LANES), F32)],
        compiler_params=_params("parallel"), name="compress_prompt",
    )(kvc, kvc, *weights)


def _stack_heads(q, g, q_per_kv):
    return jnp.concatenate([q[:, (g * q_per_kv + h) * HEAD_DIM:(g * q_per_kv + h + 1) * HEAD_DIM]
                            for h in range(q_per_kv)], axis=0)


def _select_blocks(score, n_sel):
    rows, n = score.shape
    lane = lax.broadcasted_iota(jnp.int32, (rows, n), 1).astype(F32)
    sel = jnp.zeros((rows, n), F32)
    picks = []
    x = score
    for _ in range(n_sel):
        m = jnp.max(x, axis=-1, keepdims=True)
        idx = jnp.min(jnp.where(x == m, lane, float(n)), axis=-1, keepdims=True)
        hit = lane == idx
        ok = m > 0.1 * MASKED
        sel = jnp.where(hit & ok, 1.0, sel)
        x = jnp.where(hit, -3e38, x)
        picks.append((idx, ok))
    return sel, picks


def _select_mask_t(xt, n_sel):
    n, cols = xt.shape
    tiles = [xt[t * SUBLANES:(t + 1) * SUBLANES] for t in range(n // SUBLANES)]
    row = lax.broadcasted_iota(jnp.int32, (SUBLANES, cols), 0)
    ahead = [jnp.zeros((SUBLANES, cols), F32) for _ in tiles]
    for i in range(n):
        xi = xt[i:i + 1, :]
        for t, x in enumerate(tiles):
            first, last = t * SUBLANES, (t + 1) * SUBLANES - 1
            if first > i:
                inc = jnp.where(xi >= x, 1.0, 0.0)
            elif last <= i:
                inc = jnp.where(xi > x, 1.0, 0.0)
            else:
                inc = jnp.where(row + first > i, jnp.where(xi >= x, 1.0, 0.0), jnp.where(xi > x, 1.0, 0.0))
            ahead[t] = ahead[t] + inc
    ahead = jnp.concatenate(ahead, axis=0)
    return jnp.where((ahead < n_sel) & (xt > 0.1 * MASKED), 1.0, 0.0)


def _block_scores(p_slc, blk_t, n_blocks, axis=1):
    sj = lax.broadcasted_iota(jnp.int32, p_slc.shape, axis)
    causal = (sj <= blk_t) & (sj < n_blocks)
    forced = causal & ((sj == 0) | (sj >= blk_t - 1))
    score = jnp.where(forced, p_slc + FORCE_BONUS, p_slc)
    return jnp.where(causal, score, MASKED)


def _nsa_prompt_kernel(q_ref, gate_ref, kck_ref, kcv_ref, kts_ref, vsa_ref, ktw_ref, vwa_ref, ovl_ref, exp_ref,
                       gsel_ref, gob_ref, out_ref, s_scr, mx_scr, acc_scr, *, n_cmp, n_heads):
    blk = pl.program_id(1)
    start = blk * Q_BLOCK
    q_per_kv = n_heads // N_KV
    q = q_ref[...]
    ncp = kcv_ref.shape[1]
    n_blocks = ovl_ref.shape[0]
    tpos = start + lax.broadcasted_iota(jnp.int32, (Q_BLOCK, 1), 0)
    cn = lax.broadcasted_iota(jnp.int32, (1, ncp), 1)
    cmp_mask = ((cn * CMP_STRIDE + CMP_LEN - 1 <= tpos) & (cn < n_cmp)).astype(F32)
    cmp_mask = jnp.concatenate([cmp_mask] * (N_KV * q_per_kv), axis=0) > 0.5
    grp_lanes = [slice(g * HEAD_DIM, (g + 1) * HEAD_DIM) for g in range(N_KV)]
    qbs = [_stack_heads(q, g, q_per_kv) for g in range(N_KV)]

    def normalised(acc, g):
        return (acc / pltpu.roll(acc, HEAD_DIM, 1))[:, grp_lanes[g]]

    o_cmp, p_slc = [], []
    ovl = ovl_ref[...]
    grp_rows = [slice(g * q_per_kv * Q_BLOCK, (g + 1) * q_per_kv * Q_BLOCK) for g in range(N_KV)]
    s = jnp.concatenate([_dot(qbs[g], kck_ref[0, grp_lanes[g], :]) for g in range(N_KV)], axis=0)
    s = jnp.where(cmp_mask, s, MASKED)
    e = jnp.where(cmp_mask, jnp.exp2(s - jnp.max(s, axis=-1, keepdims=True)), 0.0)
    p_all = e / jnp.maximum(jnp.sum(e, axis=-1, keepdims=True), 1e-30)
    for g in range(N_KV):
        p = p_all[grp_rows[g]]
        o_cmp.append(_dot(p.astype(BF16), kcv_ref[0, :, grp_lanes[g]]))
        p_sum = p[:Q_BLOCK]
        for h in range(1, q_per_kv):
            p_sum = p_sum + p[h * Q_BLOCK:(h + 1) * Q_BLOCK]
        p_hi = p_sum.astype(BF16)
        p_lo = (p_sum - p_hi.astype(F32)).astype(BF16)
        p_slc.append(_dot_nt(ovl, p_hi) + _dot_nt(ovl, p_lo))
    blk_t = (start + lax.broadcasted_iota(jnp.int32, (1, Q_BLOCK), 1)) // SLC_BLOCK
    score = _block_scores(jnp.concatenate(p_slc, axis=1), jnp.concatenate([blk_t] * N_KV, axis=1), n_blocks, axis=0)
    sel = _select_mask_t(score, min(N_SELECT, n_blocks)) * SEL_BONUS
    sel = jnp.concatenate([sel, jnp.zeros((LANES - n_blocks, N_KV * Q_BLOCK), F32)], axis=0)
    q_aug = []
    for g in range(N_KV):
        sel_g = sel[:, g * Q_BLOCK:(g + 1) * Q_BLOCK].T[:, :n_blocks]
        q_aug.append(jnp.concatenate([qbs[g], jnp.concatenate([sel_g] * q_per_kv, axis=0).astype(BF16)], axis=1))

    ws = pl.multiple_of(jnp.maximum(start - WINDOW, 0), LANES)
    dpos = tpos - (ws + lax.broadcasted_iota(jnp.int32, (1, WIN_KEYS), 1))
    win_bias = jnp.where((dpos >= 0) & (dpos < WINDOW), 0.0, MASKED)
    win_bias = jnp.concatenate([win_bias] * (N_KV * q_per_kv), axis=0)
    s = jnp.concatenate([_dot(qbs[g], ktw_ref[0, grp_lanes[g], pl.ds(ws, WIN_KEYS)]) for g in range(N_KV)], axis=0)
    s = s + win_bias
    p_all = jnp.exp2(s - jnp.max(s, axis=-1, keepdims=True)).astype(BF16)
    o_win = [normalised(_dot(p_all[grp_rows[g]], vwa_ref[pl.ds(ws, WIN_KEYS), g * K_LANES:(g + 1) * K_LANES]), g)
             for g in range(N_KV)]

    last = (start + Q_BLOCK - 1) // SLC_TILE
    lane_tiles = SLC_TILE // LANES
    mx_scr[...] = jnp.full(mx_scr.shape, MASKED, F32)
    acc_scr[...] = jnp.zeros(acc_scr.shape, F32)

    def score_tile(kt, masked):
        off = pl.multiple_of(kt * SLC_TILE, SLC_TILE)
        for g in range(N_KV):
            keys = jnp.concatenate([kts_ref[0, grp_lanes[g], pl.ds(off, SLC_TILE)],
                                    exp_ref[:, pl.ds(off, SLC_TILE)]], axis=0)
            s = _dot(q_aug[g], keys)
            if masked:
                row = lax.broadcasted_iota(jnp.int32, (q_per_kv * Q_BLOCK, 1), 0) % Q_BLOCK
                s = jnp.where(off + lax.broadcasted_iota(jnp.int32, (1, SLC_TILE), 1) <= start + row, s, MASKED)
            s_scr[g, :, pl.ds(off, SLC_TILE)] = s
            m = s[:, :LANES]
            for i in range(1, lane_tiles):
                m = jnp.maximum(m, s[:, i * LANES:(i + 1) * LANES])
            mx_scr[g] = jnp.maximum(mx_scr[g], m)

    def pair_loop(n, tile):
        def trip(i, carry):
            for k in range(TILES_PER_TRIP):
                tile(TILES_PER_TRIP * i + k)
            return carry

        lax.fori_loop(0, n // TILES_PER_TRIP, trip, 0)
        done = n // TILES_PER_TRIP * TILES_PER_TRIP
        size = TILES_PER_TRIP // 2
        while size:
            @pl.when((n - done) & size != 0)
            def _(done=done, size=size):
                for k in range(size):
                    tile(done + k)

            done = done + ((n - done) & size)
            size //= 2

    pair_loop(last, lambda kt: score_tile(kt, False))
    score_tile(last, True)
    for g in range(N_KV):
        mx_scr[g] = jnp.broadcast_to(jnp.max(mx_scr[g], axis=-1, keepdims=True), mx_scr.shape[1:])

    def value_tile(kt):
        off = pl.multiple_of(kt * SLC_TILE, SLC_TILE)
        for g in range(N_KV):
            p = jnp.exp2(s_scr[g, :, pl.ds(off, SLC_TILE)] - jnp.concatenate([mx_scr[g]] * lane_tiles, axis=1))
            acc_scr[g] += _dot(p.astype(BF16), vsa_ref[pl.ds(off, SLC_TILE), g * K_LANES:(g + 1) * K_LANES])

    pair_loop(last + 1, value_tile)
    o_slc = [normalised(acc_scr[g], g) for g in range(N_KV)]

    gates = gate_ref[...]
    g_hi = gates.astype(BF16)
    g_lo = (gates - g_hi.astype(F32)).astype(BF16)
    out = None
    for j, branch in enumerate((o_cmp, o_slc, o_win)):
        spread = _dot(g_hi, gsel_ref[j]) + _dot(g_lo, gsel_ref[j])
        o = jnp.concatenate([branch[g][h * Q_BLOCK:(h + 1) * Q_BLOCK] for g in range(N_KV) for h in range(q_per_kv)],
                            axis=1)
        out = spread * o if out is None else out + spread * o
    out_ref[...] = _rms(out, gob_ref[...]).astype(BF16)


def _nsa_prompt(q, gates, kck_t, kcv, kts, vsa, ktw, vwa, w, b, t):
    n, b_width = q.shape
    n_heads = b_width // HEAD_DIM
    assert t % SLC_TILE == 0 and t >= WIN_KEYS
    nqb = t // Q_BLOCK
    ncp = kcv.shape[1]
    n_cmp = ncp - 1
    n_blocks = (n_cmp + 1) * CMP_STRIDE // SLC_BLOCK
    assert n_blocks % SUBLANES == 0 and n_blocks <= LANES
    ovl = _overlap_matrix(ncp, n_blocks, n_blocks).T.astype(BF16)
    key_blk = jnp.arange(t, dtype=jnp.int32)[None, :] // SLC_BLOCK
    expand = (key_blk == jnp.arange(n_blocks, dtype=jnp.int32)[:, None]).astype(BF16)
    col = jnp.arange(LANES, dtype=jnp.int32)[None, :, None]
    head = jnp.arange(b_width, dtype=jnp.int32)[None, None, :] // HEAD_DIM
    gate_sel = (col == head * 3 + jnp.arange(3, dtype=jnp.int32)[:, None, None]).astype(BF16)
    tok = lambda i, j: (i * nqb + j, 0)
    seq3 = lambda i, j: (i, 0, 0)
    seq2 = lambda i, j: (i, 0)
    rows = (n_heads // N_KV) * Q_BLOCK
    return pl.pallas_call(
        functools.partial(_nsa_prompt_kernel, n_cmp=n_cmp, n_heads=n_heads),
        grid=(b, nqb),
        in_specs=[pl.BlockSpec((Q_BLOCK, b_width), tok), pl.BlockSpec((Q_BLOCK, LANES), tok),
                  pl.BlockSpec((1, K_LANES, ncp), seq3), pl.BlockSpec((1, ncp, K_LANES), seq3),
                  pl.BlockSpec((1, K_LANES, t), seq3), pl.BlockSpec((t, N_KV * K_LANES), seq2),
                  pl.BlockSpec((1, K_LANES, t), seq3), pl.BlockSpec((t, N_KV * K_LANES), seq2),
                  _full(ovl.shape), _full(expand.shape), _full(gate_sel.shape), _full(w["g_out_b"].shape)],
        out_specs=pl.BlockSpec((Q_BLOCK, b_width), tok),
        out_shape=jax.ShapeDtypeStruct((n, b_width), BF16),
        scratch_shapes=[pltpu.VMEM((N_KV, rows, t), F32), pltpu.VMEM((N_KV, rows, LANES), F32),
                        pltpu.VMEM((N_KV, rows, K_LANES), F32)],
        compiler_params=_params("parallel", "arbitrary"), name="nsa_prompt",
    )(q, gates, kck_t, kcv, kts, vsa, ktw, vwa, ovl, expand, gate_sel, w["g_out_b"])


def _overlap_matrix(rows, cols, n_blocks):
    ci = jnp.arange(rows, dtype=jnp.int32)[:, None]
    sj = jnp.arange(cols, dtype=jnp.int32)[None, :]
    hit = (ci * CMP_STRIDE < (sj + 1) * SLC_BLOCK) & (ci * CMP_STRIDE + CMP_LEN > sj * SLC_BLOCK) & (sj < n_blocks)
    return hit.astype(F32)


def _compress_sample_kernel(pt_ref, *refs, pages_per_step, n_chunks, n_steps):
    del pt_ref
    pages = refs[:pages_per_step]
    new_ref, w1k_ref, w1v_ref, b1_ref, w2_ref, kc_ref, fsk_scr, fsv_scr = refs[pages_per_step:pages_per_step + 8]
    chunk_scr = refs[pages_per_step + 8:]
    chunk_pages = pages_per_step // n_chunks
    j = pl.program_id(1)
    blocks_per_page = PAGE_SIZE // CMP_STRIDE
    chunk_blocks = chunk_pages * blocks_per_page

    def scratch(c):
        return chunk_scr[2 * (c % 2)], chunk_scr[2 * (c % 2) + 1]

    def transpose_chunk(c):
        xk_scr, xv_scr = scratch(c)
        for i in range(chunk_pages):
            page = pages[c * chunk_pages + i]
            xk_scr[pl.ds(i * PAGE_SIZE, PAGE_SIZE), :] = page[0, :K_LANES, :].T
            xv_scr[pl.ds(i * PAGE_SIZE, PAGE_SIZE), :] = page[0, K_LANES:, :].T

    def project_chunk(c):
        xk_scr, xv_scr = scratch(c)
        fs_k, fs_v = _compress_partial(lambda s: xk_scr[pl.ds(s, chunk_blocks, stride=CMP_STRIDE), :],
                                       lambda s: xv_scr[pl.ds(s, chunk_blocks, stride=CMP_STRIDE), :],
                                       w1k_ref, w1v_ref)
        off = pl.multiple_of((j * n_chunks + c) * chunk_blocks, chunk_blocks)
        fsk_scr[pl.ds(off, chunk_blocks), :] = fs_k
        fsv_scr[pl.ds(off, chunk_blocks), :] = fs_v

    transpose_chunk(0)
    for c in range(n_chunks):
        if c + 1 < n_chunks:
            transpose_chunk(c + 1)
        project_chunk(c)

    @pl.when(j == n_steps - 1)
    def _():
        past_blocks = n_steps * pages_per_step * blocks_per_page
        tail = fsk_scr.shape[0] - past_blocks
        new = new_ref[0]
        is_first = lax.broadcasted_iota(jnp.int32, (tail, 1), 0) == 0
        nk = _dot(new[:, :K_LANES].astype(BF16), w1k_ref[0, :K_LANES, :])
        nv = _dot(new[:, K_LANES:].astype(BF16), w1v_ref[0, :K_LANES, :])
        fsk_scr[pl.ds(past_blocks, tail), :] = jnp.where(is_first, nk, 0.0)
        fsv_scr[pl.ds(past_blocks, tail), :] = jnp.where(is_first, nv, 0.0)
        kc_ref[0] = _compress_finish(fsk_scr[...], fsv_scr[...], b1_ref, w2_ref)


def _compress_sample(cache_cmp_t, page_table, new_rows, w):
    db, n_pages = page_table.shape
    pages_per_step, n_chunks = min(CMP_PAGES_PER_STEP, n_pages), CMP_PAGE_CHUNKS
    assert n_pages % pages_per_step == 0 and pages_per_step % n_chunks == 0
    n_steps = n_pages // pages_per_step
    blocks_per_page = PAGE_SIZE // CMP_STRIDE
    past_blocks = n_pages * blocks_per_page
    nbp = past_blocks + SUBLANES
    weights = [w["w1k"], w["w1v"], w["b_c1"], w["w2"]]
    hid2 = w["w1k"].shape[2]
    page_spec = lambda i: pl.BlockSpec((1, KV_LANES, PAGE_SIZE),
                                       lambda b, j, pt, i=i: (pt[b, j * pages_per_step + i], 0, 0))
    chunk_rows = pages_per_step // n_chunks * PAGE_SIZE
    grid_spec = pltpu.PrefetchScalarGridSpec(
        num_scalar_prefetch=1, grid=(db, n_steps),
        in_specs=[page_spec(i) for i in range(pages_per_step)]
        + [pl.BlockSpec((1, 1, KV_LANES), lambda b, j, pt: (b, 0, 0))]
        + [pl.BlockSpec(a.shape, lambda b, j, pt, nd=a.ndim: (0,) * nd) for a in weights],
        out_specs=pl.BlockSpec((1, nbp, KV_LANES), lambda b, j, pt: (b, 0, 0)),
        scratch_shapes=[pltpu.VMEM((nbp, hid2), F32), pltpu.VMEM((nbp, hid2), F32)]
        + [pltpu.VMEM((chunk_rows, K_LANES), F32)] * 4)
    return pl.pallas_call(
        functools.partial(_compress_sample_kernel, pages_per_step=pages_per_step, n_chunks=n_chunks,
                          n_steps=n_steps),
        grid_spec=grid_spec, out_shape=jax.ShapeDtypeStruct((db, nbp, KV_LANES), F32),
        compiler_params=_params("parallel", "arbitrary"), name="compress_sample",
    )(page_table, *([cache_cmp_t] * pages_per_step), new_rows[:, None, :], *weights)


def _select_sample_kernel(q_ref, kc_ref, ocmp_ref, psum_ref, *, pos, n_cmp, n_heads):
    q_per_kv = n_heads // N_KV
    q = q_ref[0] * (LOG2_E * HEAD_DIM ** -0.5)
    kc = kc_ref[0]
    ncp = kc.shape[0]
    cn = lax.broadcasted_iota(jnp.int32, (1, ncp), 1)
    mask = (cn * CMP_STRIDE + CMP_LEN - 1 <= pos) & (cn < n_cmp)
    o_rows, p_rows = [], []
    for g in range(N_KV):
        qg = _stack_heads(q, g, q_per_kv).astype(BF16)
        s = jnp.where(mask, _dot_nt(qg, kc[:, g * HEAD_DIM:(g + 1) * HEAD_DIM].astype(BF16)), MASKED)
        e = jnp.where(mask, jnp.exp2(s - jnp.max(s, axis=-1, keepdims=True)), 0.0)
        p = e / jnp.maximum(jnp.sum(e, axis=-1, keepdims=True), 1e-30)
        o_rows.append(_dot(p.astype(BF16), kc[:, K_LANES + g * HEAD_DIM:K_LANES + (g + 1) * HEAD_DIM].astype(BF16)))
        p_rows.append(jnp.sum(p, axis=0, keepdims=True))
    ocmp_ref[0] = jnp.concatenate(o_rows, axis=0)
    psum_ref[0] = jnp.concatenate(p_rows, axis=0)


def _pick_sample_kernel(psum_ref, ovl_ref, idx_ref, *, pos, n_blocks):
    p = psum_ref[...]
    p_hi = p.astype(BF16)
    p_lo = (p - p_hi.astype(F32)).astype(BF16)
    ovl = ovl_ref[...]
    score = _block_scores(_dot(p_hi, ovl) + _dot(p_lo, ovl), pos // SLC_BLOCK, n_blocks)
    _, picks = _select_blocks(score, min(N_SELECT, n_blocks))
    lane = lax.broadcasted_iota(jnp.int32, idx_ref.shape, 1)
    out = jnp.full(idx_ref.shape, -1.0, F32)
    for i, (idx, ok) in enumerate(picks):
        out = jnp.where((lane == i) & ok, idx, out)
    idx_ref[...] = out.astype(jnp.int32)


def _select_sample(q, kc, pos, n_cmp, n_blocks):
    db, b_width = q.shape
    n_heads = b_width // HEAD_DIM
    ncp = kc.shape[1]
    o_cmp, p_sum = pl.pallas_call(
        functools.partial(_select_sample_kernel, pos=pos, n_cmp=n_cmp, n_heads=n_heads),
        grid=(db,),
        in_specs=[pl.BlockSpec((1, 1, b_width), lambda i: (i, 0, 0)),
                  pl.BlockSpec((1, ncp, KV_LANES), lambda i: (i, 0, 0))],
        out_specs=[pl.BlockSpec((1, n_heads, HEAD_DIM), lambda i: (i, 0, 0)),
                   pl.BlockSpec((1, N_KV, ncp), lambda i: (i, 0, 0))],
        out_shape=[jax.ShapeDtypeStruct((db, n_heads, HEAD_DIM), F32),
                   jax.ShapeDtypeStruct((db, N_KV, ncp), F32)],
        compiler_params=_params("parallel"), name="select_sample",
    )(q[:, None, :], kc)
    nsp = -(-n_blocks // LANES) * LANES
    ovl = _overlap_matrix(ncp, nsp, n_blocks).astype(BF16)
    idx = pl.pallas_call(
        functools.partial(_pick_sample_kernel, pos=pos, n_blocks=n_blocks),
        grid=(1,),
        in_specs=[_full((db * N_KV, ncp)), _full(ovl.shape)],
        out_specs=_full((db * N_KV, LANES)),
        out_shape=jax.ShapeDtypeStruct((db * N_KV, LANES), jnp.int32),
        compiler_params=_params("arbitrary"), name="pick_sample",
    )(p_sum.reshape(db * N_KV, ncp), ovl)
    return o_cmp, idx.reshape(db, N_KV, LANES)


def _attend_sample_kernel(page_ref, *refs, n_sel, past_blocks, n_heads):
    del page_ref
    n_slots = N_KV * n_sel
    pages = refs[:n_slots]
    (q_ref, gate_ref, ocmp_ref, idx_ref, newslc_ref, win_ref, newwin_ref, exp_ref, gob_ref,
     out_ref, winout_ref) = refs[n_slots:]
    q_per_kv = n_heads // N_KV
    q = q_ref[0] * (HEAD_DIM ** -0.5)
    lane = lax.broadcasted_iota(jnp.int32, (1, K_LANES), 1)
    wb = win_ref.shape[2]
    is_last = lax.broadcasted_iota(jnp.int32, (1, wb), 1) == wb - 1
    win = jnp.where(is_last, newwin_ref[0], pltpu.roll(win_ref[0], wb - 1, 1))
    winout_ref[0] = win
    win_k = win[:K_LANES].astype(BF16)
    win_v = win[K_LANES:].astype(BF16)
    new_slc = newslc_ref[0]
    idx = idx_ref[0].astype(F32)
    n_keys = n_sel * PAGE_SIZE
    key_half = (lax.broadcasted_iota(jnp.int32, (1, n_keys), 1) % PAGE_SIZE) // SLC_BLOCK
    o_slc, o_win = [], []
    for g in range(N_KV):
        qpad = jnp.concatenate(
            [jnp.where(lane // HEAD_DIM == g,
                       jnp.concatenate([q[:, (g * q_per_kv + h) * HEAD_DIM:(g * q_per_kv + h + 1) * HEAD_DIM]] * N_KV,
                                       axis=1), 0.0)
             for h in range(q_per_kv)], axis=0)
        qpb = qpad.astype(BF16)
        kt = jnp.concatenate([pages[g * n_sel + i][0, :K_LANES, :] for i in range(n_sel)], axis=1).astype(BF16)
        vt = jnp.concatenate([pages[g * n_sel + i][0, K_LANES:, :] for i in range(n_sel)], axis=1).astype(BF16)
        s = _dot(qpb, kt)
        idg = idx[g:g + 1, :]
        idk = jnp.dot(idg, exp_ref[...], precision=lax.Precision.HIGHEST, preferred_element_type=F32)
        parity = idk - 2.0 * jnp.floor(idk * 0.5)
        key_ok = (idk >= 0.0) & (idk < past_blocks) & (parity == key_half.astype(F32))
        s = jnp.where(key_ok, s, MASKED)
        has_new = jnp.max(jnp.where(idg == past_blocks, 1.0, 0.0), axis=-1, keepdims=True) > 0.5
        s_new = jnp.sum(qpad * new_slc[:, :K_LANES], axis=-1, keepdims=True)
        s_new = jnp.where(has_new, s_new, MASKED)
        m = jnp.maximum(jnp.max(s, axis=-1, keepdims=True), s_new)
        e = jnp.where(key_ok, jnp.exp(s - m), 0.0)
        e_new = jnp.where(has_new, jnp.exp(s_new - m), 0.0)
        den = jnp.maximum(jnp.sum(e, axis=-1, keepdims=True) + e_new, 1e-30)
        o = (_dot_nt(e.astype(BF16), vt) + e_new * new_slc[:, K_LANES:]) / den
        o_slc.append(o[:, g * HEAD_DIM:(g + 1) * HEAD_DIM])
        s = _dot(qpb, win_k)
        e = jnp.exp(s - jnp.max(s, axis=-1, keepdims=True))
        o = _dot_nt(e.astype(BF16), win_v) / jnp.sum(e, axis=-1, keepdims=True)
        o_win.append(o[:, g * HEAD_DIM:(g + 1) * HEAD_DIM])
    gates = gate_ref[0]
    o = (gates[:, 0:1] * ocmp_ref[0] + gates[:, 1:2] * jnp.concatenate(o_slc, axis=0)
         + gates[:, 2:3] * jnp.concatenate(o_win, axis=0))
    ms = jnp.sum(jnp.sum(o * o, axis=-1, keepdims=True), axis=0, keepdims=True) / (n_heads * HEAD_DIM)
    out_ref[0] = o * lax.rsqrt(ms + NORM_EPS) * gob_ref[...]


def _attend_sample(q, gates, o_cmp, idx, cache_slc_t, page_table, new_slc, win_t, new_win, w, past_blocks):
    db, b_width = q.shape
    n_heads = b_width // HEAD_DIM
    n_sel = min(N_SELECT, past_blocks + 1)
    sub_per_page = PAGE_SIZE // SLC_BLOCK
    jp = jnp.clip(idx[:, :, :n_sel], 0, past_blocks - 1)
    page = jnp.take_along_axis(page_table, (jp // sub_per_page).reshape(db, -1), axis=1).astype(jnp.int32)
    n_keys = n_sel * PAGE_SIZE
    expand = (jnp.arange(n_keys, dtype=jnp.int32)[None, :] // PAGE_SIZE
              == jnp.arange(LANES, dtype=jnp.int32)[:, None]).astype(F32)
    gob = w["g_out_b"].reshape(n_heads, HEAD_DIM)
    wb = win_t.shape[2]
    per_seq = lambda shape: pl.BlockSpec((1,) + shape, lambda b, pg: (b, 0, 0))
    page_spec = lambda i: pl.BlockSpec((1, KV_LANES, PAGE_SIZE), lambda b, pg, i=i: (pg[b, i], 0, 0))
    grid_spec = pltpu.PrefetchScalarGridSpec(
        num_scalar_prefetch=1, grid=(db,),
        in_specs=[page_spec(i) for i in range(N_KV * n_sel)]
        + [per_seq((1, b_width)), per_seq((n_heads, 3)), per_seq((n_heads, HEAD_DIM)), per_seq((N_KV, LANES)),
           per_seq((1, KV_LANES)), per_seq((KV_LANES, wb)), per_seq((KV_LANES, 1)),
           pl.BlockSpec(expand.shape, lambda b, pg: (0, 0)), pl.BlockSpec(gob.shape, lambda b, pg: (0, 0))],
        out_specs=[per_seq((n_heads, HEAD_DIM)), per_seq((KV_LANES, wb))])
    return pl.pallas_call(
        functools.partial(_attend_sample_kernel, n_sel=n_sel, past_blocks=past_blocks, n_heads=n_heads),
        grid_spec=grid_spec,
        out_shape=[jax.ShapeDtypeStruct((db, n_heads, HEAD_DIM), F32),
                   jax.ShapeDtypeStruct((db, KV_LANES, wb), F32)],
        compiler_params=_params("parallel"), name="attend_sample",
    )(page, *([cache_slc_t] * (N_KV * n_sel)), q[:, None, :], gates[:, :n_heads * 3].reshape(db, n_heads, 3), o_cmp,
      idx, new_slc[:, None, :], win_t, new_win[:, :, None], expand, gob)


def _ffn_kernel(x_ref, ma_ref, mb_ref, woa_ref, wob_ref, gffn_ref, wgate_ref, wup_ref, wdown_ref, gfin_ref, y_ref,
                acc_scr, xn_scr):
    c = pl.program_id(1)

    @pl.when(c == 0)
    def _():
        x = x_ref[...] + _dot(ma_ref[...], woa_ref[...]) + _dot(mb_ref[...], wob_ref[...])
        acc_scr[...] = x
        xn_scr[...] = _rms(x, gffn_ref[...]).astype(BF16)

    xn = xn_scr[...]
    hid = jax.nn.silu(_dot(xn, wgate_ref[...])) * _dot(xn, wup_ref[...])
    acc_scr[...] += _dot(hid.astype(BF16), wdown_ref[...])

    @pl.when(c == pl.num_programs(1) - 1)
    def _():
        y_ref[...] = _rms(acc_scr[...], gfin_ref[...])


def _output_ffn(x2, mix_a, mix_b, w, tm):
    n, d = x2.shape
    d_ff = w["w_down"].shape[0]
    ff_chunks = FFN_CHUNKS
    step = d_ff // ff_chunks
    assert n % tm == 0 and d_ff % ff_chunks == 0 and step % LANES == 0
    row = lambda i, c: (i, 0)
    fixed = lambda a: pl.BlockSpec(a.shape, lambda i, c: (0, 0))
    return pl.pallas_call(
        _ffn_kernel,
        grid=(n // tm, ff_chunks),
        in_specs=[pl.BlockSpec((tm, d), row), pl.BlockSpec((tm, mix_a.shape[1]), row),
                  pl.BlockSpec((tm, mix_b.shape[1]), row), fixed(w["wo_a"]), fixed(w["wo_b"]), fixed(w["g_ffn"]),
                  pl.BlockSpec((d, step), lambda i, c: (0, c)), pl.BlockSpec((d, step), lambda i, c: (0, c)),
                  pl.BlockSpec((step, d), lambda i, c: (c, 0)), fixed(w["g_final"])],
        out_specs=pl.BlockSpec((tm, d), row),
        out_shape=jax.ShapeDtypeStruct((n, d), F32),
        scratch_shapes=[pltpu.VMEM((tm, d), F32), pltpu.VMEM((tm, d), BF16)],
        compiler_params=_params("parallel", "arbitrary"), name="output_ffn",
    )(x2, mix_a, mix_b, w["wo_a"], w["wo_b"], w["g_ffn"], w["w_gate"], w["w_up"], w["w_down"], w["g_final"])


def _prepare_weights(l, g_attn, w_in, g_sgu, w_s, b_s, w_c1, b_c1, w_c2, g_out_a, g_out_b, w_out,
                     g_ffn, w_gate_up, w_down, g_final):
    a_width = g_sgu.shape[1]
    b_width = g_out_b.shape[1]
    n_heads = b_width // HEAD_DIM
    d_ff = w_down.shape[1]
    hid = b_c1.shape[2]
    o1 = 2 * a_width
    o2 = o1 + b_width
    o3 = o2 + 3 * KV_LANES
    wi = w_in[l]
    wg = jnp.pad(wi[:, o3:], ((0, 0), (0, LANES - 3 * n_heads)))
    row = lambda a: a.reshape(1, -1)
    w1 = w_c1[l].reshape(2, 2, CMP_STRIDE, HEAD_DIM, hid)
    w1 = jnp.transpose(w1, (0, 2, 3, 1, 4)).reshape(2, CMP_STRIDE, HEAD_DIM, 2 * hid)
    eye = jnp.eye(N_KV, dtype=F32)
    w1 = jnp.einsum("gh,csdn->csgdhn", eye, w1).reshape(2, CMP_STRIDE // CMP_STACK, CMP_STACK * K_LANES, N_KV * 2 * hid)
    w2 = jnp.einsum("gh,cne->cgnhe", eye, w_c2[l]).reshape(2, N_KV * hid, K_LANES)
    return {
        "g_attn": row(g_attn[l]), "wuv": wi[:, :o1].astype(BF16), "wq": wi[:, o1:o2].astype(BF16),
        "wkv": wi[:, o2:o3].astype(BF16), "wg": wg.astype(BF16),
        "g_sgu": row(g_sgu[l]), "g_out_a": row(g_out_a[l]), "g_out_b": row(g_out_b[l]),
        "w_s": w_s[l], "bs_full": jnp.repeat(b_s[l].T, LANES, axis=1),
        "ws0": row(jnp.repeat(w_s[l][:, 0, 0], LANES)), "bs0": row(jnp.repeat(b_s[l][:, 0], LANES)),
        "w1k": w1[0].astype(BF16), "w1v": w1[1].astype(BF16), "b_c1": b_c1[l], "w2": w2.astype(BF16),
        "wo_a": w_out[l][:a_width].astype(BF16), "wo_b": w_out[l][a_width:].astype(BF16),
        "g_ffn": row(g_ffn[l]), "w_gate": w_gate_up[l][:, :d_ff].astype(BF16),
        "w_up": w_gate_up[l][:, d_ff:].astype(BF16), "w_down": w_down[l].astype(BF16), "g_final": row(g_final),
    }


def kernel(x_prompt, x_sample, cache_cmp_kv, cache_slc_kv, state_win_kv, page_table, g_attn, w_in, g_sgu, w_s, b_s,
           w_c1, b_c1, w_c2, g_out_a, g_out_b, w_out, g_ffn, w_gate_up, w_down, g_final):
    depth = w_in.shape[0]
    b, t, d = x_prompt.shape
    db, t_s, _ = x_sample.shape
    assert depth == 1 and t_s == 1
    n_pages = page_table.shape[1]
    past = n_pages * PAGE_SIZE
    wb = state_win_kv.shape[2]
    assert wb == WINDOW and past % SLC_BLOCK == 0
    l = 0
    w = _prepare_weights(l, g_attn, w_in, g_sgu, w_s, b_s, w_c1, b_c1, w_c2, g_out_a, g_out_b, w_out,
                         g_ffn, w_gate_up, w_down, g_final)
    xp = x_prompt.reshape(b * t, d)
    mix_a, q, kvc, gates, kvc_t, kvs_t, kvw_t, kts, vs, ktw, vw = _in_projection(
        xp, jnp.arange(t, dtype=jnp.int32), w, prompt_shape=(b, t))
    kck_t, kcv = _compress_prompt(kvc.reshape(b, t, KV_LANES), w)
    mix_b = _nsa_prompt(q, gates, kck_t, kcv, kts, vs, ktw, vw, w, b, t)
    y_prompt = _output_ffn(xp, mix_a, mix_b, w, tm=min(FFN_ROWS, b * t)).reshape(b, t, d)

    xs = x_sample.reshape(db, d)
    pos_s = past + jnp.zeros((db,), jnp.int32)
    mix_a_s, q_s, kvc_s, kvs_s, kvw_s, gates_s, v_rows = _in_projection(xs, pos_s, w)
    kc_s = _compress_sample(_feature_major(cache_cmp_kv[l]), page_table, kvc_s, w)
    lp = -(-(past + t_s) // SLC_BLOCK) * SLC_BLOCK
    n_cmp_s = lp // CMP_STRIDE - 1
    n_blocks_s = (n_cmp_s + 1) * CMP_STRIDE // SLC_BLOCK
    o_cmp_s, idx_s = _select_sample(q_s, kc_s, past, n_cmp_s, n_blocks_s)
    mix_b_s, win_new_t = _attend_sample(q_s, gates_s, o_cmp_s, idx_s, _feature_major(cache_slc_kv[l]), page_table,
                                        kvs_s, _feature_major(state_win_kv[l]), kvw_s, w, past // SLC_BLOCK)
    y_sample = _output_ffn(xs, mix_a_s, mix_b_s.reshape(db, -1).astype(BF16), w, tm=db).reshape(db, t_s, d)

    kv_shape = (2, N_KV, HEAD_DIM)
    return (y_prompt, y_sample,
            _row_major(kvc_t)[None], _row_major(kvs_t)[None], _row_major(kvw_t[:, :, t - min(WINDOW, t):])[None],
            kvc_s.reshape(1, db, t_s, *kv_shape), kvs_s.reshape(1, db, t_s, *kv_shape),
            _row_major(win_new_t)[None], v_rows.reshape(1, db, t_s, -1))


def _feature_major(kv):
    n, rows = kv.shape[:2]
    return jnp.transpose(kv, (0, 2, 3, 4, 1)).reshape(n, KV_LANES, rows)


def _row_major(kv_t):
    n, _, rows = kv_t.shape
    return jnp.transpose(kv_t.reshape(n, 2, N_KV, HEAD_DIM, rows), (0, 4, 1, 2, 3))
```

```python
import functools

import jax
import jax.numpy as jnp
from jax import lax
from jax.experimental import pallas as pl
from jax.experimental.pallas import tpu as pltpu

F32 = jnp.float32
BF16 = jnp.bfloat16

A_GROUPS = 4
CHUNK = 128
HEAD_DIM = 64
N_KV = 2
ROT_DIM = HEAD_DIM // 4
ROPE_THETA = 500000.0
CMP_LEN = 32
CMP_STRIDE = 16
SLC_BLOCK = 64
N_SELECT = 16
WINDOW = 512
Q_BLOCK = 128
FORCE_BONUS = 1000.0
PAGE_SIZE = 128
NORM_EPS = 1e-6
MASKED = -1e30
LOG2_E = 1.4426950408889634
SEL_BONUS = 16384.0

LANES = 128
SUBLANES = 8
VMEM_LIMIT_BYTES = 56 * 1024 * 1024

PROJ_ROWS = 512
FFN_ROWS = 512
FFN_CHUNKS = 2
CMP_CHUNK_PAGES = 32

KV_LANES = 2 * N_KV * HEAD_DIM
K_LANES = N_KV * HEAD_DIM
CMP_STACK = 2
SLC_TILE = 512
TILES_PER_TRIP = 4
WIN_KEYS = WINDOW + Q_BLOCK


def _rms(x, g):
    return x * lax.rsqrt(jnp.mean(x * x, axis=-1, keepdims=True) + NORM_EPS) * g


def _dot(a, b):
    return jnp.dot(a, b, preferred_element_type=F32)


def _dot_nt(a, b, precision=None):
    return lax.dot_general(a, b, (((1,), (1,)), ((), ())), precision=precision,
                           preferred_element_type=F32)


def _rope(z, rc, rs1, rs2):
    return z * rc + pltpu.roll(z, LANES - ROT_DIM // 2, 1) * rs1 + pltpu.roll(z, ROT_DIM // 2, 1) * rs2


def _project(x_ref, gattn_ref, wuv_ref, wq_ref, wkv_ref, wg_ref, rc_ref, rs1_ref, rs2_ref):
    xn = _rms(x_ref[...], gattn_ref[...]).astype(BF16)
    rc, rs1, rs2 = rc_ref[...], rs1_ref[...], rs2_ref[...]
    zuv = _dot(xn, wuv_ref[...])
    a_width = zuv.shape[1] // 2
    zq = _dot(xn, wq_ref[...])
    q = jnp.concatenate([_rope(zq[:, i * LANES:(i + 1) * LANES], rc, rs1, rs2)
                         for i in range(zq.shape[1] // LANES)], axis=1)
    zkv = _dot(xn, wkv_ref[...])
    branches = []
    for br in range(3):
        k = _rope(zkv[:, br * KV_LANES:br * KV_LANES + K_LANES], rc, rs1, rs2)
        v = zkv[:, br * KV_LANES + K_LANES:(br + 1) * KV_LANES]
        branches.append((k, v))
    gates = jax.nn.sigmoid(_dot(xn, wg_ref[...]))
    return zuv[:, :a_width], zuv[:, a_width:], q, branches, gates


def _gmlp_norm_v(v, gsgu):
    v = jax.nn.gelu(v)
    return jnp.concatenate([_rms(v[:, g * LANES:(g + 1) * LANES], gsgu[:, g * LANES:(g + 1) * LANES])
                            for g in range(A_GROUPS)], axis=1)


def _inproj_prompt_kernel(x_ref, gattn_ref, wuv_ref, wq_ref, wkv_ref, wg_ref, rc_ref, rs1_ref, rs2_ref,
                          ws_ref, bs_ref, gsgu_ref, goa_ref,
                          mixa_ref, q_ref, kvc_ref, gate_ref, kvct_ref, kvst_ref, kvwt_ref,
                          kts_ref, vs_ref, ktw_ref, vw_ref):
    u, v, q, branches, gates = _project(x_ref, gattn_ref, wuv_ref, wq_ref, wkv_ref, wg_ref,
                                        rc_ref, rs1_ref, rs2_ref)
    tm = u.shape[0]
    u = jax.nn.gelu(u)
    vg = _gmlp_norm_v(v, gsgu_ref[...]).astype(BF16)
    row = lax.broadcasted_iota(jnp.int32, (CHUNK, CHUNK), 0)
    col = lax.broadcasted_iota(jnp.int32, (CHUNK, CHUNK), 1)
    bias = bs_ref[...]
    parts = []
    for g in range(A_GROUPS):
        w = jnp.where(row >= col, ws_ref[g], 0.0).astype(BF16)
        s = jnp.concatenate(
            [_dot(w, vg[c * CHUNK:(c + 1) * CHUNK, g * LANES:(g + 1) * LANES]) for c in range(tm // CHUNK)],
            axis=0)
        s = s + jnp.concatenate([bias[:, g * LANES:(g + 1) * LANES]] * (tm // CHUNK), axis=0)
        parts.append(u[:, g * LANES:(g + 1) * LANES] * s)
    mixa_ref[...] = _rms(jnp.concatenate(parts, axis=1), goa_ref[...]).astype(BF16)
    q_ref[...] = (q * (LOG2_E * HEAD_DIM ** -0.5)).astype(BF16)
    gate_ref[...] = gates
    kvc_ref[...] = jnp.concatenate(branches[0], axis=1)
    kts = []
    for ref, (k, v_) in zip((kvct_ref, kvst_ref, kvwt_ref), branches):
        kt = k.T
        ref[0, :K_LANES, :] = kt
        ref[0, K_LANES:, :] = v_.T
        kts.append(kt)
    own = [lax.broadcasted_iota(jnp.int32, (1, K_LANES), 1) // HEAD_DIM == g for g in range(N_KV)]
    for kt_ref, va_ref, br in ((kts_ref, vs_ref, 1), (ktw_ref, vw_ref, 2)):
        kt_ref[0] = kts[br].astype(BF16)
        va_ref[...] = jnp.concatenate([jnp.where(m, branches[br][1], 1.0) for m in own], axis=1).astype(BF16)


def _inproj_sample_kernel(x_ref, gattn_ref, wuv_ref, wq_ref, wkv_ref, wg_ref, rc_ref, rs1_ref, rs2_ref,
                          ws0_ref, bs0_ref, gsgu_ref, goa_ref,
                          mixa_ref, q_ref, kvc_ref, kvs_ref, kvw_ref, gate_ref, vrow_ref):
    u, v, q, branches, gates = _project(x_ref, gattn_ref, wuv_ref, wq_ref, wkv_ref, wg_ref,
                                        rc_ref, rs1_ref, rs2_ref)
    vg = _gmlp_norm_v(v, gsgu_ref[...])
    o_a = jax.nn.gelu(u) * (vg * ws0_ref[...] + bs0_ref[...])
    mixa_ref[...] = _rms(o_a, goa_ref[...]).astype(BF16)
    vrow_ref[...] = vg
    q_ref[...] = q
    for ref, (k, v_) in zip((kvc_ref, kvs_ref, kvw_ref), branches):
        ref[...] = jnp.concatenate([k, v_], axis=1)
    gate_ref[...] = gates


def _full(shape):
    return pl.BlockSpec(shape, lambda *_: (0,) * len(shape))


def _params(*sem):
    return pltpu.CompilerParams(dimension_semantics=sem, vmem_limit_bytes=VMEM_LIMIT_BYTES)


def _in_projection(x2, pos, w, *, prompt_shape=None):
    n, d = x2.shape
    rc, rs1, rs2 = _rope_tables(pos)
    a_width = w["wuv"].shape[1] // 2
    b_width = w["wq"].shape[1]
    weights = [w["g_attn"], w["wuv"], w["wq"], w["wkv"], w["wg"]]
    wspecs = [_full(a.shape) for a in weights]
    tail = [w["g_sgu"], w["g_out_a"]]
    if prompt_shape is None:
        tm, grid = n, (1,)
        row = lambda i: (i, 0)
        rope_map = row
        gm = [w["ws0"], w["bs0"]]
    else:
        b, t = prompt_shape
        tm = PROJ_ROWS
        assert t % tm == 0 and tm % CHUNK == 0
        tpb = t // tm
        grid = (b * tpb,)
        row = lambda i: (i, 0)
        rope_map = lambda i: (i % tpb, 0)
        gm = [w["w_s"], w["bs_full"]]
    rspec = pl.BlockSpec((tm, LANES), rope_map)
    in_specs = ([pl.BlockSpec((tm, d), row)] + wspecs + [rspec] * 3
                + [_full(a.shape) for a in gm] + [_full(a.shape) for a in tail])
    if prompt_shape is None:
        kern = _inproj_sample_kernel
        outs = [((n, a_width), BF16), ((n, b_width), F32), ((n, KV_LANES), F32), ((n, KV_LANES), F32),
                ((n, KV_LANES), F32), ((n, LANES), F32), ((n, a_width), F32)]
        out_specs = [pl.BlockSpec((tm, s[1]), row) for s, _ in outs]
    else:
        kern = _inproj_prompt_kernel
        kt_map = lambda i: (i // tpb, 0, i % tpb)
        outs = [((n, a_width), BF16), ((n, b_width), BF16), ((n, KV_LANES), F32), ((n, LANES), F32)]
        out_specs = [pl.BlockSpec((tm, s[1]), row) for s, _ in outs]
        outs += [((b, KV_LANES, t), F32)] * 3
        out_specs += [pl.BlockSpec((1, KV_LANES, tm), kt_map)] * 3
        outs += [((b, K_LANES, t), BF16), ((n, N_KV * K_LANES), BF16)] * 2
        out_specs += [pl.BlockSpec((1, K_LANES, tm), kt_map), pl.BlockSpec((tm, N_KV * K_LANES), row)] * 2
    return pl.pallas_call(
        kern, grid=grid, in_specs=in_specs, out_specs=out_specs,
        out_shape=[jax.ShapeDtypeStruct(s, dt) for s, dt in outs],
        compiler_params=_params("parallel"), name="in_projection",
    )(x2, *weights, rc, rs1, rs2, *gm, *tail)


def _rope_tables(pos):
    half = ROT_DIM // 2
    inv = ROPE_THETA ** (-jnp.arange(half, dtype=F32) / half)
    ang = pos.astype(F32)[:, None] * inv[None, :]
    cos, sin = jnp.cos(ang), jnp.sin(ang)
    n = pos.shape[0]
    rest0 = jnp.zeros((n, HEAD_DIM - ROT_DIM), F32)
    zero = jnp.zeros((n, half), F32)
    rc = jnp.concatenate([cos, cos, rest0 + 1.0], axis=1)
    rs1 = jnp.concatenate([-sin, zero, rest0], axis=1)
    rs2 = jnp.concatenate([zero, sin, rest0], axis=1)
    return tuple(jnp.tile(a, (1, LANES // HEAD_DIM)) for a in (rc, rs1, rs2))


def _compress_partial(read_k, read_v, w1k_ref, w1v_ref):
    acc_k = acc_v = None
    stack = w1k_ref.shape[1] // K_LANES
    for i in range(CMP_STRIDE // stack):
        rows = range(i * stack, (i + 1) * stack)
        pk = _dot(jnp.concatenate([read_k(s).astype(BF16) for s in rows], axis=1), w1k_ref[i])
        pv = _dot(jnp.concatenate([read_v(s).astype(BF16) for s in rows], axis=1), w1v_ref[i])
        acc_k = pk if acc_k is None else acc_k + pk
        acc_v = pv if acc_v is None else acc_v + pv
    return acc_k, acc_v


def _compress_finish(fs_k, fs_v, b1_ref, w2_ref):
    hid = b1_ref.shape[1]
    outs = []
    for c, fs in enumerate((fs_k, fs_v)):
        hs = []
        for g in range(N_KV):
            first = fs[:, g * 2 * hid:g * 2 * hid + hid]
            second = fs[:, g * 2 * hid + hid:(g + 1) * 2 * hid]
            nxt = pltpu.roll(second, second.shape[0] - 1, 0)
            hs.append(jax.nn.silu(first + nxt + b1_ref[c:c + 1, :]))
        outs.append(_dot(jnp.concatenate(hs, axis=1).astype(BF16), w2_ref[c]))
    return jnp.concatenate(outs, axis=1)


def _compress_prompt_kernel(k_ref, v_ref, w1k_ref, w1v_ref, b1_ref, w2_ref, kck_ref, kcv_ref, kc_scr):
    nb = kcv_ref.shape[1]
    fs_k, fs_v = _compress_partial(lambda s: k_ref[0, pl.ds(s, nb, stride=CMP_STRIDE), :],
                                   lambda s: v_ref[0, pl.ds(s, nb, stride=CMP_STRIDE), :], w1k_ref, w1v_ref)
    kc_scr[...] = _compress_finish(fs_k, fs_v, b1_ref, w2_ref)
    kck_ref[0] = kc_scr[:, :K_LANES].T.astype(BF16)
    kcv_ref[0] = kc_scr[:, K_LANES:].astype(BF16)


def _compress_prompt(kvc, w):
    b, t, _ = kvc.shape
    nb = t // CMP_STRIDE
    weights = [w["w1k"], w["w1v"], w["b_c1"], w["w2"]]
    return pl.pallas_call(
        _compress_prompt_kernel, grid=(b,),
        in_specs=[pl.BlockSpec((1, t, K_LANES), lambda i: (i, 0, 0)), pl.BlockSpec((1, t, K_LANES), lambda i: (i, 0, 1))]
        + [_full(a.shape) for a in weights],
        out_specs=[pl.BlockSpec((1, K_LANES, nb), lambda i: (i, 0, 0)), pl.BlockSpec((1, nb, K_LANES), lambda i: (i, 0, 0))],
        out_shape=[jax.ShapeDtypeStruct((b, K_LANES, nb), BF16), jax.ShapeDtypeStruct((b, nb, K_LANES), BF16)],
        scratch_shapes=[pltpu.VMEM((nb, KV_LANES), F32)],
        compiler_params=_params("parallel"), name="compress_prompt",
    )(kvc, kvc, *weights)


def _stack_heads(q, g, q_per_kv):
    return jnp.concatenate([q[:, (g * q_per_kv + h) * HEAD_DIM:(g * q_per_kv + h + 1) * HEAD_DIM]
                            for h in range(q_per_kv)], axis=0)


def _select_blocks(score, n_sel):
    rows, n = score.shape
    lane = lax.broadcasted_iota(jnp.int32, (rows, n), 1).astype(F32)
    sel = jnp.zeros((rows, n), F32)
    picks = []
    x = score
    for _ in range(n_sel):
        m = jnp.max(x, axis=-1, keepdims=True)
        idx = jnp.min(jnp.where(x == m, lane, float(n)), axis=-1, keepdims=True)
        hit = lane == idx
        ok = m > 0.1 * MASKED
        sel = jnp.where(hit & ok, 1.0, sel)
        x = jnp.where(hit, -3e38, x)
        picks.append((idx, ok))
    return sel, picks


def _select_mask_t(xt, n_sel):
    n, cols = xt.shape
    tiles = [xt[t * SUBLANES:(t + 1) * SUBLANES] for t in range(n // SUBLANES)]
    row = lax.broadcasted_iota(jnp.int32, (SUBLANES, cols), 0)
    ahead = [jnp.zeros((SUBLANES, cols), F32) for _ in tiles]
    for i in range(n):
        xi = xt[i:i + 1, :]
        for t, x in enumerate(tiles):
            first, last = t * SUBLANES, (t + 1) * SUBLANES - 1
            if first > i:
                inc = jnp.where(xi >= x, 1.0, 0.0)
            elif last <= i:
                inc = jnp.where(xi > x, 1.0, 0.0)
            else:
                inc = jnp.where(row + first > i, jnp.where(xi >= x, 1.0, 0.0), jnp.where(xi > x, 1.0, 0.0))
            ahead[t] = ahead[t] + inc
    ahead = jnp.concatenate(ahead, axis=0)
    return jnp.where((ahead < n_sel) & (xt > 0.1 * MASKED), 1.0, 0.0)


def _block_scores(p_slc, blk_t, n_blocks, axis=1):
    sj = lax.broadcasted_iota(jnp.int32, p_slc.shape, axis)
    causal = (sj <= blk_t) & (sj < n_blocks)
    forced = causal & ((sj == 0) | (sj >= blk_t - 1))
    score = jnp.where(forced, p_slc + FORCE_BONUS, p_slc)
    return jnp.where(causal, score, MASKED)


def _nsa_prompt_kernel(q_ref, gate_ref, kck_ref, kcv_ref, kts_ref, vsa_ref, ktw_ref, vwa_ref, ovl_ref, exp_ref,
                       gsel_ref, gob_ref, out_ref, s_scr, mx_scr, acc_scr, *, n_cmp, n_heads):
    blk = pl.program_id(1)
    start = blk * Q_BLOCK
    q_per_kv = n_heads // N_KV
    q = q_ref[...]
    ncp = kcv_ref.shape[1]
    n_blocks = ovl_ref.shape[0]
    tpos = start + lax.broadcasted_iota(jnp.int32, (Q_BLOCK, 1), 0)
    cn = lax.broadcasted_iota(jnp.int32, (1, ncp), 1)
    cmp_mask = ((cn * CMP_STRIDE + CMP_LEN - 1 <= tpos) & (cn < n_cmp)).astype(F32)
    cmp_mask = jnp.concatenate([cmp_mask] * (N_KV * q_per_kv), axis=0) > 0.5
    grp_lanes = [slice(g * HEAD_DIM, (g + 1) * HEAD_DIM) for g in range(N_KV)]
    qbs = [_stack_heads(q, g, q_per_kv) for g in range(N_KV)]

    def normalised(acc, g):
        return (acc / pltpu.roll(acc, HEAD_DIM, 1))[:, grp_lanes[g]]

    o_cmp, p_slc = [], []
    ovl = ovl_ref[...]
    grp_rows = [slice(g * q_per_kv * Q_BLOCK, (g + 1) * q_per_kv * Q_BLOCK) for g in range(N_KV)]
    s = jnp.concatenate([_dot(qbs[g], kck_ref[0, grp_lanes[g], :]) for g in range(N_KV)], axis=0)
    s = jnp.where(cmp_mask, s, MASKED)
    e = jnp.where(cmp_mask, jnp.exp2(s - jnp.max(s, axis=-1, keepdims=True)), 0.0)
    p_all = e / jnp.maximum(jnp.sum(e, axis=-1, keepdims=True), 1e-30)
    for g in range(N_KV):
        p = p_all[grp_rows[g]]
        o_cmp.append(_dot(p.astype(BF16), kcv_ref[0, :, grp_lanes[g]]))
        p_sum = p[:Q_BLOCK]
        for h in range(1, q_per_kv):
            p_sum = p_sum + p[h * Q_BLOCK:(h + 1) * Q_BLOCK]
        p_hi = p_sum.astype(BF16)
        p_lo = (p_sum - p_hi.astype(F32)).astype(BF16)
        p_slc.append(_dot_nt(ovl, p_hi) + _dot_nt(ovl, p_lo))
    blk_t = (start + lax.broadcasted_iota(jnp.int32, (1, Q_BLOCK), 1)) // SLC_BLOCK
    score = _block_scores(jnp.concatenate(p_slc, axis=1), jnp.concatenate([blk_t] * N_KV, axis=1), n_blocks, axis=0)
    sel = _select_mask_t(score, min(N_SELECT, n_blocks)) * SEL_BONUS
    sel = jnp.concatenate([sel, jnp.zeros((LANES - n_blocks, N_KV * Q_BLOCK), F32)], axis=0)
    q_aug = []
    for g in range(N_KV):
        sel_g = sel[:, g * Q_BLOCK:(g + 1) * Q_BLOCK].T[:, :n_blocks]
        q_aug.append(jnp.concatenate([qbs[g], jnp.concatenate([sel_g] * q_per_kv, axis=0).astype(BF16)], axis=1))

    ws = pl.multiple_of(jnp.maximum(start - WINDOW, 0), LANES)
    dpos = tpos - (ws + lax.broadcasted_iota(jnp.int32, (1, WIN_KEYS), 1))
    win_bias = jnp.where((dpos >= 0) & (dpos < WINDOW), 0.0, MASKED)
    win_bias = jnp.concatenate([win_bias] * (N_KV * q_per_kv), axis=0)
    s = jnp.concatenate([_dot(qbs[g], ktw_ref[0, grp_lanes[g], pl.ds(ws, WIN_KEYS)]) for g in range(N_KV)], axis=0)
    s = s + win_bias
    p_all = jnp.exp2(s - jnp.max(s, axis=-1, keepdims=True)).astype(BF16)
    o_win = [normalised(_dot(p_all[grp_rows[g]], vwa_ref[pl.ds(ws, WIN_KEYS), g * K_LANES:(g + 1) * K_LANES]), g)
             for g in range(N_KV)]

    last = (start + Q_BLOCK - 1) // SLC_TILE
    lane_tiles = SLC_TILE // LANES
    mx_scr[...] = jnp.full(mx_scr.shape, MASKED, F32)
    acc_scr[...] = jnp.zeros(acc_scr.shape, F32)

    def score_tile(kt, masked):
        off = pl.multiple_of(kt * SLC_TILE, SLC_TILE)
        for g in range(N_KV):
            keys = jnp.concatenate([kts_ref[0, grp_lanes[g], pl.ds(off, SLC_TILE)],
                                    exp_ref[:, pl.ds(off, SLC_TILE)]], axis=0)
            s = _dot(q_aug[g], keys)
            if masked:
                row = lax.broadcasted_iota(jnp.int32, (q_per_kv * Q_BLOCK, 1), 0) % Q_BLOCK
                s = jnp.where(off + lax.broadcasted_iota(jnp.int32, (1, SLC_TILE), 1) <= start + row, s, MASKED)
            s_scr[g, :, pl.ds(off, SLC_TILE)] = s
            m = s[:, :LANES]
            for i in range(1, lane_tiles):
                m = jnp.maximum(m, s[:, i * LANES:(i + 1) * LANES])
            mx_scr[g] = jnp.maximum(mx_scr[g], m)

    def pair_loop(n, tile):
        def trip(i, carry):
            for k in range(TILES_PER_TRIP):
                tile(TILES_PER_TRIP * i + k)
            return carry

        lax.fori_loop(0, n // TILES_PER_TRIP, trip, 0)
        done = n // TILES_PER_TRIP * TILES_PER_TRIP
        size = TILES_PER_TRIP // 2
        while size:
            @pl.when((n - done) & size != 0)
            def _(done=done, size=size):
                for k in range(size):
                    tile(done + k)

            done = done + ((n - done) & size)
            size //= 2

    pair_loop(last, lambda kt: score_tile(kt, False))
    score_tile(last, True)
    for g in range(N_KV):
        mx_scr[g] = jnp.broadcast_to(jnp.max(mx_scr[g], axis=-1, keepdims=True), mx_scr.shape[1:])

    def value_tile(kt):
        off = pl.multiple_of(kt * SLC_TILE, SLC_TILE)
        for g in range(N_KV):
            p = jnp.exp2(s_scr[g, :, pl.ds(off, SLC_TILE)] - jnp.concatenate([mx_scr[g]] * lane_tiles, axis=1))
            acc_scr[g] += _dot(p.astype(BF16), vsa_ref[pl.ds(off, SLC_TILE), g * K_LANES:(g + 1) * K_LANES])

    pair_loop(last + 1, value_tile)
    o_slc = [normalised(acc_scr[g], g) for g in range(N_KV)]

    gates = gate_ref[...]
    g_hi = gates.astype(BF16)
    g_lo = (gates - g_hi.astype(F32)).astype(BF16)
    out = None
    for j, branch in enumerate((o_cmp, o_slc, o_win)):
        spread = _dot(g_hi, gsel_ref[j]) + _dot(g_lo, gsel_ref[j])
        o = jnp.concatenate([branch[g][h * Q_BLOCK:(h + 1) * Q_BLOCK] for g in range(N_KV) for h in range(q_per_kv)],
                            axis=1)
        out = spread * o if out is None else out + spread * o
    out_ref[...] = _rms(out, gob_ref[...]).astype(BF16)


def _nsa_prompt(q, gates, kck_t, kcv, kts, vsa, ktw, vwa, w, b, t):
    n, b_width = q.shape
    n_heads = b_width // HEAD_DIM
    assert t % SLC_TILE == 0 and t >= WIN_KEYS
    nqb = t // Q_BLOCK
    ncp = kcv.shape[1]
    n_cmp = ncp - 1
    n_blocks = (n_cmp + 1) * CMP_STRIDE // SLC_BLOCK
    assert n_blocks % SUBLANES == 0 and n_blocks <= LANES
    ovl = _overlap_matrix(ncp, n_blocks, n_blocks).T.astype(BF16)
    key_blk = jnp.arange(t, dtype=jnp.int32)[None, :] // SLC_BLOCK
    expand = (key_blk == jnp.arange(n_blocks, dtype=jnp.int32)[:, None]).astype(BF16)
    col = jnp.arange(LANES, dtype=jnp.int32)[None, :, None]
    head = jnp.arange(b_width, dtype=jnp.int32)[None, None, :] // HEAD_DIM
    gate_sel = (col == head * 3 + jnp.arange(3, dtype=jnp.int32)[:, None, None]).astype(BF16)
    tok = lambda i, j: (i * nqb + j, 0)
    seq3 = lambda i, j: (i, 0, 0)
    seq2 = lambda i, j: (i, 0)
    rows = (n_heads // N_KV) * Q_BLOCK
    return pl.pallas_call(
        functools.partial(_nsa_prompt_kernel, n_cmp=n_cmp, n_heads=n_heads),
        grid=(b, nqb),
        in_specs=[pl.BlockSpec((Q_BLOCK, b_width), tok), pl.BlockSpec((Q_BLOCK, LANES), tok),
                  pl.BlockSpec((1, K_LANES, ncp), seq3), pl.BlockSpec((1, ncp, K_LANES), seq3),
                  pl.BlockSpec((1, K_LANES, t), seq3), pl.BlockSpec((t, N_KV * K_LANES), seq2),
                  pl.BlockSpec((1, K_LANES, t), seq3), pl.BlockSpec((t, N_KV * K_LANES), seq2),
                  _full(ovl.shape), _full(expand.shape), _full(gate_sel.shape), _full(w["g_out_b"].shape)],
        out_specs=pl.BlockSpec((Q_BLOCK, b_width), tok),
        out_shape=jax.ShapeDtypeStruct((n, b_width), BF16),
        scratch_shapes=[pltpu.VMEM((N_KV, rows, t), F32), pltpu.VMEM((N_KV, rows, LANES), F32),
                        pltpu.VMEM((N_KV, rows, K_LANES), F32)],
        compiler_params=_params("parallel", "arbitrary"), name="nsa_prompt",
    )(q, gates, kck_t, kcv, kts, vsa, ktw, vwa, ovl, expand, gate_sel, w["g_out_b"])


def _overlap_matrix(rows, cols, n_blocks):
    ci = jnp.arange(rows, dtype=jnp.int32)[:, None]
    sj = jnp.arange(cols, dtype=jnp.int32)[None, :]
    hit = (ci * CMP_STRIDE < (sj + 1) * SLC_BLOCK) & (ci * CMP_STRIDE + CMP_LEN > sj * SLC_BLOCK) & (sj < n_blocks)
    return hit.astype(F32)


def _compress_sample_kernel(pt_ref, cache_ref, new_ref, w1k_ref, w1v_ref, b1_ref, w2_ref, kc_ref,
                            fsk_scr, fsv_scr, buf, sem, *chunk_scr, n_pages, chunk_pages):
    seq = pl.program_id(0)
    n_chunks = n_pages // chunk_pages
    blocks_per_page = PAGE_SIZE // CMP_STRIDE
    chunk_blocks = chunk_pages * blocks_per_page

    def page_copy(s, c, i):
        return pltpu.make_async_copy(cache_ref.at[pt_ref[s, c * chunk_pages + i]], buf.at[c % 2, i], sem.at[c % 2])

    def start_chunk(s, c):
        for i in range(chunk_pages):
            page_copy(s, c, i).start()

    def wait_chunk(c):
        for i in range(chunk_pages):
            page_copy(seq, c, i).wait()

    def scratch(c):
        return chunk_scr[2 * (c % 2)], chunk_scr[2 * (c % 2) + 1]

    def transpose_chunk(c):
        xk_scr, xv_scr = scratch(c)
        for i in range(chunk_pages):
            xk_scr[pl.ds(i * PAGE_SIZE, PAGE_SIZE), :] = buf[c % 2, i, :K_LANES, :].T
            xv_scr[pl.ds(i * PAGE_SIZE, PAGE_SIZE), :] = buf[c % 2, i, K_LANES:, :].T

    def project_chunk(c):
        xk_scr, xv_scr = scratch(c)
        fs_k, fs_v = _compress_partial(lambda s: xk_scr[pl.ds(s, chunk_blocks, stride=CMP_STRIDE), :],
                                       lambda s: xv_scr[pl.ds(s, chunk_blocks, stride=CMP_STRIDE), :],
                                       w1k_ref, w1v_ref)
        fsk_scr[pl.ds(c * chunk_blocks, chunk_blocks), :] = fs_k
        fsv_scr[pl.ds(c * chunk_blocks, chunk_blocks), :] = fs_v

    @pl.when(seq == 0)
    def _():
        start_chunk(seq, 0)
        start_chunk(seq, 1)

    for c in range(n_chunks):
        wait_chunk(c)
        transpose_chunk(c)
        if c + 2 < n_chunks:
            start_chunk(seq, c + 2)
        if c >= 1:
            project_chunk(c - 1)

    @pl.when(seq + 1 < pl.num_programs(0))
    def _():
        start_chunk(seq + 1, 0)
        start_chunk(seq + 1, 1)

    project_chunk(n_chunks - 1)
    past_blocks = n_chunks * chunk_blocks
    tail = fsk_scr.shape[0] - past_blocks
    new = new_ref[0]
    is_first = lax.broadcasted_iota(jnp.int32, (tail, 1), 0) == 0
    nk = _dot(new[:, :K_LANES].astype(BF16), w1k_ref[0, :K_LANES, :])
    nv = _dot(new[:, K_LANES:].astype(BF16), w1v_ref[0, :K_LANES, :])
    fsk_scr[pl.ds(past_blocks, tail), :] = jnp.where(is_first, nk, 0.0)
    fsv_scr[pl.ds(past_blocks, tail), :] = jnp.where(is_first, nv, 0.0)
    kc_ref[0] = _compress_finish(fsk_scr[...], fsv_scr[...], b1_ref, w2_ref)


def _compress_sample(cache_cmp_t, page_table, new_rows, w):
    db, n_pages = page_table.shape
    chunk_pages = CMP_CHUNK_PAGES
    assert n_pages % chunk_pages == 0 and n_pages // chunk_pages >= 2
    blocks_per_page = PAGE_SIZE // CMP_STRIDE
    past_blocks = n_pages * blocks_per_page
    nbp = past_blocks + SUBLANES
    weights = [w["w1k"], w["w1v"], w["b_c1"], w["w2"]]
    hid2 = w["w1k"].shape[2]
    chunk_rows = chunk_pages * PAGE_SIZE
    grid_spec = pltpu.PrefetchScalarGridSpec(
        num_scalar_prefetch=1, grid=(db,),
        in_specs=[pl.BlockSpec(memory_space=pl.ANY), pl.BlockSpec((1, 1, KV_LANES), lambda b, pt: (b, 0, 0))]
        + [pl.BlockSpec(a.shape, lambda b, pt, nd=a.ndim: (0,) * nd) for a in weights],
        out_specs=pl.BlockSpec((1, nbp, KV_LANES), lambda b, pt: (b, 0, 0)),
        scratch_shapes=[pltpu.VMEM((nbp, hid2), F32), pltpu.VMEM((nbp, hid2), F32),
                        pltpu.VMEM((2, chunk_pages, KV_LANES, PAGE_SIZE), F32), pltpu.SemaphoreType.DMA((2,))]
        + [pltpu.VMEM((chunk_rows, K_LANES), F32)] * 4)
    return pl.pallas_call(
        functools.partial(_compress_sample_kernel, n_pages=n_pages, chunk_pages=chunk_pages),
        grid_spec=grid_spec, out_shape=jax.ShapeDtypeStruct((db, nbp, KV_LANES), F32),
        compiler_params=_params("arbitrary"), name="compress_sample",
    )(page_table, cache_cmp_t, new_rows[:, None, :], *weights)


def _select_sample_kernel(q_ref, kc_ref, ocmp_ref, psum_ref, *, pos, n_cmp, n_heads):
    q_per_kv = n_heads // N_KV
    q = q_ref[0] * (LOG2_E * HEAD_DIM ** -0.5)
    kc = kc_ref[0]
    ncp = kc.shape[0]
    cn = lax.broadcasted_iota(jnp.int32, (1, ncp), 1)
    mask = (cn * CMP_STRIDE + CMP_LEN - 1 <= pos) & (cn < n_cmp)
    o_rows, p_rows = [], []
    for g in range(N_KV):
        qg = _stack_heads(q, g, q_per_kv).astype(BF16)
        s = jnp.where(mask, _dot_nt(qg, kc[:, g * HEAD_DIM:(g + 1) * HEAD_DIM].astype(BF16)), MASKED)
        e = jnp.where(mask, jnp.exp2(s - jnp.max(s, axis=-1, keepdims=True)), 0.0)
        p = e / jnp.maximum(jnp.sum(e, axis=-1, keepdims=True), 1e-30)
        o_rows.append(_dot(p.astype(BF16), kc[:, K_LANES + g * HEAD_DIM:K_LANES + (g + 1) * HEAD_DIM].astype(BF16)))
        p_rows.append(jnp.sum(p, axis=0, keepdims=True))
    ocmp_ref[0] = jnp.concatenate(o_rows, axis=0)
    psum_ref[0] = jnp.concatenate(p_rows, axis=0)


def _pick_sample_kernel(psum_ref, ovl_ref, idx_ref, *, pos, n_blocks):
    p = psum_ref[...]
    p_hi = p.astype(BF16)
    p_lo = (p - p_hi.astype(F32)).astype(BF16)
    ovl = ovl_ref[...]
    score = _block_scores(_dot(p_hi, ovl) + _dot(p_lo, ovl), pos // SLC_BLOCK, n_blocks)
    _, picks = _select_blocks(score, min(N_SELECT, n_blocks))
    lane = lax.broadcasted_iota(jnp.int32, idx_ref.shape, 1)
    out = jnp.full(idx_ref.shape, -1.0, F32)
    for i, (idx, ok) in enumerate(picks):
        out = jnp.where((lane == i) & ok, idx, out)
    idx_ref[...] = out.astype(jnp.int32)


def _select_sample(q, kc, pos, n_cmp, n_blocks):
    db, b_width = q.shape
    n_heads = b_width // HEAD_DIM
    ncp = kc.shape[1]
    o_cmp, p_sum = pl.pallas_call(
        functools.partial(_select_sample_kernel, pos=pos, n_cmp=n_cmp, n_heads=n_heads),
        grid=(db,),
        in_specs=[pl.BlockSpec((1, 1, b_width), lambda i: (i, 0, 0)),
                  pl.BlockSpec((1, ncp, KV_LANES), lambda i: (i, 0, 0))],
        out_specs=[pl.BlockSpec((1, n_heads, HEAD_DIM), lambda i: (i, 0, 0)),
                   pl.BlockSpec((1, N_KV, ncp), lambda i: (i, 0, 0))],
        out_shape=[jax.ShapeDtypeStruct((db, n_heads, HEAD_DIM), F32),
                   jax.ShapeDtypeStruct((db, N_KV, ncp), F32)],
        compiler_params=_params("parallel"), name="select_sample",
    )(q[:, None, :], kc)
    nsp = -(-n_blocks // LANES) * LANES
    ovl = _overlap_matrix(ncp, nsp, n_blocks).astype(BF16)
    idx = pl.pallas_call(
        functools.partial(_pick_sample_kernel, pos=pos, n_blocks=n_blocks),
        grid=(1,),
        in_specs=[_full((db * N_KV, ncp)), _full(ovl.shape)],
        out_specs=_full((db * N_KV, LANES)),
        out_shape=jax.ShapeDtypeStruct((db * N_KV, LANES), jnp.int32),
        compiler_params=_params("arbitrary"), name="pick_sample",
    )(p_sum.reshape(db * N_KV, ncp), ovl)
    return o_cmp, idx.reshape(db, N_KV, LANES)


def _attend_sample_kernel(page_ref, *refs, n_sel, past_blocks, n_heads):
    del page_ref
    n_slots = N_KV * n_sel
    pages = refs[:n_slots]
    (q_ref, gate_ref, ocmp_ref, idx_ref, newslc_ref, win_ref, newwin_ref, exp_ref, gob_ref,
     out_ref, winout_ref) = refs[n_slots:]
    q_per_kv = n_heads // N_KV
    q = q_ref[0] * (HEAD_DIM ** -0.5)
    lane = lax.broadcasted_iota(jnp.int32, (1, K_LANES), 1)
    wb = win_ref.shape[2]
    is_last = lax.broadcasted_iota(jnp.int32, (1, wb), 1) == wb - 1
    win = jnp.where(is_last, newwin_ref[0], pltpu.roll(win_ref[0], wb - 1, 1))
    winout_ref[0] = win
    win_k = win[:K_LANES].astype(BF16)
    win_v = win[K_LANES:].astype(BF16)
    new_slc = newslc_ref[0]
    idx = idx_ref[0].astype(F32)
    n_keys = n_sel * PAGE_SIZE
    key_half = (lax.broadcasted_iota(jnp.int32, (1, n_keys), 1) % PAGE_SIZE) // SLC_BLOCK
    o_slc, o_win = [], []
    for g in range(N_KV):
        qpad = jnp.concatenate(
            [jnp.where(lane // HEAD_DIM == g,
                       jnp.concatenate([q[:, (g * q_per_kv + h) * HEAD_DIM:(g * q_per_kv + h + 1) * HEAD_DIM]] * N_KV,
                                       axis=1), 0.0)
             for h in range(q_per_kv)], axis=0)
        qpb = qpad.astype(BF16)
        kt = jnp.concatenate([pages[g * n_sel + i][0, :K_LANES, :] for i in range(n_sel)], axis=1).astype(BF16)
        vt = jnp.concatenate([pages[g * n_sel + i][0, K_LANES:, :] for i in range(n_sel)], axis=1).astype(BF16)
        s = _dot(qpb, kt)
        idg = idx[g:g + 1, :]
        idk = jnp.dot(idg, exp_ref[...], precision=lax.Precision.HIGHEST, preferred_element_type=F32)
        parity = idk - 2.0 * jnp.floor(idk * 0.5)
        key_ok = (idk >= 0.0) & (idk < past_blocks) & (parity == key_half.astype(F32))
        s = jnp.where(key_ok, s, MASKED)
        has_new = jnp.max(jnp.where(idg == past_blocks, 1.0, 0.0), axis=-1, keepdims=True) > 0.5
        s_new = jnp.sum(qpad * new_slc[:, :K_LANES], axis=-1, keepdims=True)
        s_new = jnp.where(has_new, s_new, MASKED)
        m = jnp.maximum(jnp.max(s, axis=-1, keepdims=True), s_new)
        e = jnp.where(key_ok, jnp.exp(s - m), 0.0)
        e_new = jnp.where(has_new, jnp.exp(s_new - m), 0.0)
        den = jnp.maximum(jnp.sum(e, axis=-1, keepdims=True) + e_new, 1e-30)
        o = (_dot_nt(e.astype(BF16), vt) + e_new * new_slc[:, K_LANES:]) / den
        o_slc.append(o[:, g * HEAD_DIM:(g + 1) * HEAD_DIM])
        s = _dot(qpb, win_k)
        e = jnp.exp(s - jnp.max(s, axis=-1, keepdims=True))
        o = _dot_nt(e.astype(BF16), win_v) / jnp.sum(e, axis=-1, keepdims=True)
        o_win.append(o[:, g * HEAD_DIM:(g + 1) * HEAD_DIM])
    gates = gate_ref[0]
    o = (gates[:, 0:1] * ocmp_ref[0] + gates[:, 1:2] * jnp.concatenate(o_slc, axis=0)
         + gates[:, 2:3] * jnp.concatenate(o_win, axis=0))
    ms = jnp.sum(jnp.sum(o * o, axis=-1, keepdims=True), axis=0, keepdims=True) / (n_heads * HEAD_DIM)
    out_ref[0] = o * lax.rsqrt(ms + NORM_EPS) * gob_ref[...]


def _attend_sample(q, gates, o_cmp, idx, cache_slc_t, page_table, new_slc, win_t, new_win, w, past_blocks):
    db, b_width = q.shape
    n_heads = b_width // HEAD_DIM
    n_sel = min(N_SELECT, past_blocks + 1)
    sub_per_page = PAGE_SIZE // SLC_BLOCK
    jp = jnp.clip(idx[:, :, :n_sel], 0, past_blocks - 1)
    page = jnp.take_along_axis(page_table, (jp // sub_per_page).reshape(db, -1), axis=1).astype(jnp.int32)
    n_keys = n_sel * PAGE_SIZE
    expand = (jnp.arange(n_keys, dtype=jnp.int32)[None, :] // PAGE_SIZE
              == jnp.arange(LANES, dtype=jnp.int32)[:, None]).astype(F32)
    gob = w["g_out_b"].reshape(n_heads, HEAD_DIM)
    wb = win_t.shape[2]
    per_seq = lambda shape: pl.BlockSpec((1,) + shape, lambda b, pg: (b, 0, 0))
    page_spec = lambda i: pl.BlockSpec((1, KV_LANES, PAGE_SIZE), lambda b, pg, i=i: (pg[b, i], 0, 0))
    grid_spec = pltpu.PrefetchScalarGridSpec(
        num_scalar_prefetch=1, grid=(db,),
        in_specs=[page_spec(i) for i in range(N_KV * n_sel)]
        + [per_seq((1, b_width)), per_seq((n_heads, 3)), per_seq((n_heads, HEAD_DIM)), per_seq((N_KV, LANES)),
           per_seq((1, KV_LANES)), per_seq((KV_LANES, wb)), per_seq((KV_LANES, 1)),
           pl.BlockSpec(expand.shape, lambda b, pg: (0, 0)), pl.BlockSpec(gob.shape, lambda b, pg: (0, 0))],
        out_specs=[per_seq((n_heads, HEAD_DIM)), per_seq((KV_LANES, wb))])
    return pl.pallas_call(
        functools.partial(_attend_sample_kernel, n_sel=n_sel, past_blocks=past_blocks, n_heads=n_heads),
        grid_spec=grid_spec,
        out_shape=[jax.ShapeDtypeStruct((db, n_heads, HEAD_DIM), F32),
                   jax.ShapeDtypeStruct((db, KV_LANES, wb), F32)],
        compiler_params=_params("parallel"), name="attend_sample",
    )(page, *([cache_slc_t] * (N_KV * n_sel)), q[:, None, :], gates[:, :n_heads * 3].reshape(db, n_heads, 3), o_cmp,
      idx, new_slc[:, None, :], win_t, new_win[:, :, None], expand, gob)


def _ffn_kernel(x_ref, ma_ref, mb_ref, woa_ref, wob_ref, gffn_ref, wgate_ref, wup_ref, wdown_ref, gfin_ref, y_ref,
                acc_scr, xn_scr):
    c = pl.program_id(1)

    @pl.when(c == 0)
    def _():
        x = x_ref[...] + _dot(ma_ref[...], woa_ref[...]) + _dot(mb_ref[...], wob_ref[...])
        acc_scr[...] = x
        xn_scr[...] = _rms(x, gffn_ref[...]).astype(BF16)

    xn = xn_scr[...]
    hid = jax.nn.silu(_dot(xn, wgate_ref[...])) * _dot(xn, wup_ref[...])
    acc_scr[...] += _dot(hid.astype(BF16), wdown_ref[...])

    @pl.when(c == pl.num_programs(1) - 1)
    def _():
        y_ref[...] = _rms(acc_scr[...], gfin_ref[...])


def _output_ffn(x2, mix_a, mix_b, w, tm):
    n, d = x2.shape
    d_ff = w["w_down"].shape[0]
    ff_chunks = FFN_CHUNKS
    step = d_ff // ff_chunks
    assert n % tm == 0 and d_ff % ff_chunks == 0 and step % LANES == 0
    row = lambda i, c: (i, 0)
    fixed = lambda a: pl.BlockSpec(a.shape, lambda i, c: (0, 0))
    return pl.pallas_call(
        _ffn_kernel,
        grid=(n // tm, ff_chunks),
        in_specs=[pl.BlockSpec((tm, d), row), pl.BlockSpec((tm, mix_a.shape[1]), row),
                  pl.BlockSpec((tm, mix_b.shape[1]), row), fixed(w["wo_a"]), fixed(w["wo_b"]), fixed(w["g_ffn"]),
                  pl.BlockSpec((d, step), lambda i, c: (0, c)), pl.BlockSpec((d, step), lambda i, c: (0, c)),
                  pl.BlockSpec((step, d), lambda i, c: (c, 0)), fixed(w["g_final"])],
        out_specs=pl.BlockSpec((tm, d), row),
        out_shape=jax.ShapeDtypeStruct((n, d), F32),
        scratch_shapes=[pltpu.VMEM((tm, d), F32), pltpu.VMEM((tm, d), BF16)],
        compiler_params=_params("parallel", "arbitrary"), name="output_ffn",
    )(x2, mix_a, mix_b, w["wo_a"], w["wo_b"], w["g_ffn"], w["w_gate"], w["w_up"], w["w_down"], w["g_final"])


def _prepare_weights(l, g_attn, w_in, g_sgu, w_s, b_s, w_c1, b_c1, w_c2, g_out_a, g_out_b, w_out,
                     g_ffn, w_gate_up, w_down, g_final):
    a_width = g_sgu.shape[1]
    b_width = g_out_b.shape[1]
    n_heads = b_width // HEAD_DIM
    d_ff = w_down.shape[1]
    hid = b_c1.shape[2]
    o1 = 2 * a_width
    o2 = o1 + b_width
    o3 = o2 + 3 * KV_LANES
    wi = w_in[l]
    wg = jnp.pad(wi[:, o3:], ((0, 0), (0, LANES - 3 * n_heads)))
    row = lambda a: a.reshape(1, -1)
    w1 = w_c1[l].reshape(2, 2, CMP_STRIDE, HEAD_DIM, hid)
    w1 = jnp.transpose(w1, (0, 2, 3, 1, 4)).reshape(2, CMP_STRIDE, HEAD_DIM, 2 * hid)
    eye = jnp.eye(N_KV, dtype=F32)
    w1 = jnp.einsum("gh,csdn->csgdhn", eye, w1).reshape(2, CMP_STRIDE // CMP_STACK, CMP_STACK * K_LANES, N_KV * 2 * hid)
    w2 = jnp.einsum("gh,cne->cgnhe", eye, w_c2[l]).reshape(2, N_KV * hid, K_LANES)
    return {
        "g_attn": row(g_attn[l]), "wuv": wi[:, :o1].astype(BF16), "wq": wi[:, o1:o2].astype(BF16),
        "wkv": wi[:, o2:o3].astype(BF16), "wg": wg.astype(BF16),
        "g_sgu": row(g_sgu[l]), "g_out_a": row(g_out_a[l]), "g_out_b": row(g_out_b[l]),
        "w_s": w_s[l], "bs_full": jnp.repeat(b_s[l].T, LANES, axis=1),
        "ws0": row(jnp.repeat(w_s[l][:, 0, 0], LANES)), "bs0": row(jnp.repeat(b_s[l][:, 0], LANES)),
        "w1k": w1[0].astype(BF16), "w1v": w1[1].astype(BF16), "b_c1": b_c1[l], "w2": w2.astype(BF16),
        "wo_a": w_out[l][:a_width].astype(BF16), "wo_b": w_out[l][a_width:].astype(BF16),
        "g_ffn": row(g_ffn[l]), "w_gate": w_gate_up[l][:, :d_ff].astype(BF16),
        "w_up": w_gate_up[l][:, d_ff:].astype(BF16), "w_down": w_down[l].astype(BF16), "g_final": row(g_final),
    }


def kernel(x_prompt, x_sample, cache_cmp_kv, cache_slc_kv, state_win_kv, page_table, g_attn, w_in, g_sgu, w_s, b_s,
           w_c1, b_c1, w_c2, g_out_a, g_out_b, w_out, g_ffn, w_gate_up, w_down, g_final):
    depth = w_in.shape[0]
    b, t, d = x_prompt.shape
    db, t_s, _ = x_sample.shape
    assert depth == 1 and t_s == 1
    n_pages = page_table.shape[1]
    past = n_pages * PAGE_SIZE
    wb = state_win_kv.shape[2]
    assert wb == WINDOW and past % SLC_BLOCK == 0
    l = 0
    w = _prepare_weights(l, g_attn, w_in, g_sgu, w_s, b_s, w_c1, b_c1, w_c2, g_out_a, g_out_b, w_out,
                         g_ffn, w_gate_up, w_down, g_final)
    xp = x_prompt.reshape(b * t, d)
    mix_a, q, kvc, gates, kvc_t, kvs_t, kvw_t, kts, vs, ktw, vw = _in_projection(
        xp, jnp.arange(t, dtype=jnp.int32), w, prompt_shape=(b, t))
    kck_t, kcv = _compress_prompt(kvc.reshape(b, t, KV_LANES), w)
    mix_b = _nsa_prompt(q, gates, kck_t, kcv, kts, vs, ktw, vw, w, b, t)
    y_prompt = _output_ffn(xp, mix_a, mix_b, w, tm=min(FFN_ROWS, b * t)).reshape(b, t, d)

    xs = x_sample.reshape(db, d)
    pos_s = past + jnp.zeros((db,), jnp.int32)
    mix_a_s, q_s, kvc_s, kvs_s, kvw_s, gates_s, v_rows = _in_projection(xs, pos_s, w)
    kc_s = _compress_sample(_feature_major(cache_cmp_kv[l]), page_table, kvc_s, w)
    lp = -(-(past + t_s) // SLC_BLOCK) * SLC_BLOCK
    n_cmp_s = lp // CMP_STRIDE - 1
    n_blocks_s = (n_cmp_s + 1) * CMP_STRIDE // SLC_BLOCK
    o_cmp_s, idx_s = _select_sample(q_s, kc_s, past, n_cmp_s, n_blocks_s)
    mix_b_s, win_new_t = _attend_sample(q_s, gates_s, o_cmp_s, idx_s, _feature_major(cache_slc_kv[l]), page_table,
                                        kvs_s, _feature_major(state_win_kv[l]), kvw_s, w, past // SLC_BLOCK)
    y_sample = _output_ffn(xs, mix_a_s, mix_b_s.reshape(db, -1).astype(BF16), w, tm=db).reshape(db, t_s, d)

    kv_shape = (2, N_KV, HEAD_DIM)
    return (y_prompt, y_sample,
            _row_major(kvc_t)[None], _row_major(kvs_t)[None], _row_major(kvw_t[:, :, t - min(WINDOW, t):])[None],
            kvc_s.reshape(1, db, t_s, *kv_shape), kvs_s.reshape(1, db, t_s, *kv_shape),
            _row_major(win_new_t)[None], v_rows.reshape(1, db, t_s, -1))


def _feature_major(kv):
    n, rows = kv.shape[:2]
    return jnp.transpose(kv, (0, 2, 3, 4, 1)).reshape(n, KV_LANES, rows)


def _row_major(kv_t):
    n, _, rows = kv_t.shape
    return jnp.transpose(kv_t.reshape(n, 2, N_KV, HEAD_DIM, rows), (0, 4, 1, 2, 3))
```

```python
import functools

import jax
import jax.numpy as jnp
from jax import lax
from jax.experimental import pallas as pl
from jax.experimental.pallas import tpu as pltpu

F32 = jnp.float32
BF16 = jnp.bfloat16

A_GROUPS = 4
CHUNK = 128
HEAD_DIM = 64
N_KV = 2
ROT_DIM = HEAD_DIM // 4
ROPE_THETA = 500000.0
CMP_LEN = 32
CMP_STRIDE = 16
SLC_BLOCK = 64
N_SELECT = 16
WINDOW = 512
Q_BLOCK = 128
FORCE_BONUS = 1000.0
PAGE_SIZE = 128
NORM_EPS = 1e-6
MASKED = -1e30
LOG2_E = 1.4426950408889634
SEL_MARGIN = 160.0

LANES = 128
SUBLANES = 8
VMEM_LIMIT_BYTES = 56 * 1024 * 1024

PROJ_ROWS = 512
FFN_ROWS = 512
FFN_CHUNKS = 2
CMP_CHUNK_PAGES = 32

KV_LANES = 2 * N_KV * HEAD_DIM
K_LANES = N_KV * HEAD_DIM
CMP_STACK = 2
SLC_TILE = 512
TILES_PER_TRIP = 4
WIN_KEYS = WINDOW + Q_BLOCK


def _rms(x, g):
    return x * lax.rsqrt(jnp.mean(x * x, axis=-1, keepdims=True) + NORM_EPS) * g


def _dot(a, b):
    return jnp.dot(a, b, preferred_element_type=F32)


def _dot_nt(a, b, precision=None):
    return lax.dot_general(a, b, (((1,), (1,)), ((), ())), precision=precision,
                           preferred_element_type=F32)


def _rope(z, rc, rs1, rs2):
    return z * rc + pltpu.roll(z, LANES - ROT_DIM // 2, 1) * rs1 + pltpu.roll(z, ROT_DIM // 2, 1) * rs2


def _project(x_ref, gattn_ref, wuv_ref, wq_ref, wkv_ref, wg_ref, rc_ref, rs1_ref, rs2_ref):
    xn = _rms(x_ref[...], gattn_ref[...]).astype(BF16)
    rc, rs1, rs2 = rc_ref[...], rs1_ref[...], rs2_ref[...]
    zuv = _dot(xn, wuv_ref[...])
    a_width = zuv.shape[1] // 2
    zq = _dot(xn, wq_ref[...])
    q = jnp.concatenate([_rope(zq[:, i * LANES:(i + 1) * LANES], rc, rs1, rs2)
                         for i in range(zq.shape[1] // LANES)], axis=1)
    zkv = _dot(xn, wkv_ref[...])
    branches = []
    for br in range(3):
        k = _rope(zkv[:, br * KV_LANES:br * KV_LANES + K_LANES], rc, rs1, rs2)
        v = zkv[:, br * KV_LANES + K_LANES:(br + 1) * KV_LANES]
        branches.append((k, v))
    gates = jax.nn.sigmoid(_dot(xn, wg_ref[...]))
    return zuv[:, :a_width], zuv[:, a_width:], q, branches, gates


def _gmlp_norm_v(v, gsgu):
    v = jax.nn.gelu(v)
    return jnp.concatenate([_rms(v[:, g * LANES:(g + 1) * LANES], gsgu[:, g * LANES:(g + 1) * LANES])
                            for g in range(A_GROUPS)], axis=1)


def _inproj_prompt_kernel(x_ref, gattn_ref, wuv_ref, wq_ref, wkv_ref, wg_ref, rc_ref, rs1_ref, rs2_ref,
                          ws_ref, bs_ref, gsgu_ref, goa_ref,
                          mixa_ref, q_ref, kvc_ref, gate_ref, kvct_ref, kvst_ref, kvwt_ref,
                          kts_ref, vs_ref, ktw_ref, vw_ref):
    u, v, q, branches, gates = _project(x_ref, gattn_ref, wuv_ref, wq_ref, wkv_ref, wg_ref,
                                        rc_ref, rs1_ref, rs2_ref)
    tm = u.shape[0]
    u = jax.nn.gelu(u)
    vg = _gmlp_norm_v(v, gsgu_ref[...]).astype(BF16)
    row = lax.broadcasted_iota(jnp.int32, (CHUNK, CHUNK), 0)
    col = lax.broadcasted_iota(jnp.int32, (CHUNK, CHUNK), 1)
    bias = bs_ref[...]
    parts = []
    for g in range(A_GROUPS):
        w = jnp.where(row >= col, ws_ref[g], 0.0).astype(BF16)
        s = jnp.concatenate(
            [_dot(w, vg[c * CHUNK:(c + 1) * CHUNK, g * LANES:(g + 1) * LANES]) for c in range(tm // CHUNK)],
            axis=0)
        s = s + jnp.concatenate([bias[:, g * LANES:(g + 1) * LANES]] * (tm // CHUNK), axis=0)
        parts.append(u[:, g * LANES:(g + 1) * LANES] * s)
    mixa_ref[...] = _rms(jnp.concatenate(parts, axis=1), goa_ref[...]).astype(BF16)
    q_ref[...] = (q * (LOG2_E * HEAD_DIM ** -0.5)).astype(BF16)
    gate_ref[...] = gates
    kvc_ref[...] = jnp.concatenate(branches[0], axis=1)
    kts = []
    for ref, (k, v_) in zip((kvct_ref, kvst_ref, kvwt_ref), branches):
        kt = k.T
        ref[0, :K_LANES, :] = kt
        ref[0, K_LANES:, :] = v_.T
        kts.append(kt)
    own = [lax.broadcasted_iota(jnp.int32, (1, K_LANES), 1) // HEAD_DIM == g for g in range(N_KV)]
    for kt_ref, va_ref, br in ((kts_ref, vs_ref, 1), (ktw_ref, vw_ref, 2)):
        kt_ref[0] = kts[br].astype(BF16)
        va_ref[...] = jnp.concatenate([jnp.where(m, branches[br][1], 1.0) for m in own], axis=1).astype(BF16)


def _inproj_sample_kernel(x_ref, gattn_ref, wuv_ref, wq_ref, wkv_ref, wg_ref, rc_ref, rs1_ref, rs2_ref,
                          ws0_ref, bs0_ref, gsgu_ref, goa_ref,
                          mixa_ref, q_ref, kvc_ref, kvs_ref, kvw_ref, gate_ref, vrow_ref):
    u, v, q, branches, gates = _project(x_ref, gattn_ref, wuv_ref, wq_ref, wkv_ref, wg_ref,
                                        rc_ref, rs1_ref, rs2_ref)
    vg = _gmlp_norm_v(v, gsgu_ref[...])
    o_a = jax.nn.gelu(u) * (vg * ws0_ref[...] + bs0_ref[...])
    mixa_ref[...] = _rms(o_a, goa_ref[...]).astype(BF16)
    vrow_ref[...] = vg
    q_ref[...] = q
    for ref, (k, v_) in zip((kvc_ref, kvs_ref, kvw_ref), branches):
        ref[...] = jnp.concatenate([k, v_], axis=1)
    gate_ref[...] = gates


def _full(shape):
    return pl.BlockSpec(shape, lambda *_: (0,) * len(shape))


def _params(*sem):
    return pltpu.CompilerParams(dimension_semantics=sem, vmem_limit_bytes=VMEM_LIMIT_BYTES)


def _in_projection(x2, pos, w, *, prompt_shape=None):
    n, d = x2.shape
    rc, rs1, rs2 = _rope_tables(pos)
    a_width = w["wuv"].shape[1] // 2
    b_width = w["wq"].shape[1]
    weights = [w["g_attn"], w["wuv"], w["wq"], w["wkv"], w["wg"]]
    wspecs = [_full(a.shape) for a in weights]
    tail = [w["g_sgu"], w["g_out_a"]]
    if prompt_shape is None:
        tm, grid = n, (1,)
        row = lambda i: (i, 0)
        rope_map = row
        gm = [w["ws0"], w["bs0"]]
    else:
        b, t = prompt_shape
        tm = PROJ_ROWS
        assert t % tm == 0 and tm % CHUNK == 0
        tpb = t // tm
        grid = (b * tpb,)
        row = lambda i: (i, 0)
        rope_map = lambda i: (i % tpb, 0)
        gm = [w["w_s"], w["bs_full"]]
    rspec = pl.BlockSpec((tm, LANES), rope_map)
    in_specs = ([pl.BlockSpec((tm, d), row)] + wspecs + [rspec] * 3
                + [_full(a.shape) for a in gm] + [_full(a.shape) for a in tail])
    if prompt_shape is None:
        kern = _inproj_sample_kernel
        outs = [((n, a_width), BF16), ((n, b_width), F32), ((n, KV_LANES), F32), ((n, KV_LANES), F32),
                ((n, KV_LANES), F32), ((n, LANES), F32), ((n, a_width), F32)]
        out_specs = [pl.BlockSpec((tm, s[1]), row) for s, _ in outs]
    else:
        kern = _inproj_prompt_kernel
        kt_map = lambda i: (i // tpb, 0, i % tpb)
        outs = [((n, a_width), BF16), ((n, b_width), BF16), ((n, KV_LANES), F32), ((n, LANES), F32)]
        out_specs = [pl.BlockSpec((tm, s[1]), row) for s, _ in outs]
        outs += [((b, KV_LANES, t), F32)] * 3
        out_specs += [pl.BlockSpec((1, KV_LANES, tm), kt_map)] * 3
        outs += [((b, K_LANES, t), BF16), ((n, N_KV * K_LANES), BF16)] * 2
        out_specs += [pl.BlockSpec((1, K_LANES, tm), kt_map), pl.BlockSpec((tm, N_KV * K_LANES), row)] * 2
    return pl.pallas_call(
        kern, grid=grid, in_specs=in_specs, out_specs=out_specs,
        out_shape=[jax.ShapeDtypeStruct(s, dt) for s, dt in outs],
        compiler_params=_params("parallel"), name="in_projection",
    )(x2, *weights, rc, rs1, rs2, *gm, *tail)


def _rope_tables(pos):
    half = ROT_DIM // 2
    inv = ROPE_THETA ** (-jnp.arange(half, dtype=F32) / half)
    ang = pos.astype(F32)[:, None] * inv[None, :]
    cos, sin = jnp.cos(ang), jnp.sin(ang)
    n = pos.shape[0]
    rest0 = jnp.zeros((n, HEAD_DIM - ROT_DIM), F32)
    zero = jnp.zeros((n, half), F32)
    rc = jnp.concatenate([cos, cos, rest0 + 1.0], axis=1)
    rs1 = jnp.concatenate([-sin, zero, rest0], axis=1)
    rs2 = jnp.concatenate([zero, sin, rest0], axis=1)
    return tuple(jnp.tile(a, (1, LANES // HEAD_DIM)) for a in (rc, rs1, rs2))


def _compress_partial(read_k, read_v, w1k_ref, w1v_ref):
    acc_k = acc_v = None
    stack = w1k_ref.shape[1] // K_LANES
    for i in range(CMP_STRIDE // stack):
        rows = range(i * stack, (i + 1) * stack)
        pk = _dot(jnp.concatenate([read_k(s).astype(BF16) for s in rows], axis=1), w1k_ref[i])
        pv = _dot(jnp.concatenate([read_v(s).astype(BF16) for s in rows], axis=1), w1v_ref[i])
        acc_k = pk if acc_k is None else acc_k + pk
        acc_v = pv if acc_v is None else acc_v + pv
    return acc_k, acc_v


def _compress_finish(fs_k, fs_v, b1_ref, w2_ref):
    hid = b1_ref.shape[1]
    outs = []
    for c, fs in enumerate((fs_k, fs_v)):
        hs = []
        for g in range(N_KV):
            first = fs[:, g * 2 * hid:g * 2 * hid + hid]
            second = fs[:, g * 2 * hid + hid:(g + 1) * 2 * hid]
            nxt = pltpu.roll(second, second.shape[0] - 1, 0)
            hs.append(jax.nn.silu(first + nxt + b1_ref[c:c + 1, :]))
        outs.append(_dot(jnp.concatenate(hs, axis=1).astype(BF16), w2_ref[c]))
    return jnp.concatenate(outs, axis=1)


def _compress_prompt_kernel(k_ref, v_ref, w1k_ref, w1v_ref, b1_ref, w2_ref, kck_ref, kcv_ref, kc_scr):
    nb = kcv_ref.shape[1]
    fs_k, fs_v = _compress_partial(lambda s: k_ref[0, pl.ds(s, nb, stride=CMP_STRIDE), :],
                                   lambda s: v_ref[0, pl.ds(s, nb, stride=CMP_STRIDE), :], w1k_ref, w1v_ref)
    kc_scr[...] = _compress_finish(fs_k, fs_v, b1_ref, w2_ref)
    kck_ref[0] = kc_scr[:, :K_LANES].T.astype(BF16)
    kcv_ref[0] = kc_scr[:, K_LANES:].astype(BF16)


def _compress_prompt(kvc, w):
    b, t, _ = kvc.shape
    nb = t // CMP_STRIDE
    weights = [w["w1k"], w["w1v"], w["b_c1"], w["w2"]]
    return pl.pallas_call(
        _compress_prompt_kernel, grid=(b,),
        in_specs=[pl.BlockSpec((1, t, K_LANES), lambda i: (i, 0, 0)), pl.BlockSpec((1, t, K_LANES), lambda i: (i, 0, 1))]
        + [_full(a.shape) for a in weights],
        out_specs=[pl.BlockSpec((1, K_LANES, nb), lambda i: (i, 0, 0)), pl.BlockSpec((1, nb, K_LANES), lambda i: (i, 0, 0))],
        out_shape=[jax.ShapeDtypeStruct((b, K_LANES, nb), BF16), jax.ShapeDtypeStruct((b, nb, K_LANES), BF16)],
        scratch_shapes=[pltpu.VMEM((nb, KV_LANES), F32)],
        compiler_params=_params("parallel"), name="compress_prompt",
    )(kvc, kvc, *weights)


def _stack_heads(q, g, q_per_kv):
    return jnp.concatenate([q[:, (g * q_per_kv + h) * HEAD_DIM:(g * q_per_kv + h + 1) * HEAD_DIM]
                            for h in range(q_per_kv)], axis=0)


def _select_blocks(score, n_sel):
    rows, n = score.shape
    lane = lax.broadcasted_iota(jnp.int32, (rows, n), 1).astype(F32)
    sel = jnp.zeros((rows, n), F32)
    picks = []
    x = score
    for _ in range(n_sel):
        m = jnp.max(x, axis=-1, keepdims=True)
        idx = jnp.min(jnp.where(x == m, lane, float(n)), axis=-1, keepdims=True)
        hit = lane == idx
        ok = m > 0.1 * MASKED
        sel = jnp.where(hit & ok, 1.0, sel)
        x = jnp.where(hit, -3e38, x)
        picks.append((idx, ok))
    return sel, picks


def _select_mask_t(xt, n_sel):
    n, cols = xt.shape
    tiles = [xt[t * SUBLANES:(t + 1) * SUBLANES] for t in range(n // SUBLANES)]
    row = lax.broadcasted_iota(jnp.int32, (SUBLANES, cols), 0)
    ahead = [jnp.zeros((SUBLANES, cols), F32) for _ in tiles]
    for i in range(n):
        xi = xt[i:i + 1, :]
        for t, x in enumerate(tiles):
            first, last = t * SUBLANES, (t + 1) * SUBLANES - 1
            if first > i:
                inc = jnp.where(xi >= x, 1.0, 0.0)
            elif last <= i:
                inc = jnp.where(xi > x, 1.0, 0.0)
            else:
                inc = jnp.where(row + first > i, jnp.where(xi >= x, 1.0, 0.0), jnp.where(xi > x, 1.0, 0.0))
            ahead[t] = ahead[t] + inc
    ahead = jnp.concatenate(ahead, axis=0)
    return jnp.where((ahead < n_sel) & (xt > 0.1 * MASKED), 1.0, 0.0)


def _block_scores(p_slc, blk_t, n_blocks, axis=1):
    sj = lax.broadcasted_iota(jnp.int32, p_slc.shape, axis)
    causal = (sj <= blk_t) & (sj < n_blocks)
    forced = causal & ((sj == 0) | (sj >= blk_t - 1))
    score = jnp.where(forced, p_slc + FORCE_BONUS, p_slc)
    return jnp.where(causal, score, MASKED)


def _nsa_prompt_kernel(q_ref, gate_ref, kck_ref, kcv_ref, kts_ref, vsa_ref, ktw_ref, vwa_ref, ovl_ref, exp_ref,
                       gsel_ref, gob_ref, out_ref, s_scr, mx_scr, acc_scr, kmax_scr, *, n_cmp, n_heads):
    blk = pl.program_id(1)
    start = blk * Q_BLOCK
    q_per_kv = n_heads // N_KV
    q = q_ref[...]
    ncp = kcv_ref.shape[1]
    n_blocks = ovl_ref.shape[0]
    tpos = start + lax.broadcasted_iota(jnp.int32, (Q_BLOCK, 1), 0)
    cn = lax.broadcasted_iota(jnp.int32, (1, ncp), 1)
    cmp_mask = ((cn * CMP_STRIDE + CMP_LEN - 1 <= tpos) & (cn < n_cmp)).astype(F32)
    cmp_mask = jnp.concatenate([cmp_mask] * (N_KV * q_per_kv), axis=0) > 0.5
    grp_lanes = [slice(g * HEAD_DIM, (g + 1) * HEAD_DIM) for g in range(N_KV)]
    qbs = [_stack_heads(q, g, q_per_kv) for g in range(N_KV)]

    @pl.when(blk == 0)
    def _():
        for g in range(N_KV):
            k = kts_ref[0, grp_lanes[g], :].astype(F32)
            k_sq = jnp.max(jnp.sum(k * k, axis=0, keepdims=True), axis=-1, keepdims=True)
            kmax_scr[g] = jnp.broadcast_to(jnp.sqrt(k_sq), kmax_scr.shape[1:])

    def normalised(acc, g):
        return (acc / pltpu.roll(acc, HEAD_DIM, 1))[:, grp_lanes[g]]

    o_cmp, p_slc = [], []
    ovl = ovl_ref[...]
    grp_rows = [slice(g * q_per_kv * Q_BLOCK, (g + 1) * q_per_kv * Q_BLOCK) for g in range(N_KV)]
    s = jnp.concatenate([_dot(qbs[g], kck_ref[0, grp_lanes[g], :]) for g in range(N_KV)], axis=0)
    s = jnp.where(cmp_mask, s, MASKED)
    e = jnp.where(cmp_mask, jnp.exp2(s - jnp.max(s, axis=-1, keepdims=True)), 0.0)
    p_all = e / jnp.maximum(jnp.sum(e, axis=-1, keepdims=True), 1e-30)
    for g in range(N_KV):
        p = p_all[grp_rows[g]]
        o_cmp.append(_dot(p.astype(BF16), kcv_ref[0, :, grp_lanes[g]]))
        p_sum = p[:Q_BLOCK]
        for h in range(1, q_per_kv):
            p_sum = p_sum + p[h * Q_BLOCK:(h + 1) * Q_BLOCK]
        p_hi = p_sum.astype(BF16)
        p_lo = (p_sum - p_hi.astype(F32)).astype(BF16)
        p_slc.append(_dot_nt(ovl, p_hi) + _dot_nt(ovl, p_lo))
    blk_t = (start + lax.broadcasted_iota(jnp.int32, (1, Q_BLOCK), 1)) // SLC_BLOCK
    score = _block_scores(jnp.concatenate(p_slc, axis=1), jnp.concatenate([blk_t] * N_KV, axis=1), n_blocks, axis=0)
    sel = _select_mask_t(score, min(N_SELECT, n_blocks))
    sel = jnp.concatenate([sel, jnp.zeros((LANES - n_blocks, N_KV * Q_BLOCK), F32)], axis=0)
    q_aug = []
    for g in range(N_KV):
        qf = qbs[g].astype(F32)
        q_max = jnp.sqrt(jnp.max(jnp.sum(qf * qf, axis=-1, keepdims=True), axis=0, keepdims=True))
        bonus = jnp.exp2(jnp.ceil(jnp.log2(2.0 * q_max * kmax_scr[g][:1, :1] + SEL_MARGIN)))
        sel_g = sel[:, g * Q_BLOCK:(g + 1) * Q_BLOCK].T[:, :n_blocks] * bonus
        q_aug.append(jnp.concatenate([qbs[g], jnp.concatenate([sel_g] * q_per_kv, axis=0).astype(BF16)], axis=1))

    ws = pl.multiple_of(jnp.maximum(start - WINDOW, 0), LANES)
    dpos = tpos - (ws + lax.broadcasted_iota(jnp.int32, (1, WIN_KEYS), 1))
    win_bias = jnp.where((dpos >= 0) & (dpos < WINDOW), 0.0, MASKED)
    win_bias = jnp.concatenate([win_bias] * (N_KV * q_per_kv), axis=0)
    s = jnp.concatenate([_dot(qbs[g], ktw_ref[0, grp_lanes[g], pl.ds(ws, WIN_KEYS)]) for g in range(N_KV)], axis=0)
    s = s + win_bias
    p_all = jnp.exp2(s - jnp.max(s, axis=-1, keepdims=True)).astype(BF16)
    o_win = [normalised(_dot(p_all[grp_rows[g]], vwa_ref[pl.ds(ws, WIN_KEYS), g * K_LANES:(g + 1) * K_LANES]), g)
             for g in range(N_KV)]

    last = (start + Q_BLOCK - 1) // SLC_TILE
    lane_tiles = SLC_TILE // LANES
    mx_scr[...] = jnp.full(mx_scr.shape, MASKED, F32)
    acc_scr[...] = jnp.zeros(acc_scr.shape, F32)

    def score_tile(kt, masked):
        off = pl.multiple_of(kt * SLC_TILE, SLC_TILE)
        for g in range(N_KV):
            keys = jnp.concatenate([kts_ref[0, grp_lanes[g], pl.ds(off, SLC_TILE)],
                                    exp_ref[:, pl.ds(off, SLC_TILE)]], axis=0)
            s = _dot(q_aug[g], keys)
            if masked:
                row = lax.broadcasted_iota(jnp.int32, (q_per_kv * Q_BLOCK, 1), 0) % Q_BLOCK
                s = jnp.where(off + lax.broadcasted_iota(jnp.int32, (1, SLC_TILE), 1) <= start + row, s, MASKED)
            s_scr[g, :, pl.ds(off, SLC_TILE)] = s
            m = s[:, :LANES]
            for i in range(1, lane_tiles):
                m = jnp.maximum(m, s[:, i * LANES:(i + 1) * LANES])
            mx_scr[g] = jnp.maximum(mx_scr[g], m)

    def pair_loop(n, tile):
        def trip(i, carry):
            for k in range(TILES_PER_TRIP):
                tile(TILES_PER_TRIP * i + k)
            return carry

        lax.fori_loop(0, n // TILES_PER_TRIP, trip, 0)
        done = n // TILES_PER_TRIP * TILES_PER_TRIP
        size = TILES_PER_TRIP // 2
        while size:
            @pl.when((n - done) & size != 0)
            def _(done=done, size=size):
                for k in range(size):
                    tile(done + k)

            done = done + ((n - done) & size)
            size //= 2

    pair_loop(last, lambda kt: score_tile(kt, False))
    score_tile(last, True)
    for g in range(N_KV):
        mx_scr[g] = jnp.broadcast_to(jnp.max(mx_scr[g], axis=-1, keepdims=True), mx_scr.shape[1:])

    def value_tile(kt):
        off = pl.multiple_of(kt * SLC_TILE, SLC_TILE)
        for g in range(N_KV):
            p = jnp.exp2(s_scr[g, :, pl.ds(off, SLC_TILE)] - jnp.concatenate([mx_scr[g]] * lane_tiles, axis=1))
            acc_scr[g] += _dot(p.astype(BF16), vsa_ref[pl.ds(off, SLC_TILE), g * K_LANES:(g + 1) * K_LANES])

    pair_loop(last + 1, value_tile)
    o_slc = [normalised(acc_scr[g], g) for g in range(N_KV)]

    gates = gate_ref[...]
    g_hi = gates.astype(BF16)
    g_lo = (gates - g_hi.astype(F32)).astype(BF16)
    out = None
    for j, branch in enumerate((o_cmp, o_slc, o_win)):
        spread = _dot(g_hi, gsel_ref[j]) + _dot(g_lo, gsel_ref[j])
        o = jnp.concatenate([branch[g][h * Q_BLOCK:(h + 1) * Q_BLOCK] for g in range(N_KV) for h in range(q_per_kv)],
                            axis=1)
        out = spread * o if out is None else out + spread * o
    out_ref[...] = _rms(out, gob_ref[...]).astype(BF16)


def _nsa_prompt(q, gates, kck_t, kcv, kts, vsa, ktw, vwa, w, b, t):
    n, b_width = q.shape
    n_heads = b_width // HEAD_DIM
    assert t % SLC_TILE == 0 and t >= WIN_KEYS
    nqb = t // Q_BLOCK
    ncp = kcv.shape[1]
    n_cmp = ncp - 1
    n_blocks = (n_cmp + 1) * CMP_STRIDE // SLC_BLOCK
    assert n_blocks % SUBLANES == 0 and n_blocks <= LANES
    ovl = _overlap_matrix(ncp, n_blocks, n_blocks).T.astype(BF16)
    key_blk = jnp.arange(t, dtype=jnp.int32)[None, :] // SLC_BLOCK
    expand = (key_blk == jnp.arange(n_blocks, dtype=jnp.int32)[:, None]).astype(BF16)
    col = jnp.arange(LANES, dtype=jnp.int32)[None, :, None]
    head = jnp.arange(b_width, dtype=jnp.int32)[None, None, :] // HEAD_DIM
    gate_sel = (col == head * 3 + jnp.arange(3, dtype=jnp.int32)[:, None, None]).astype(BF16)
    tok = lambda i, j: (i * nqb + j, 0)
    seq3 = lambda i, j: (i, 0, 0)
    seq2 = lambda i, j: (i, 0)
    rows = (n_heads // N_KV) * Q_BLOCK
    return pl.pallas_call(
        functools.partial(_nsa_prompt_kernel, n_cmp=n_cmp, n_heads=n_heads),
        grid=(b, nqb),
        in_specs=[pl.BlockSpec((Q_BLOCK, b_width), tok), pl.BlockSpec((Q_BLOCK, LANES), tok),
                  pl.BlockSpec((1, K_LANES, ncp), seq3), pl.BlockSpec((1, ncp, K_LANES), seq3),
                  pl.BlockSpec((1, K_LANES, t), seq3), pl.BlockSpec((t, N_KV * K_LANES), seq2),
                  pl.BlockSpec((1, K_LANES, t), seq3), pl.BlockSpec((t, N_KV * K_LANES), seq2),
                  _full(ovl.shape), _full(expand.shape), _full(gate_sel.shape), _full(w["g_out_b"].shape)],
        out_specs=pl.BlockSpec((Q_BLOCK, b_width), tok),
        out_shape=jax.ShapeDtypeStruct((n, b_width), BF16),
        scratch_shapes=[pltpu.VMEM((N_KV, rows, t), F32), pltpu.VMEM((N_KV, rows, LANES), F32),
                        pltpu.VMEM((N_KV, rows, K_LANES), F32), pltpu.VMEM((N_KV, SUBLANES, LANES), F32)],
        compiler_params=_params("parallel", "arbitrary"), name="nsa_prompt",
    )(q, gates, kck_t, kcv, kts, vsa, ktw, vwa, ovl, expand, gate_sel, w["g_out_b"])


def _overlap_matrix(rows, cols, n_blocks):
    ci = jnp.arange(rows, dtype=jnp.int32)[:, None]
    sj = jnp.arange(cols, dtype=jnp.int32)[None, :]
    hit = (ci * CMP_STRIDE < (sj + 1) * SLC_BLOCK) & (ci * CMP_STRIDE + CMP_LEN > sj * SLC_BLOCK) & (sj < n_blocks)
    return hit.astype(F32)


def _compress_sample_kernel(pt_ref, cache_ref, new_ref, w1k_ref, w1v_ref, b1_ref, w2_ref, kc_ref,
                            fsk_scr, fsv_scr, buf, sem, *chunk_scr, n_pages, chunk_pages):
    seq = pl.program_id(0)
    n_chunks = n_pages // chunk_pages
    blocks_per_page = PAGE_SIZE // CMP_STRIDE
    chunk_blocks = chunk_pages * blocks_per_page

    def page_copy(s, c, i):
        return pltpu.make_async_copy(cache_ref.at[pt_ref[s, c * chunk_pages + i]], buf.at[c % 2, i], sem.at[c % 2])

    def start_chunk(s, c):
        for i in range(chunk_pages):
            page_copy(s, c, i).start()

    def wait_chunk(c):
        for i in range(chunk_pages):
            page_copy(seq, c, i).wait()

    def scratch(c):
        return chunk_scr[2 * (c % 2)], chunk_scr[2 * (c % 2) + 1]

    def transpose_chunk(c):
        xk_scr, xv_scr = scratch(c)
        for i in range(chunk_pages):
            xk_scr[pl.ds(i * PAGE_SIZE, PAGE_SIZE), :] = buf[c % 2, i, :K_LANES, :].T
            xv_scr[pl.ds(i * PAGE_SIZE, PAGE_SIZE), :] = buf[c % 2, i, K_LANES:, :].T

    def project_chunk(c):
        xk_scr, xv_scr = scratch(c)
        fs_k, fs_v = _compress_partial(lambda s: xk_scr[pl.ds(s, chunk_blocks, stride=CMP_STRIDE), :],
                                       lambda s: xv_scr[pl.ds(s, chunk_blocks, stride=CMP_STRIDE), :],
                                       w1k_ref, w1v_ref)
        fsk_scr[pl.ds(c * chunk_blocks, chunk_blocks), :] = fs_k
        fsv_scr[pl.ds(c * chunk_blocks, chunk_blocks), :] = fs_v

    @pl.when(seq == 0)
    def _():
        start_chunk(seq, 0)
        start_chunk(seq, 1)

    for c in range(n_chunks):
        wait_chunk(c)
        transpose_chunk(c)
        if c + 2 < n_chunks:
            start_chunk(seq, c + 2)
        if c >= 1:
            project_chunk(c - 1)

    @pl.when(seq + 1 < pl.num_programs(0))
    def _():
        start_chunk(seq + 1, 0)
        start_chunk(seq + 1, 1)

    project_chunk(n_chunks - 1)
    past_blocks = n_chunks * chunk_blocks
    tail = fsk_scr.shape[0] - past_blocks
    new = new_ref[0]
    is_first = lax.broadcasted_iota(jnp.int32, (tail, 1), 0) == 0
    nk = _dot(new[:, :K_LANES].astype(BF16), w1k_ref[0, :K_LANES, :])
    nv = _dot(new[:, K_LANES:].astype(BF16), w1v_ref[0, :K_LANES, :])
    fsk_scr[pl.ds(past_blocks, tail), :] = jnp.where(is_first, nk, 0.0)
    fsv_scr[pl.ds(past_blocks, tail), :] = jnp.where(is_first, nv, 0.0)
    kc_ref[0] = _compress_finish(fsk_scr[...], fsv_scr[...], b1_ref, w2_ref)


def _compress_sample(cache_cmp_t, page_table, new_rows, w):
    db, n_pages = page_table.shape
    chunk_pages = CMP_CHUNK_PAGES
    assert n_pages % chunk_pages == 0 and n_pages // chunk_pages >= 2
    blocks_per_page = PAGE_SIZE // CMP_STRIDE
    past_blocks = n_pages * blocks_per_page
    nbp = past_blocks + SUBLANES
    weights = [w["w1k"], w["w1v"], w["b_c1"], w["w2"]]
    hid2 = w["w1k"].shape[2]
    chunk_rows = chunk_pages * PAGE_SIZE
    grid_spec = pltpu.PrefetchScalarGridSpec(
        num_scalar_prefetch=1, grid=(db,),
        in_specs=[pl.BlockSpec(memory_space=pl.ANY), pl.BlockSpec((1, 1, KV_LANES), lambda b, pt: (b, 0, 0))]
        + [pl.BlockSpec(a.shape, lambda b, pt, nd=a.ndim: (0,) * nd) for a in weights],
        out_specs=pl.BlockSpec((1, nbp, KV_LANES), lambda b, pt: (b, 0, 0)),
        scratch_shapes=[pltpu.VMEM((nbp, hid2), F32), pltpu.VMEM((nbp, hid2), F32),
                        pltpu.VMEM((2, chunk_pages, KV_LANES, PAGE_SIZE), F32), pltpu.SemaphoreType.DMA((2,))]
        + [pltpu.VMEM((chunk_rows, K_LANES), F32)] * 4)
    return pl.pallas_call(
        functools.partial(_compress_sample_kernel, n_pages=n_pages, chunk_pages=chunk_pages),
        grid_spec=grid_spec, out_shape=jax.ShapeDtypeStruct((db, nbp, KV_LANES), F32),
        compiler_params=_params("arbitrary"), name="compress_sample",
    )(page_table, cache_cmp_t, new_rows[:, None, :], *weights)


def _select_sample_kernel(q_ref, kc_ref, ocmp_ref, psum_ref, *, pos, n_cmp, n_heads):
    q_per_kv = n_heads // N_KV
    q = q_ref[0] * (LOG2_E * HEAD_DIM ** -0.5)
    kc = kc_ref[0]
    ncp = kc.shape[0]
    cn = lax.broadcasted_iota(jnp.int32, (1, ncp), 1)
    mask = (cn * CMP_STRIDE + CMP_LEN - 1 <= pos) & (cn < n_cmp)
    o_rows, p_rows = [], []
    for g in range(N_KV):
        qg = _stack_heads(q, g, q_per_kv).astype(BF16)
        s = jnp.where(mask, _dot_nt(qg, kc[:, g * HEAD_DIM:(g + 1) * HEAD_DIM].astype(BF16)), MASKED)
        e = jnp.where(mask, jnp.exp2(s - jnp.max(s, axis=-1, keepdims=True)), 0.0)
        p = e / jnp.maximum(jnp.sum(e, axis=-1, keepdims=True), 1e-30)
        o_rows.append(_dot(p.astype(BF16), kc[:, K_LANES + g * HEAD_DIM:K_LANES + (g + 1) * HEAD_DIM].astype(BF16)))
        p_rows.append(jnp.sum(p, axis=0, keepdims=True))
    ocmp_ref[0] = jnp.concatenate(o_rows, axis=0)
    psum_ref[0] = jnp.concatenate(p_rows, axis=0)


def _pick_sample_kernel(psum_ref, ovl_ref, idx_ref, *, pos, n_blocks):
    p = psum_ref[...]
    p_hi = p.astype(BF16)
    p_lo = (p - p_hi.astype(F32)).astype(BF16)
    ovl = ovl_ref[...]
    score = _block_scores(_dot(p_hi, ovl) + _dot(p_lo, ovl), pos // SLC_BLOCK, n_blocks)
    _, picks = _select_blocks(score, min(N_SELECT, n_blocks))
    lane = lax.broadcasted_iota(jnp.int32, idx_ref.shape, 1)
    out = jnp.full(idx_ref.shape, -1.0, F32)
    for i, (idx, ok) in enumerate(picks):
        out = jnp.where((lane == i) & ok, idx, out)
    idx_ref[...] = out.astype(jnp.int32)


def _select_sample(q, kc, pos, n_cmp, n_blocks):
    db, b_width = q.shape
    n_heads = b_width // HEAD_DIM
    ncp = kc.shape[1]
    o_cmp, p_sum = pl.pallas_call(
        functools.partial(_select_sample_kernel, pos=pos, n_cmp=n_cmp, n_heads=n_heads),
        grid=(db,),
        in_specs=[pl.BlockSpec((1, 1, b_width), lambda i: (i, 0, 0)),
                  pl.BlockSpec((1, ncp, KV_LANES), lambda i: (i, 0, 0))],
        out_specs=[pl.BlockSpec((1, n_heads, HEAD_DIM), lambda i: (i, 0, 0)),
                   pl.BlockSpec((1, N_KV, ncp), lambda i: (i, 0, 0))],
        out_shape=[jax.ShapeDtypeStruct((db, n_heads, HEAD_DIM), F32),
                   jax.ShapeDtypeStruct((db, N_KV, ncp), F32)],
        compiler_params=_params("parallel"), name="select_sample",
    )(q[:, None, :], kc)
    nsp = -(-n_blocks // LANES) * LANES
    ovl = _overlap_matrix(ncp, nsp, n_blocks).astype(BF16)
    idx = pl.pallas_call(
        functools.partial(_pick_sample_kernel, pos=pos, n_blocks=n_blocks),
        grid=(1,),
        in_specs=[_full((db * N_KV, ncp)), _full(ovl.shape)],
        out_specs=_full((db * N_KV, LANES)),
        out_shape=jax.ShapeDtypeStruct((db * N_KV, LANES), jnp.int32),
        compiler_params=_params("arbitrary"), name="pick_sample",
    )(p_sum.reshape(db * N_KV, ncp), ovl)
    return o_cmp, idx.reshape(db, N_KV, LANES)


def _attend_sample_kernel(page_ref, *refs, n_sel, past_blocks, n_heads):
    del page_ref
    n_slots = N_KV * n_sel
    pages = refs[:n_slots]
    (q_ref, gate_ref, ocmp_ref, idx_ref, newslc_ref, win_ref, newwin_ref, exp_ref, gob_ref,
     out_ref, winout_ref) = refs[n_slots:]
    q_per_kv = n_heads // N_KV
    q = q_ref[0] * (HEAD_DIM ** -0.5)
    lane = lax.broadcasted_iota(jnp.int32, (1, K_LANES), 1)
    wb = win_ref.shape[2]
    is_last = lax.broadcasted_iota(jnp.int32, (1, wb), 1) == wb - 1
    win = jnp.where(is_last, newwin_ref[0], pltpu.roll(win_ref[0], wb - 1, 1))
    winout_ref[0] = win
    win_k = win[:K_LANES].astype(BF16)
    win_v = win[K_LANES:].astype(BF16)
    new_slc = newslc_ref[0]
    idx = idx_ref[0].astype(F32)
    n_keys = n_sel * PAGE_SIZE
    key_half = (lax.broadcasted_iota(jnp.int32, (1, n_keys), 1) % PAGE_SIZE) // SLC_BLOCK
    o_slc, o_win = [], []
    for g in range(N_KV):
        qpad = jnp.concatenate(
            [jnp.where(lane // HEAD_DIM == g,
                       jnp.concatenate([q[:, (g * q_per_kv + h) * HEAD_DIM:(g * q_per_kv + h + 1) * HEAD_DIM]] * N_KV,
                                       axis=1), 0.0)
             for h in range(q_per_kv)], axis=0)
        qpb = qpad.astype(BF16)
        kt = jnp.concatenate([pages[g * n_sel + i][0, :K_LANES, :] for i in range(n_sel)], axis=1).astype(BF16)
        vt = jnp.concatenate([pages[g * n_sel + i][0, K_LANES:, :] for i in range(n_sel)], axis=1).astype(BF16)
        s = _dot(qpb, kt)
        idg = idx[g:g + 1, :]
        idk = jnp.dot(idg, exp_ref[...], precision=lax.Precision.HIGHEST, preferred_element_type=F32)
        parity = idk - 2.0 * jnp.floor(idk * 0.5)
        key_ok = (idk >= 0.0) & (idk < past_blocks) & (parity == key_half.astype(F32))
        s = jnp.where(key_ok, s, MASKED)
        has_new = jnp.max(jnp.where(idg == past_blocks, 1.0, 0.0), axis=-1, keepdims=True) > 0.5
        s_new = jnp.sum(qpad * new_slc[:, :K_LANES], axis=-1, keepdims=True)
        s_new = jnp.where(has_new, s_new, MASKED)
        m = jnp.maximum(jnp.max(s, axis=-1, keepdims=True), s_new)
        e = jnp.where(key_ok, jnp.exp(s - m), 0.0)
        e_new = jnp.where(has_new, jnp.exp(s_new - m), 0.0)
        den = jnp.maximum(jnp.sum(e, axis=-1, keepdims=True) + e_new, 1e-30)
        o = (_dot_nt(e.astype(BF16), vt) + e_new * new_slc[:, K_LANES:]) / den
        o_slc.append(o[:, g * HEAD_DIM:(g + 1) * HEAD_DIM])
        s = _dot(qpb, win_k)
        e = jnp.exp(s - jnp.max(s, axis=-1, keepdims=True))
        o = _dot_nt(e.astype(BF16), win_v) / jnp.sum(e, axis=-1, keepdims=True)
        o_win.append(o[:, g * HEAD_DIM:(g + 1) * HEAD_DIM])
    gates = gate_ref[0]
    o = (gates[:, 0:1] * ocmp_ref[0] + gates[:, 1:2] * jnp.concatenate(o_slc, axis=0)
         + gates[:, 2:3] * jnp.concatenate(o_win, axis=0))
    ms = jnp.sum(jnp.sum(o * o, axis=-1, keepdims=True), axis=0, keepdims=True) / (n_heads * HEAD_DIM)
    out_ref[0] = o * lax.rsqrt(ms + NORM_EPS) * gob_ref[...]


def _attend_sample(q, gates, o_cmp, idx, cache_slc_t, page_table, new_slc, win_t, new_win, w, past_blocks):
    db, b_width = q.shape
    n_heads = b_width // HEAD_DIM
    n_sel = min(N_SELECT, past_blocks + 1)
    sub_per_page = PAGE_SIZE // SLC_BLOCK
    jp = jnp.clip(idx[:, :, :n_sel], 0, past_blocks - 1)
    page = jnp.take_along_axis(page_table, (jp // sub_per_page).reshape(db, -1), axis=1).astype(jnp.int32)
    n_keys = n_sel * PAGE_SIZE
    expand = (jnp.arange(n_keys, dtype=jnp.int32)[None, :] // PAGE_SIZE
              == jnp.arange(LANES, dtype=jnp.int32)[:, None]).astype(F32)
    gob = w["g_out_b"].reshape(n_heads, HEAD_DIM)
    wb = win_t.shape[2]
    per_seq = lambda shape: pl.BlockSpec((1,) + shape, lambda b, pg: (b, 0, 0))
    page_spec = lambda i: pl.BlockSpec((1, KV_LANES, PAGE_SIZE), lambda b, pg, i=i: (pg[b, i], 0, 0))
    grid_spec = pltpu.PrefetchScalarGridSpec(
        num_scalar_prefetch=1, grid=(db,),
        in_specs=[page_spec(i) for i in range(N_KV * n_sel)]
        + [per_seq((1, b_width)), per_seq((n_heads, 3)), per_seq((n_heads, HEAD_DIM)), per_seq((N_KV, LANES)),
           per_seq((1, KV_LANES)), per_seq((KV_LANES, wb)), per_seq((KV_LANES, 1)),
           pl.BlockSpec(expand.shape, lambda b, pg: (0, 0)), pl.BlockSpec(gob.shape, lambda b, pg: (0, 0))],
        out_specs=[per_seq((n_heads, HEAD_DIM)), per_seq((KV_LANES, wb))])
    return pl.pallas_call(
        functools.partial(_attend_sample_kernel, n_sel=n_sel, past_blocks=past_blocks, n_heads=n_heads),
        grid_spec=grid_spec,
        out_shape=[jax.ShapeDtypeStruct((db, n_heads, HEAD_DIM), F32),
                   jax.ShapeDtypeStruct((db, KV_LANES, wb), F32)],
        compiler_params=_params("parallel"), name="attend_sample",
    )(page, *([cache_slc_t] * (N_KV * n_sel)), q[:, None, :], gates[:, :n_heads * 3].reshape(db, n_heads, 3), o_cmp,
      idx, new_slc[:, None, :], win_t, new_win[:, :, None], expand, gob)


def _ffn_kernel(x_ref, ma_ref, mb_ref, woa_ref, wob_ref, gffn_ref, wgate_ref, wup_ref, wdown_ref, gfin_ref, y_ref,
                acc_scr, xn_scr):
    c = pl.program_id(1)

    @pl.when(c == 0)
    def _():
        x = x_ref[...] + _dot(ma_ref[...], woa_ref[...]) + _dot(mb_ref[...], wob_ref[...])
        acc_scr[...] = x
        xn_scr[...] = _rms(x, gffn_ref[...]).astype(BF16)

    xn = xn_scr[...]
    hid = jax.nn.silu(_dot(xn, wgate_ref[...])) * _dot(xn, wup_ref[...])
    acc_scr[...] += _dot(hid.astype(BF16), wdown_ref[...])

    @pl.when(c == pl.num_programs(1) - 1)
    def _():
        y_ref[...] = _rms(acc_scr[...], gfin_ref[...])


def _output_ffn(x2, mix_a, mix_b, w, tm):
    n, d = x2.shape
    d_ff = w["w_down"].shape[0]
    ff_chunks = FFN_CHUNKS
    step = d_ff // ff_chunks
    assert n % tm == 0 and d_ff % ff_chunks == 0 and step % LANES == 0
    row = lambda i, c: (i, 0)
    fixed = lambda a: pl.BlockSpec(a.shape, lambda i, c: (0, 0))
    return pl.pallas_call(
        _ffn_kernel,
        grid=(n // tm, ff_chunks),
        in_specs=[pl.BlockSpec((tm, d), row), pl.BlockSpec((tm, mix_a.shape[1]), row),
                  pl.BlockSpec((tm, mix_b.shape[1]), row), fixed(w["wo_a"]), fixed(w["wo_b"]), fixed(w["g_ffn"]),
                  pl.BlockSpec((d, step), lambda i, c: (0, c)), pl.BlockSpec((d, step), lambda i, c: (0, c)),
                  pl.BlockSpec((step, d), lambda i, c: (c, 0)), fixed(w["g_final"])],
        out_specs=pl.BlockSpec((tm, d), row),
        out_shape=jax.ShapeDtypeStruct((n, d), F32),
        scratch_shapes=[pltpu.VMEM((tm, d), F32), pltpu.VMEM((tm, d), BF16)],
        compiler_params=_params("parallel", "arbitrary"), name="output_ffn",
    )(x2, mix_a, mix_b, w["wo_a"], w["wo_b"], w["g_ffn"], w["w_gate"], w["w_up"], w["w_down"], w["g_final"])


def _prepare_weights(l, g_attn, w_in, g_sgu, w_s, b_s, w_c1, b_c1, w_c2, g_out_a, g_out_b, w_out,
                     g_ffn, w_gate_up, w_down, g_final):
    a_width = g_sgu.shape[1]
    b_width = g_out_b.shape[1]
    n_heads = b_width // HEAD_DIM
    d_ff = w_down.shape[1]
    hid = b_c1.shape[2]
    o1 = 2 * a_width
    o2 = o1 + b_width
    o3 = o2 + 3 * KV_LANES
    wi = w_in[l]
    wg = jnp.pad(wi[:, o3:], ((0, 0), (0, LANES - 3 * n_heads)))
    row = lambda a: a.reshape(1, -1)
    w1 = w_c1[l].reshape(2, 2, CMP_STRIDE, HEAD_DIM, hid)
    w1 = jnp.transpose(w1, (0, 2, 3, 1, 4)).reshape(2, CMP_STRIDE, HEAD_DIM, 2 * hid)
    eye = jnp.eye(N_KV, dtype=F32)
    w1 = jnp.einsum("gh,csdn->csgdhn", eye, w1).reshape(2, CMP_STRIDE // CMP_STACK, CMP_STACK * K_LANES, N_KV * 2 * hid)
    w2 = jnp.einsum("gh,cne->cgnhe", eye, w_c2[l]).reshape(2, N_KV * hid, K_LANES)
    return {
        "g_attn": row(g_attn[l]), "wuv": wi[:, :o1].astype(BF16), "wq": wi[:, o1:o2].astype(BF16),
        "wkv": wi[:, o2:o3].astype(BF16), "wg": wg.astype(BF16),
        "g_sgu": row(g_sgu[l]), "g_out_a": row(g_out_a[l]), "g_out_b": row(g_out_b[l]),
        "w_s": w_s[l], "bs_full": jnp.repeat(b_s[l].T, LANES, axis=1),
        "ws0": row(jnp.repeat(w_s[l][:, 0, 0], LANES)), "bs0": row(jnp.repeat(b_s[l][:, 0], LANES)),
        "w1k": w1[0].astype(BF16), "w1v": w1[1].astype(BF16), "b_c1": b_c1[l], "w2": w2.astype(BF16),
        "wo_a": w_out[l][:a_width].astype(BF16), "wo_b": w_out[l][a_width:].astype(BF16),
        "g_ffn": row(g_ffn[l]), "w_gate": w_gate_up[l][:, :d_ff].astype(BF16),
        "w_up": w_gate_up[l][:, d_ff:].astype(BF16), "w_down": w_down[l].astype(BF16), "g_final": row(g_final),
    }


def kernel(x_prompt, x_sample, cache_cmp_kv, cache_slc_kv, state_win_kv, page_table, g_attn, w_in, g_sgu, w_s, b_s,
           w_c1, b_c1, w_c2, g_out_a, g_out_b, w_out, g_ffn, w_gate_up, w_down, g_final):
    depth = w_in.shape[0]
    b, t, d = x_prompt.shape
    db, t_s, _ = x_sample.shape
    assert depth == 1 and t_s == 1
    n_pages = page_table.shape[1]
    past = n_pages * PAGE_SIZE
    wb = state_win_kv.shape[2]
    assert wb == WINDOW and past % SLC_BLOCK == 0
    l = 0
    w = _prepare_weights(l, g_attn, w_in, g_sgu, w_s, b_s, w_c1, b_c1, w_c2, g_out_a, g_out_b, w_out,
                         g_ffn, w_gate_up, w_down, g_final)
    xp = x_prompt.reshape(b * t, d)
    mix_a, q, kvc, gates, kvc_t, kvs_t, kvw_t, kts, vs, ktw, vw = _in_projection(
        xp, jnp.arange(t, dtype=jnp.int32), w, prompt_shape=(b, t))
    kck_t, kcv = _compress_prompt(kvc.reshape(b, t, KV_LANES), w)
    mix_b = _nsa_prompt(q, gates, kck_t, kcv, kts, vs, ktw, vw, w, b, t)
    y_prompt = _output_ffn(xp, mix_a, mix_b, w, tm=min(FFN_ROWS, b * t)).reshape(b, t, d)

    xs = x_sample.reshape(db, d)
    pos_s = past + jnp.zeros((db,), jnp.int32)
    mix_a_s, q_s, kvc_s, kvs_s, kvw_s, gates_s, v_rows = _in_projection(xs, pos_s, w)
    kc_s = _compress_sample(_feature_major(cache_cmp_kv[l]), page_table, kvc_s, w)
    lp = -(-(past + t_s) // SLC_BLOCK) * SLC_BLOCK
    n_cmp_s = lp // CMP_STRIDE - 1
    n_blocks_s = (n_cmp_s + 1) * CMP_STRIDE // SLC_BLOCK
    o_cmp_s, idx_s = _select_sample(q_s, kc_s, past, n_cmp_s, n_blocks_s)
    mix_b_s, win_new_t = _attend_sample(q_s, gates_s, o_cmp_s, idx_s, _feature_major(cache_slc_kv[l]), page_table,
                                        kvs_s, _feature_major(state_win_kv[l]), kvw_s, w, past // SLC_BLOCK)
    y_sample = _output_ffn(xs, mix_a_s, mix_b_s.reshape(db, -1).astype(BF16), w, tm=db).reshape(db, t_s, d)

    kv_shape = (2, N_KV, HEAD_DIM)
    return (y_prompt, y_sample,
            _row_major(kvc_t)[None], _row_major(kvs_t)[None], _row_major(kvw_t[:, :, t - min(WINDOW, t):])[None],
            kvc_s.reshape(1, db, t_s, *kv_shape), kvs_s.reshape(1, db, t_s, *kv_shape),
            _row_major(win_new_t)[None], v_rows.reshape(1, db, t_s, -1))


def _feature_major(kv):
    n, rows = kv.shape[:2]
    return jnp.transpose(kv, (0, 2, 3, 4, 1)).reshape(n, KV_LANES, rows)


def _row_major(kv_t):
    n, _, rows = kv_t.shape
    return jnp.transpose(kv_t.reshape(n, 2, N_KV, HEAD_DIM, rows), (0, 4, 1, 2, 3))
```

```python
import functools

import jax
import jax.numpy as jnp
from jax import lax
from jax.experimental import pallas as pl
from jax.experimental.pallas import tpu as pltpu

F32 = jnp.float32
BF16 = jnp.bfloat16

A_GROUPS = 4
CHUNK = 128
HEAD_DIM = 64
N_KV = 2
ROT_DIM = HEAD_DIM // 4
ROPE_THETA = 500000.0
CMP_LEN = 32
CMP_STRIDE = 16
SLC_BLOCK = 64
N_SELECT = 16
WINDOW = 512
Q_BLOCK = 128
FORCE_BONUS = 1000.0
PAGE_SIZE = 128
NORM_EPS = 1e-6
MASKED = -1e30
LOG2_E = 1.4426950408889634
SEL_MARGIN = 160.0

LANES = 128
SUBLANES = 8
VMEM_LIMIT_BYTES = 56 * 1024 * 1024

PROJ_ROWS = 512
FFN_ROWS = 512
FFN_CHUNKS = 2
CMP_CHUNK_PAGES = 32

KV_LANES = 2 * N_KV * HEAD_DIM
K_LANES = N_KV * HEAD_DIM
CMP_STACK = 2
SLC_TILE = 512
TILES_PER_TRIP = 4
WIN_KEYS = WINDOW + Q_BLOCK


def _rms(x, g):
    return x * lax.rsqrt(jnp.mean(x * x, axis=-1, keepdims=True) + NORM_EPS) * g


def _dot(a, b):
    return jnp.dot(a, b, preferred_element_type=F32)


def _dot_nt(a, b, precision=None):
    return lax.dot_general(a, b, (((1,), (1,)), ((), ())), precision=precision,
                           preferred_element_type=F32)


def _rope(z, rc, rs1, rs2):
    return z * rc + pltpu.roll(z, LANES - ROT_DIM // 2, 1) * rs1 + pltpu.roll(z, ROT_DIM // 2, 1) * rs2


def _project(x_ref, gattn_ref, wuv_ref, wq_ref, wkv_ref, wg_ref, rc_ref, rs1_ref, rs2_ref):
    xn = _rms(x_ref[...], gattn_ref[...]).astype(BF16)
    rc, rs1, rs2 = rc_ref[...], rs1_ref[...], rs2_ref[...]
    zuv = _dot(xn, wuv_ref[...])
    a_width = zuv.shape[1] // 2
    zq = _dot(xn, wq_ref[...])
    q = jnp.concatenate([_rope(zq[:, i * LANES:(i + 1) * LANES], rc, rs1, rs2)
                         for i in range(zq.shape[1] // LANES)], axis=1)
    zkv = _dot(xn, wkv_ref[...])
    branches = []
    for br in range(3):
        k = _rope(zkv[:, br * KV_LANES:br * KV_LANES + K_LANES], rc, rs1, rs2)
        v = zkv[:, br * KV_LANES + K_LANES:(br + 1) * KV_LANES]
        branches.append((k, v))
    gates = jax.nn.sigmoid(_dot(xn, wg_ref[...]))
    return zuv[:, :a_width], zuv[:, a_width:], q, branches, gates


def _gmlp_norm_v(v, gsgu):
    v = jax.nn.gelu(v)
    return jnp.concatenate([_rms(v[:, g * LANES:(g + 1) * LANES], gsgu[:, g * LANES:(g + 1) * LANES])
                            for g in range(A_GROUPS)], axis=1)


def _inproj_prompt_kernel(x_ref, gattn_ref, wuv_ref, wq_ref, wkv_ref, wg_ref, rc_ref, rs1_ref, rs2_ref,
                          ws_ref, bs_ref, gsgu_ref, goa_ref,
                          mixa_ref, q_ref, kvc_ref, gate_ref, kvct_ref, kvst_ref, kvwt_ref,
                          kts_ref, vs_ref, ktw_ref, vw_ref):
    u, v, q, branches, gates = _project(x_ref, gattn_ref, wuv_ref, wq_ref, wkv_ref, wg_ref,
                                        rc_ref, rs1_ref, rs2_ref)
    tm = u.shape[0]
    u = jax.nn.gelu(u)
    vg = _gmlp_norm_v(v, gsgu_ref[...]).astype(BF16)
    row = lax.broadcasted_iota(jnp.int32, (CHUNK, CHUNK), 0)
    col = lax.broadcasted_iota(jnp.int32, (CHUNK, CHUNK), 1)
    bias = bs_ref[...]
    parts = []
    for g in range(A_GROUPS):
        w = jnp.where(row >= col, ws_ref[g], 0.0).astype(BF16)
        s = jnp.concatenate(
            [_dot(w, vg[c * CHUNK:(c + 1) * CHUNK, g * LANES:(g + 1) * LANES]) for c in range(tm // CHUNK)],
            axis=0)
        s = s + jnp.concatenate([bias[:, g * LANES:(g + 1) * LANES]] * (tm // CHUNK), axis=0)
        parts.append(u[:, g * LANES:(g + 1) * LANES] * s)
    mixa_ref[...] = _rms(jnp.concatenate(parts, axis=1), goa_ref[...]).astype(BF16)
    q_ref[...] = (q * (LOG2_E * HEAD_DIM ** -0.5)).astype(BF16)
    gate_ref[...] = gates
    kvc_ref[...] = jnp.concatenate(branches[0], axis=1)
    kts = []
    for ref, (k, v_) in zip((kvct_ref, kvst_ref, kvwt_ref), branches):
        kt = k.T
        ref[0, :K_LANES, :] = kt
        ref[0, K_LANES:, :] = v_.T
        kts.append(kt)
    own = [lax.broadcasted_iota(jnp.int32, (1, K_LANES), 1) // HEAD_DIM == g for g in range(N_KV)]
    for kt_ref, va_ref, br in ((kts_ref, vs_ref, 1), (ktw_ref, vw_ref, 2)):
        kt_ref[0] = kts[br].astype(BF16)
        va_ref[...] = jnp.concatenate([jnp.where(m, branches[br][1], 1.0) for m in own], axis=1).astype(BF16)


def _inproj_sample_kernel(x_ref, gattn_ref, wuv_ref, wq_ref, wkv_ref, wg_ref, rc_ref, rs1_ref, rs2_ref,
                          ws0_ref, bs0_ref, gsgu_ref, goa_ref,
                          mixa_ref, q_ref, kvc_ref, kvs_ref, kvw_ref, gate_ref, vrow_ref):
    u, v, q, branches, gates = _project(x_ref, gattn_ref, wuv_ref, wq_ref, wkv_ref, wg_ref,
                                        rc_ref, rs1_ref, rs2_ref)
    vg = _gmlp_norm_v(v, gsgu_ref[...])
    o_a = jax.nn.gelu(u) * (vg * ws0_ref[...] + bs0_ref[...])
    mixa_ref[...] = _rms(o_a, goa_ref[...]).astype(BF16)
    vrow_ref[...] = vg
    q_ref[...] = q
    for ref, (k, v_) in zip((kvc_ref, kvs_ref, kvw_ref), branches):
        ref[...] = jnp.concatenate([k, v_], axis=1)
    gate_ref[...] = gates


def _full(shape):
    return pl.BlockSpec(shape, lambda *_: (0,) * len(shape))


def _params(*sem):
    return pltpu.CompilerParams(dimension_semantics=sem, vmem_limit_bytes=VMEM_LIMIT_BYTES)


def _in_projection(x2, pos, w, *, prompt_shape=None):
    n, d = x2.shape
    rc, rs1, rs2 = _rope_tables(pos)
    a_width = w["wuv"].shape[1] // 2
    b_width = w["wq"].shape[1]
    weights = [w["g_attn"], w["wuv"], w["wq"], w["wkv"], w["wg"]]
    wspecs = [_full(a.shape) for a in weights]
    tail = [w["g_sgu"], w["g_out_a"]]
    if prompt_shape is None:
        tm, grid = n, (1,)
        row = lambda i: (i, 0)
        rope_map = row
        gm = [w["ws0"], w["bs0"]]
    else:
        b, t = prompt_shape
        tm = PROJ_ROWS
        assert t % tm == 0 and tm % CHUNK == 0
        tpb = t // tm
        grid = (b * tpb,)
        row = lambda i: (i, 0)
        rope_map = lambda i: (i % tpb, 0)
        gm = [w["w_s"], w["bs_full"]]
    rspec = pl.BlockSpec((tm, LANES), rope_map)
    in_specs = ([pl.BlockSpec((tm, d), row)] + wspecs + [rspec] * 3
                + [_full(a.shape) for a in gm] + [_full(a.shape) for a in tail])
    if prompt_shape is None:
        kern = _inproj_sample_kernel
        outs = [((n, a_width), BF16), ((n, b_width), F32), ((n, KV_LANES), F32), ((n, KV_LANES), F32),
                ((n, KV_LANES), F32), ((n, LANES), F32), ((n, a_width), F32)]
        out_specs = [pl.BlockSpec((tm, s[1]), row) for s, _ in outs]
    else:
        kern = _inproj_prompt_kernel
        kt_map = lambda i: (i // tpb, 0, i % tpb)
        outs = [((n, a_width), BF16), ((n, b_width), BF16), ((n, KV_LANES), F32), ((n, LANES), F32)]
        out_specs = [pl.BlockSpec((tm, s[1]), row) for s, _ in outs]
        outs += [((b, KV_LANES, t), F32)] * 3
        out_specs += [pl.BlockSpec((1, KV_LANES, tm), kt_map)] * 3
        outs += [((b, K_LANES, t), BF16), ((n, N_KV * K_LANES), BF16)] * 2
        out_specs += [pl.BlockSpec((1, K_LANES, tm), kt_map), pl.BlockSpec((tm, N_KV * K_LANES), row)] * 2
    return pl.pallas_call(
        kern, grid=grid, in_specs=in_specs, out_specs=out_specs,
        out_shape=[jax.ShapeDtypeStruct(s, dt) for s, dt in outs],
        compiler_params=_params("parallel"), name="in_projection",
    )(x2, *weights, rc, rs1, rs2, *gm, *tail)


def _rope_tables(pos):
    half = ROT_DIM // 2
    inv = ROPE_THETA ** (-jnp.arange(half, dtype=F32) / half)
    ang = pos.astype(F32)[:, None] * inv[None, :]
    cos, sin = jnp.cos(ang), jnp.sin(ang)
    n = pos.shape[0]
    rest0 = jnp.zeros((n, HEAD_DIM - ROT_DIM), F32)
    zero = jnp.zeros((n, half), F32)
    rc = jnp.concatenate([cos, cos, rest0 + 1.0], axis=1)
    rs1 = jnp.concatenate([-sin, zero, rest0], axis=1)
    rs2 = jnp.concatenate([zero, sin, rest0], axis=1)
    return tuple(jnp.tile(a, (1, LANES // HEAD_DIM)) for a in (rc, rs1, rs2))


def _compress_partial(read_k, read_v, w1k_ref, w1v_ref):
    acc_k = acc_v = None
    stack = w1k_ref.shape[1] // K_LANES
    for i in range(CMP_STRIDE // stack):
        rows = range(i * stack, (i + 1) * stack)
        pk = _dot(jnp.concatenate([read_k(s).astype(BF16) for s in rows], axis=1), w1k_ref[i])
        pv = _dot(jnp.concatenate([read_v(s).astype(BF16) for s in rows], axis=1), w1v_ref[i])
        acc_k = pk if acc_k is None else acc_k + pk
        acc_v = pv if acc_v is None else acc_v + pv
    return acc_k, acc_v


def _compress_finish(fs_k, fs_v, b1_ref, w2_ref):
    hid = b1_ref.shape[1]
    outs = []
    for c, fs in enumerate((fs_k, fs_v)):
        hs = []
        for g in range(N_KV):
            first = fs[:, g * 2 * hid:g * 2 * hid + hid]
            second = fs[:, g * 2 * hid + hid:(g + 1) * 2 * hid]
            nxt = pltpu.roll(second, second.shape[0] - 1, 0)
            hs.append(jax.nn.silu(first + nxt + b1_ref[c:c + 1, :]))
        outs.append(_dot(jnp.concatenate(hs, axis=1).astype(BF16), w2_ref[c]))
    return jnp.concatenate(outs, axis=1)


def _compress_prompt_kernel(k_ref, v_ref, w1k_ref, w1v_ref, b1_ref, w2_ref, kck_ref, kcv_ref, kc_scr):
    nb = kcv_ref.shape[1]
    fs_k, fs_v = _compress_partial(lambda s: k_ref[0, pl.ds(s, nb, stride=CMP_STRIDE), :],
                                   lambda s: v_ref[0, pl.ds(s, nb, stride=CMP_STRIDE), :], w1k_ref, w1v_ref)
    kc_scr[...] = _compress_finish(fs_k, fs_v, b1_ref, w2_ref)
    kck_ref[0] = kc_scr[:, :K_LANES].T.astype(BF16)
    kcv_ref[0] = kc_scr[:, K_LANES:].astype(BF16)


def _compress_prompt(kvc, w):
    b, t, _ = kvc.shape
    nb = t // CMP_STRIDE
    weights = [w["w1k"], w["w1v"], w["b_c1"], w["w2"]]
    return pl.pallas_call(
        _compress_prompt_kernel, grid=(b,),
        in_specs=[pl.BlockSpec((1, t, K_LANES), lambda i: (i, 0, 0)), pl.BlockSpec((1, t, K_LANES), lambda i: (i, 0, 1))]
        + [_full(a.shape) for a in weights],
        out_specs=[pl.BlockSpec((1, K_LANES, nb), lambda i: (i, 0, 0)), pl.BlockSpec((1, nb, K_LANES), lambda i: (i, 0, 0))],
        out_shape=[jax.ShapeDtypeStruct((b, K_LANES, nb), BF16), jax.ShapeDtypeStruct((b, nb, K_LANES), BF16)],
        scratch_shapes=[pltpu.VMEM((nb, KV_LANES), F32)],
        compiler_params=_params("parallel"), name="compress_prompt",
    )(kvc, kvc, *weights)


def _stack_heads(q, g, q_per_kv):
    return jnp.concatenate([q[:, (g * q_per_kv + h) * HEAD_DIM:(g * q_per_kv + h + 1) * HEAD_DIM]
                            for h in range(q_per_kv)], axis=0)


def _select_blocks(score, n_sel):
    rows, n = score.shape
    lane = lax.broadcasted_iota(jnp.int32, (rows, n), 1).astype(F32)
    sel = jnp.zeros((rows, n), F32)
    picks = []
    x = score
    for _ in range(n_sel):
        m = jnp.max(x, axis=-1, keepdims=True)
        idx = jnp.min(jnp.where(x == m, lane, float(n)), axis=-1, keepdims=True)
        hit = lane == idx
        ok = m > 0.1 * MASKED
        sel = jnp.where(hit & ok, 1.0, sel)
        x = jnp.where(hit, -3e38, x)
        picks.append((idx, ok))
    return sel, picks


def _select_mask_t(xt, n_sel):
    n, cols = xt.shape
    tiles = [xt[t * SUBLANES:(t + 1) * SUBLANES] for t in range(n // SUBLANES)]
    row = lax.broadcasted_iota(jnp.int32, (SUBLANES, cols), 0)
    ahead = [jnp.zeros((SUBLANES, cols), F32) for _ in tiles]
    for i in range(n):
        xi = xt[i:i + 1, :]
        for t, x in enumerate(tiles):
            first, last = t * SUBLANES, (t + 1) * SUBLANES - 1
            if first > i:
                inc = jnp.where(xi >= x, 1.0, 0.0)
            elif last <= i:
                inc = jnp.where(xi > x, 1.0, 0.0)
            else:
                inc = jnp.where(row + first > i, jnp.where(xi >= x, 1.0, 0.0), jnp.where(xi > x, 1.0, 0.0))
            ahead[t] = ahead[t] + inc
    ahead = jnp.concatenate(ahead, axis=0)
    return jnp.where((ahead < n_sel) & (xt > 0.1 * MASKED), 1.0, 0.0)


def _block_scores(p_slc, blk_t, n_blocks, axis=1):
    sj = lax.broadcasted_iota(jnp.int32, p_slc.shape, axis)
    causal = (sj <= blk_t) & (sj < n_blocks)
    forced = causal & ((sj == 0) | (sj >= blk_t - 1))
    score = jnp.where(forced, p_slc + FORCE_BONUS, p_slc)
    return jnp.where(causal, score, MASKED)


def _nsa_prompt_kernel(q_ref, gate_ref, kck_ref, kcv_ref, kts_ref, vsa_ref, ktw_ref, vwa_ref, ovl_ref, exp_ref,
                       gsel_ref, gob_ref, out_ref, s_scr, mx_scr, acc_scr, kmax_scr, *, n_cmp, n_heads):
    blk = pl.program_id(1)
    start = blk * Q_BLOCK
    q_per_kv = n_heads // N_KV
    q = q_ref[...]
    ncp = kcv_ref.shape[1]
    n_blocks = ovl_ref.shape[0]
    tpos = start + lax.broadcasted_iota(jnp.int32, (Q_BLOCK, 1), 0)
    cn = lax.broadcasted_iota(jnp.int32, (1, ncp), 1)
    cmp_mask = ((cn * CMP_STRIDE + CMP_LEN - 1 <= tpos) & (cn < n_cmp)).astype(F32)
    cmp_mask = jnp.concatenate([cmp_mask] * (N_KV * q_per_kv), axis=0) > 0.5
    grp_lanes = [slice(g * HEAD_DIM, (g + 1) * HEAD_DIM) for g in range(N_KV)]
    qbs = [_stack_heads(q, g, q_per_kv) for g in range(N_KV)]

    @pl.when(blk == 0)
    def _():
        for g in range(N_KV):
            k_abs = jnp.abs(kts_ref[0, grp_lanes[g], :].astype(F32))
            k_max = jnp.max(jnp.max(k_abs, axis=0, keepdims=True), axis=-1, keepdims=True)
            kmax_scr[g] = jnp.broadcast_to(k_max, kmax_scr.shape[1:])

    def normalised(acc, g):
        return (acc / pltpu.roll(acc, HEAD_DIM, 1))[:, grp_lanes[g]]

    o_cmp, p_slc = [], []
    ovl = ovl_ref[...]
    grp_rows = [slice(g * q_per_kv * Q_BLOCK, (g + 1) * q_per_kv * Q_BLOCK) for g in range(N_KV)]
    s = jnp.concatenate([_dot(qbs[g], kck_ref[0, grp_lanes[g], :]) for g in range(N_KV)], axis=0)
    s = jnp.where(cmp_mask, s, MASKED)
    e = jnp.where(cmp_mask, jnp.exp2(s - jnp.max(s, axis=-1, keepdims=True)), 0.0)
    p_all = e / jnp.maximum(jnp.sum(e, axis=-1, keepdims=True), 1e-30)
    for g in range(N_KV):
        p = p_all[grp_rows[g]]
        o_cmp.append(_dot(p.astype(BF16), kcv_ref[0, :, grp_lanes[g]]))
        p_sum = p[:Q_BLOCK]
        for h in range(1, q_per_kv):
            p_sum = p_sum + p[h * Q_BLOCK:(h + 1) * Q_BLOCK]
        p_hi = p_sum.astype(BF16)
        p_lo = (p_sum - p_hi.astype(F32)).astype(BF16)
        p_slc.append(_dot_nt(ovl, p_hi) + _dot_nt(ovl, p_lo))
    blk_t = (start + lax.broadcasted_iota(jnp.int32, (1, Q_BLOCK), 1)) // SLC_BLOCK
    score = _block_scores(jnp.concatenate(p_slc, axis=1), jnp.concatenate([blk_t] * N_KV, axis=1), n_blocks, axis=0)
    sel = _select_mask_t(score, min(N_SELECT, n_blocks))
    sel = jnp.concatenate([sel, jnp.zeros((LANES - n_blocks, N_KV * Q_BLOCK), F32)], axis=0)
    q_aug = []
    for g in range(N_KV):
        q_max = jnp.max(jnp.max(jnp.abs(qbs[g].astype(F32)), axis=0, keepdims=True), axis=-1, keepdims=True)
        bonus = jnp.exp2(jnp.ceil(jnp.log2(2.0 * HEAD_DIM * q_max * kmax_scr[g][:1, :1] + SEL_MARGIN)))
        sel_g = sel[:, g * Q_BLOCK:(g + 1) * Q_BLOCK].T[:, :n_blocks] * bonus
        q_aug.append(jnp.concatenate([qbs[g], jnp.concatenate([sel_g] * q_per_kv, axis=0).astype(BF16)], axis=1))

    ws = pl.multiple_of(jnp.maximum(start - WINDOW, 0), LANES)
    dpos = tpos - (ws + lax.broadcasted_iota(jnp.int32, (1, WIN_KEYS), 1))
    win_bias = jnp.where((dpos >= 0) & (dpos < WINDOW), 0.0, MASKED)
    win_bias = jnp.concatenate([win_bias] * (N_KV * q_per_kv), axis=0)
    s = jnp.concatenate([_dot(qbs[g], ktw_ref[0, grp_lanes[g], pl.ds(ws, WIN_KEYS)]) for g in range(N_KV)], axis=0)
    s = s + win_bias
    p_all = jnp.exp2(s - jnp.max(s, axis=-1, keepdims=True)).astype(BF16)
    o_win = [normalised(_dot(p_all[grp_rows[g]], vwa_ref[pl.ds(ws, WIN_KEYS), g * K_LANES:(g + 1) * K_LANES]), g)
             for g in range(N_KV)]

    last = (start + Q_BLOCK - 1) // SLC_TILE
    lane_tiles = SLC_TILE // LANES
    mx_scr[...] = jnp.full(mx_scr.shape, MASKED, F32)
    acc_scr[...] = jnp.zeros(acc_scr.shape, F32)

    def score_tile(kt, masked):
        off = pl.multiple_of(kt * SLC_TILE, SLC_TILE)
        for g in range(N_KV):
            keys = jnp.concatenate([kts_ref[0, grp_lanes[g], pl.ds(off, SLC_TILE)],
                                    exp_ref[:, pl.ds(off, SLC_TILE)]], axis=0)
            s = _dot(q_aug[g], keys)
            if masked:
                row = lax.broadcasted_iota(jnp.int32, (q_per_kv * Q_BLOCK, 1), 0) % Q_BLOCK
                s = jnp.where(off + lax.broadcasted_iota(jnp.int32, (1, SLC_TILE), 1) <= start + row, s, MASKED)
            s_scr[g, :, pl.ds(off, SLC_TILE)] = s
            m = s[:, :LANES]
            for i in range(1, lane_tiles):
                m = jnp.maximum(m, s[:, i * LANES:(i + 1) * LANES])
            mx_scr[g] = jnp.maximum(mx_scr[g], m)

    def pair_loop(n, tile):
        def trip(i, carry):
            for k in range(TILES_PER_TRIP):
                tile(TILES_PER_TRIP * i + k)
            return carry

        lax.fori_loop(0, n // TILES_PER_TRIP, trip, 0)
        done = n // TILES_PER_TRIP * TILES_PER_TRIP
        size = TILES_PER_TRIP // 2
        while size:
            @pl.when((n - done) & size != 0)
            def _(done=done, size=size):
                for k in range(size):
                    tile(done + k)

            done = done + ((n - done) & size)
            size //= 2

    pair_loop(last, lambda kt: score_tile(kt, False))
    score_tile(last, True)
    for g in range(N_KV):
        mx_scr[g] = jnp.broadcast_to(jnp.max(mx_scr[g], axis=-1, keepdims=True), mx_scr.shape[1:])

    def value_tile(kt):
        off = pl.multiple_of(kt * SLC_TILE, SLC_TILE)
        for g in range(N_KV):
            p = jnp.exp2(s_scr[g, :, pl.ds(off, SLC_TILE)] - jnp.concatenate([mx_scr[g]] * lane_tiles, axis=1))
            acc_scr[g] += _dot(p.astype(BF16), vsa_ref[pl.ds(off, SLC_TILE), g * K_LANES:(g + 1) * K_LANES])

    pair_loop(last + 1, value_tile)
    o_slc = [normalised(acc_scr[g], g) for g in range(N_KV)]

    gates = gate_ref[...]
    g_hi = gates.astype(BF16)
    g_lo = (gates - g_hi.astype(F32)).astype(BF16)
    out = None
    for j, branch in enumerate((o_cmp, o_slc, o_win)):
        spread = _dot(g_hi, gsel_ref[j]) + _dot(g_lo, gsel_ref[j])
        o = jnp.concatenate([branch[g][h * Q_BLOCK:(h + 1) * Q_BLOCK] for g in range(N_KV) for h in range(q_per_kv)],
                            axis=1)
        out = spread * o if out is None else out + spread * o
    out_ref[...] = _rms(out, gob_ref[...]).astype(BF16)


def _nsa_prompt(q, gates, kck_t, kcv, kts, vsa, ktw, vwa, w, b, t):
    n, b_width = q.shape
    n_heads = b_width // HEAD_DIM
    assert t % SLC_TILE == 0 and t >= WIN_KEYS
    nqb = t // Q_BLOCK
    ncp = kcv.shape[1]
    n_cmp = ncp - 1
    n_blocks = (n_cmp + 1) * CMP_STRIDE // SLC_BLOCK
    assert n_blocks % SUBLANES == 0 and n_blocks <= LANES
    ovl = _overlap_matrix(ncp, n_blocks, n_blocks).T.astype(BF16)
    key_blk = jnp.arange(t, dtype=jnp.int32)[None, :] // SLC_BLOCK
    expand = (key_blk == jnp.arange(n_blocks, dtype=jnp.int32)[:, None]).astype(BF16)
    col = jnp.arange(LANES, dtype=jnp.int32)[None, :, None]
    head = jnp.arange(b_width, dtype=jnp.int32)[None, None, :] // HEAD_DIM
    gate_sel = (col == head * 3 + jnp.arange(3, dtype=jnp.int32)[:, None, None]).astype(BF16)
    tok = lambda i, j: (i * nqb + j, 0)
    seq3 = lambda i, j: (i, 0, 0)
    seq2 = lambda i, j: (i, 0)
    rows = (n_heads // N_KV) * Q_BLOCK
    return pl.pallas_call(
        functools.partial(_nsa_prompt_kernel, n_cmp=n_cmp, n_heads=n_heads),
        grid=(b, nqb),
        in_specs=[pl.BlockSpec((Q_BLOCK, b_width), tok), pl.BlockSpec((Q_BLOCK, LANES), tok),
                  pl.BlockSpec((1, K_LANES, ncp), seq3), pl.BlockSpec((1, ncp, K_LANES), seq3),
                  pl.BlockSpec((1, K_LANES, t), seq3), pl.BlockSpec((t, N_KV * K_LANES), seq2),
                  pl.BlockSpec((1, K_LANES, t), seq3), pl.BlockSpec((t, N_KV * K_LANES), seq2),
                  _full(ovl.shape), _full(expand.shape), _full(gate_sel.shape), _full(w["g_out_b"].shape)],
        out_specs=pl.BlockSpec((Q_BLOCK, b_width), tok),
        out_shape=jax.ShapeDtypeStruct((n, b_width), BF16),
        scratch_shapes=[pltpu.VMEM((N_KV, rows, t), F32), pltpu.VMEM((N_KV, rows, LANES), F32),
                        pltpu.VMEM((N_KV, rows, K_LANES), F32), pltpu.VMEM((N_KV, SUBLANES, LANES), F32)],
        compiler_params=_params("parallel", "arbitrary"), name="nsa_prompt",
    )(q, gates, kck_t, kcv, kts, vsa, ktw, vwa, ovl, expand, gate_sel, w["g_out_b"])


def _overlap_matrix(rows, cols, n_blocks):
    ci = jnp.arange(rows, dtype=jnp.int32)[:, None]
    sj = jnp.arange(cols, dtype=jnp.int32)[None, :]
    hit = (ci * CMP_STRIDE < (sj + 1) * SLC_BLOCK) & (ci * CMP_STRIDE + CMP_LEN > sj * SLC_BLOCK) & (sj < n_blocks)
    return hit.astype(F32)


def _compress_sample_kernel(pt_ref, cache_ref, new_ref, w1k_ref, w1v_ref, b1_ref, w2_ref, kc_ref,
                            fsk_scr, fsv_scr, buf, sem, *chunk_scr, n_pages, chunk_pages):
    seq = pl.program_id(0)
    n_chunks = n_pages // chunk_pages
    blocks_per_page = PAGE_SIZE // CMP_STRIDE
    chunk_blocks = chunk_pages * blocks_per_page

    def page_copy(s, c, i):
        return pltpu.make_async_copy(cache_ref.at[pt_ref[s, c * chunk_pages + i]], buf.at[c % 2, i], sem.at[c % 2])

    def start_chunk(s, c):
        for i in range(chunk_pages):
            page_copy(s, c, i).start()

    def wait_chunk(c):
        for i in range(chunk_pages):
            page_copy(seq, c, i).wait()

    def scratch(c):
        return chunk_scr[2 * (c % 2)], chunk_scr[2 * (c % 2) + 1]

    def transpose_chunk(c):
        xk_scr, xv_scr = scratch(c)
        for i in range(chunk_pages):
            xk_scr[pl.ds(i * PAGE_SIZE, PAGE_SIZE), :] = buf[c % 2, i, :K_LANES, :].T
            xv_scr[pl.ds(i * PAGE_SIZE, PAGE_SIZE), :] = buf[c % 2, i, K_LANES:, :].T

    def project_chunk(c):
        xk_scr, xv_scr = scratch(c)
        fs_k, fs_v = _compress_partial(lambda s: xk_scr[pl.ds(s, chunk_blocks, stride=CMP_STRIDE), :],
                                       lambda s: xv_scr[pl.ds(s, chunk_blocks, stride=CMP_STRIDE), :],
                                       w1k_ref, w1v_ref)
        fsk_scr[pl.ds(c * chunk_blocks, chunk_blocks), :] = fs_k
        fsv_scr[pl.ds(c * chunk_blocks, chunk_blocks), :] = fs_v

    @pl.when(seq == 0)
    def _():
        start_chunk(seq, 0)
        start_chunk(seq, 1)

    for c in range(n_chunks):
        wait_chunk(c)
        transpose_chunk(c)
        if c + 2 < n_chunks:
            start_chunk(seq, c + 2)
        if c >= 1:
            project_chunk(c - 1)

    @pl.when(seq + 1 < pl.num_programs(0))
    def _():
        start_chunk(seq + 1, 0)
        start_chunk(seq + 1, 1)

    project_chunk(n_chunks - 1)
    past_blocks = n_chunks * chunk_blocks
    tail = fsk_scr.shape[0] - past_blocks
    new = new_ref[0]
    is_first = lax.broadcasted_iota(jnp.int32, (tail, 1), 0) == 0
    nk = _dot(new[:, :K_LANES].astype(BF16), w1k_ref[0, :K_LANES, :])
    nv = _dot(new[:, K_LANES:].astype(BF16), w1v_ref[0, :K_LANES, :])
    fsk_scr[pl.ds(past_blocks, tail), :] = jnp.where(is_first, nk, 0.0)
    fsv_scr[pl.ds(past_blocks, tail), :] = jnp.where(is_first, nv, 0.0)
    kc_ref[0] = _compress_finish(fsk_scr[...], fsv_scr[...], b1_ref, w2_ref)


def _compress_sample(cache_cmp_t, page_table, new_rows, w):
    db, n_pages = page_table.shape
    chunk_pages = CMP_CHUNK_PAGES
    assert n_pages % chunk_pages == 0 and n_pages // chunk_pages >= 2
    blocks_per_page = PAGE_SIZE // CMP_STRIDE
    past_blocks = n_pages * blocks_per_page
    nbp = past_blocks + SUBLANES
    weights = [w["w1k"], w["w1v"], w["b_c1"], w["w2"]]
    hid2 = w["w1k"].shape[2]
    chunk_rows = chunk_pages * PAGE_SIZE
    grid_spec = pltpu.PrefetchScalarGridSpec(
        num_scalar_prefetch=1, grid=(db,),
        in_specs=[pl.BlockSpec(memory_space=pl.ANY), pl.BlockSpec((1, 1, KV_LANES), lambda b, pt: (b, 0, 0))]
        + [pl.BlockSpec(a.shape, lambda b, pt, nd=a.ndim: (0,) * nd) for a in weights],
        out_specs=pl.BlockSpec((1, nbp, KV_LANES), lambda b, pt: (b, 0, 0)),
        scratch_shapes=[pltpu.VMEM((nbp, hid2), F32), pltpu.VMEM((nbp, hid2), F32),
                        pltpu.VMEM((2, chunk_pages, KV_LANES, PAGE_SIZE), F32), pltpu.SemaphoreType.DMA((2,))]
        + [pltpu.VMEM((chunk_rows, K_LANES), F32)] * 4)
    return pl.pallas_call(
        functools.partial(_compress_sample_kernel, n_pages=n_pages, chunk_pages=chunk_pages),
        grid_spec=grid_spec, out_shape=jax.ShapeDtypeStruct((db, nbp, KV_LANES), F32),
        compiler_params=_params("arbitrary"), name="compress_sample",
    )(page_table, cache_cmp_t, new_rows[:, None, :], *weights)


def _select_sample_kernel(q_ref, kc_ref, ocmp_ref, psum_ref, *, pos, n_cmp, n_heads):
    q_per_kv = n_heads // N_KV
    q = q_ref[0] * (LOG2_E * HEAD_DIM ** -0.5)
    kc = kc_ref[0]
    ncp = kc.shape[0]
    cn = lax.broadcasted_iota(jnp.int32, (1, ncp), 1)
    mask = (cn * CMP_STRIDE + CMP_LEN - 1 <= pos) & (cn < n_cmp)
    o_rows, p_rows = [], []
    for g in range(N_KV):
        qg = _stack_heads(q, g, q_per_kv).astype(BF16)
        s = jnp.where(mask, _dot_nt(qg, kc[:, g * HEAD_DIM:(g + 1) * HEAD_DIM].astype(BF16)), MASKED)
        e = jnp.where(mask, jnp.exp2(s - jnp.max(s, axis=-1, keepdims=True)), 0.0)
        p = e / jnp.maximum(jnp.sum(e, axis=-1, keepdims=True), 1e-30)
        o_rows.append(_dot(p.astype(BF16), kc[:, K_LANES + g * HEAD_DIM:K_LANES + (g + 1) * HEAD_DIM].astype(BF16)))
        p_rows.append(jnp.sum(p, axis=0, keepdims=True))
    ocmp_ref[0] = jnp.concatenate(o_rows, axis=0)
    psum_ref[0] = jnp.concatenate(p_rows, axis=0)


def _pick_sample_kernel(psum_ref, ovl_ref, idx_ref, *, pos, n_blocks):
    p = psum_ref[...]
    p_hi = p.astype(BF16)
    p_lo = (p - p_hi.astype(F32)).astype(BF16)
    ovl = ovl_ref[...]
    score = _block_scores(_dot(p_hi, ovl) + _dot(p_lo, ovl), pos // SLC_BLOCK, n_blocks)
    _, picks = _select_blocks(score, min(N_SELECT, n_blocks))
    lane = lax.broadcasted_iota(jnp.int32, idx_ref.shape, 1)
    out = jnp.full(idx_ref.shape, -1.0, F32)
    for i, (idx, ok) in enumerate(picks):
        out = jnp.where((lane == i) & ok, idx, out)
    idx_ref[...] = out.astype(jnp.int32)


def _select_sample(q, kc, pos, n_cmp, n_blocks):
    db, b_width = q.shape
    n_heads = b_width // HEAD_DIM
    ncp = kc.shape[1]
    o_cmp, p_sum = pl.pallas_call(
        functools.partial(_select_sample_kernel, pos=pos, n_cmp=n_cmp, n_heads=n_heads),
        grid=(db,),
        in_specs=[pl.BlockSpec((1, 1, b_width), lambda i: (i, 0, 0)),
                  pl.BlockSpec((1, ncp, KV_LANES), lambda i: (i, 0, 0))],
        out_specs=[pl.BlockSpec((1, n_heads, HEAD_DIM), lambda i: (i, 0, 0)),
                   pl.BlockSpec((1, N_KV, ncp), lambda i: (i, 0, 0))],
        out_shape=[jax.ShapeDtypeStruct((db, n_heads, HEAD_DIM), F32),
                   jax.ShapeDtypeStruct((db, N_KV, ncp), F32)],
        compiler_params=_params("parallel"), name="select_sample",
    )(q[:, None, :], kc)
    nsp = -(-n_blocks // LANES) * LANES
    ovl = _overlap_matrix(ncp, nsp, n_blocks).astype(BF16)
    idx = pl.pallas_call(
        functools.partial(_pick_sample_kernel, pos=pos, n_blocks=n_blocks),
        grid=(1,),
        in_specs=[_full((db * N_KV, ncp)), _full(ovl.shape)],
        out_specs=_full((db * N_KV, LANES)),
        out_shape=jax.ShapeDtypeStruct((db * N_KV, LANES), jnp.int32),
        compiler_params=_params("arbitrary"), name="pick_sample",
    )(p_sum.reshape(db * N_KV, ncp), ovl)
    return o_cmp, idx.reshape(db, N_KV, LANES)


def _attend_sample_kernel(page_ref, *refs, n_sel, past_blocks, n_heads):
    del page_ref
    n_slots = N_KV * n_sel
    pages = refs[:n_slots]
    (q_ref, gate_ref, ocmp_ref, idx_ref, newslc_ref, win_ref, newwin_ref, exp_ref, gob_ref,
     out_ref, winout_ref) = refs[n_slots:]
    q_per_kv = n_heads // N_KV
    q = q_ref[0] * (HEAD_DIM ** -0.5)
    lane = lax.broadcasted_iota(jnp.int32, (1, K_LANES), 1)
    wb = win_ref.shape[2]
    is_last = lax.broadcasted_iota(jnp.int32, (1, wb), 1) == wb - 1
    win = jnp.where(is_last, newwin_ref[0], pltpu.roll(win_ref[0], wb - 1, 1))
    winout_ref[0] = win
    win_k = win[:K_LANES].astype(BF16)
    win_v = win[K_LANES:].astype(BF16)
    new_slc = newslc_ref[0]
    idx = idx_ref[0].astype(F32)
    n_keys = n_sel * PAGE_SIZE
    key_half = (lax.broadcasted_iota(jnp.int32, (1, n_keys), 1) % PAGE_SIZE) // SLC_BLOCK
    o_slc, o_win = [], []
    for g in range(N_KV):
        qpad = jnp.concatenate(
            [jnp.where(lane // HEAD_DIM == g,
                       jnp.concatenate([q[:, (g * q_per_kv + h) * HEAD_DIM:(g * q_per_kv + h + 1) * HEAD_DIM]] * N_KV,
                                       axis=1), 0.0)
             for h in range(q_per_kv)], axis=0)
        qpb = qpad.astype(BF16)
        kt = jnp.concatenate([pages[g * n_sel + i][0, :K_LANES, :] for i in range(n_sel)], axis=1).astype(BF16)
        vt = jnp.concatenate([pages[g * n_sel + i][0, K_LANES:, :] for i in range(n_sel)], axis=1).astype(BF16)
        s = _dot(qpb, kt)
        idg = idx[g:g + 1, :]
        idk = jnp.dot(idg, exp_ref[...], precision=lax.Precision.HIGHEST, preferred_element_type=F32)
        parity = idk - 2.0 * jnp.floor(idk * 0.5)
        key_ok = (idk >= 0.0) & (idk < past_blocks) & (parity == key_half.astype(F32))
        s = jnp.where(key_ok, s, MASKED)
        has_new = jnp.max(jnp.where(idg == past_blocks, 1.0, 0.0), axis=-1, keepdims=True) > 0.5
        s_new = jnp.sum(qpad * new_slc[:, :K_LANES], axis=-1, keepdims=True)
        s_new = jnp.where(has_new, s_new, MASKED)
        m = jnp.maximum(jnp.max(s, axis=-1, keepdims=True), s_new)
        e = jnp.where(key_ok, jnp.exp(s - m), 0.0)
        e_new = jnp.where(has_new, jnp.exp(s_new - m), 0.0)
        den = jnp.maximum(jnp.sum(e, axis=-1, keepdims=True) + e_new, 1e-30)
        o = (_dot_nt(e.astype(BF16), vt) + e_new * new_slc[:, K_LANES:]) / den
        o_slc.append(o[:, g * HEAD_DIM:(g + 1) * HEAD_DIM])
        s = _dot(qpb, win_k)
        e = jnp.exp(s - jnp.max(s, axis=-1, keepdims=True))
        o = _dot_nt(e.astype(BF16), win_v) / jnp.sum(e, axis=-1, keepdims=True)
        o_win.append(o[:, g * HEAD_DIM:(g + 1) * HEAD_DIM])
    gates = gate_ref[0]
    o = (gates[:, 0:1] * ocmp_ref[0] + gates[:, 1:2] * jnp.concatenate(o_slc, axis=0)
         + gates[:, 2:3] * jnp.concatenate(o_win, axis=0))
    ms = jnp.sum(jnp.sum(o * o, axis=-1, keepdims=True), axis=0, keepdims=True) / (n_heads * HEAD_DIM)
    out_ref[0] = o * lax.rsqrt(ms + NORM_EPS) * gob_ref[...]


def _attend_sample(q, gates, o_cmp, idx, cache_slc_t, page_table, new_slc, win_t, new_win, w, past_blocks):
    db, b_width = q.shape
    n_heads = b_width // HEAD_DIM
    n_sel = min(N_SELECT, past_blocks + 1)
    sub_per_page = PAGE_SIZE // SLC_BLOCK
    jp = jnp.clip(idx[:, :, :n_sel], 0, past_blocks - 1)
    page = jnp.take_along_axis(page_table, (jp // sub_per_page).reshape(db, -1), axis=1).astype(jnp.int32)
    n_keys = n_sel * PAGE_SIZE
    expand = (jnp.arange(n_keys, dtype=jnp.int32)[None, :] // PAGE_SIZE
              == jnp.arange(LANES, dtype=jnp.int32)[:, None]).astype(F32)
    gob = w["g_out_b"].reshape(n_heads, HEAD_DIM)
    wb = win_t.shape[2]
    per_seq = lambda shape: pl.BlockSpec((1,) + shape, lambda b, pg: (b, 0, 0))
    page_spec = lambda i: pl.BlockSpec((1, KV_LANES, PAGE_SIZE), lambda b, pg, i=i: (pg[b, i], 0, 0))
    grid_spec = pltpu.PrefetchScalarGridSpec(
        num_scalar_prefetch=1, grid=(db,),
        in_specs=[page_spec(i) for i in range(N_KV * n_sel)]
        + [per_seq((1, b_width)), per_seq((n_heads, 3)), per_seq((n_heads, HEAD_DIM)), per_seq((N_KV, LANES)),
           per_seq((1, KV_LANES)), per_seq((KV_LANES, wb)), per_seq((KV_LANES, 1)),
           pl.BlockSpec(expand.shape, lambda b, pg: (0, 0)), pl.BlockSpec(gob.shape, lambda b, pg: (0, 0))],
        out_specs=[per_seq((n_heads, HEAD_DIM)), per_seq((KV_LANES, wb))])
    return pl.pallas_call(
        functools.partial(_attend_sample_kernel, n_sel=n_sel, past_blocks=past_blocks, n_heads=n_heads),
        grid_spec=grid_spec,
        out_shape=[jax.ShapeDtypeStruct((db, n_heads, HEAD_DIM), F32),
                   jax.ShapeDtypeStruct((db, KV_LANES, wb), F32)],
        compiler_params=_params("parallel"), name="attend_sample",
    )(page, *([cache_slc_t] * (N_KV * n_sel)), q[:, None, :], gates[:, :n_heads * 3].reshape(db, n_heads, 3), o_cmp,
      idx, new_slc[:, None, :], win_t, new_win[:, :, None], expand, gob)


def _ffn_kernel(x_ref, ma_ref, mb_ref, woa_ref, wob_ref, gffn_ref, wgate_ref, wup_ref, wdown_ref, gfin_ref, y_ref,
                acc_scr, xn_scr):
    c = pl.program_id(1)

    @pl.when(c == 0)
    def _():
        x = x_ref[...] + _dot(ma_ref[...], woa_ref[...]) + _dot(mb_ref[...], wob_ref[...])
        acc_scr[...] = x
        xn_scr[...] = _rms(x, gffn_ref[...]).astype(BF16)

    xn = xn_scr[...]
    hid = jax.nn.silu(_dot(xn, wgate_ref[...])) * _dot(xn, wup_ref[...])
    acc_scr[...] += _dot(hid.astype(BF16), wdown_ref[...])

    @pl.when(c == pl.num_programs(1) - 1)
    def _():
        y_ref[...] = _rms(acc_scr[...], gfin_ref[...])


def _output_ffn(x2, mix_a, mix_b, w, tm):
    n, d = x2.shape
    d_ff = w["w_down"].shape[0]
    ff_chunks = FFN_CHUNKS
    step = d_ff // ff_chunks
    assert n % tm == 0 and d_ff % ff_chunks == 0 and step % LANES == 0
    row = lambda i, c: (i, 0)
    fixed = lambda a: pl.BlockSpec(a.shape, lambda i, c: (0, 0))
    return pl.pallas_call(
        _ffn_kernel,
        grid=(n // tm, ff_chunks),
        in_specs=[pl.BlockSpec((tm, d), row), pl.BlockSpec((tm, mix_a.shape[1]), row),
                  pl.BlockSpec((tm, mix_b.shape[1]), row), fixed(w["wo_a"]), fixed(w["wo_b"]), fixed(w["g_ffn"]),
                  pl.BlockSpec((d, step), lambda i, c: (0, c)), pl.BlockSpec((d, step), lambda i, c: (0, c)),
                  pl.BlockSpec((step, d), lambda i, c: (c, 0)), fixed(w["g_final"])],
        out_specs=pl.BlockSpec((tm, d), row),
        out_shape=jax.ShapeDtypeStruct((n, d), F32),
        scratch_shapes=[pltpu.VMEM((tm, d), F32), pltpu.VMEM((tm, d), BF16)],
        compiler_params=_params("parallel", "arbitrary"), name="output_ffn",
    )(x2, mix_a, mix_b, w["wo_a"], w["wo_b"], w["g_ffn"], w["w_gate"], w["w_up"], w["w_down"], w["g_final"])


def _prepare_weights(l, g_attn, w_in, g_sgu, w_s, b_s, w_c1, b_c1, w_c2, g_out_a, g_out_b, w_out,
                     g_ffn, w_gate_up, w_down, g_final):
    a_width = g_sgu.shape[1]
    b_width = g_out_b.shape[1]
    n_heads = b_width // HEAD_DIM
    d_ff = w_down.shape[1]
    hid = b_c1.shape[2]
    o1 = 2 * a_width
    o2 = o1 + b_width
    o3 = o2 + 3 * KV_LANES
    wi = w_in[l]
    wg = jnp.pad(wi[:, o3:], ((0, 0), (0, LANES - 3 * n_heads)))
    row = lambda a: a.reshape(1, -1)
    w1 = w_c1[l].reshape(2, 2, CMP_STRIDE, HEAD_DIM, hid)
    w1 = jnp.transpose(w1, (0, 2, 3, 1, 4)).reshape(2, CMP_STRIDE, HEAD_DIM, 2 * hid)
    eye = jnp.eye(N_KV, dtype=F32)
    w1 = jnp.einsum("gh,csdn->csgdhn", eye, w1).reshape(2, CMP_STRIDE // CMP_STACK, CMP_STACK * K_LANES, N_KV * 2 * hid)
    w2 = jnp.einsum("gh,cne->cgnhe", eye, w_c2[l]).reshape(2, N_KV * hid, K_LANES)
    return {
        "g_attn": row(g_attn[l]), "wuv": wi[:, :o1].astype(BF16), "wq": wi[:, o1:o2].astype(BF16),
        "wkv": wi[:, o2:o3].astype(BF16), "wg": wg.astype(BF16),
        "g_sgu": row(g_sgu[l]), "g_out_a": row(g_out_a[l]), "g_out_b": row(g_out_b[l]),
        "w_s": w_s[l], "bs_full": jnp.repeat(b_s[l].T, LANES, axis=1),
        "ws0": row(jnp.repeat(w_s[l][:, 0, 0], LANES)), "bs0": row(jnp.repeat(b_s[l][:, 0], LANES)),
        "w1k": w1[0].astype(BF16), "w1v": w1[1].astype(BF16), "b_c1": b_c1[l], "w2": w2.astype(BF16),
        "wo_a": w_out[l][:a_width].astype(BF16), "wo_b": w_out[l][a_width:].astype(BF16),
        "g_ffn": row(g_ffn[l]), "w_gate": w_gate_up[l][:, :d_ff].astype(BF16),
        "w_up": w_gate_up[l][:, d_ff:].astype(BF16), "w_down": w_down[l].astype(BF16), "g_final": row(g_final),
    }


def kernel(x_prompt, x_sample, cache_cmp_kv, cache_slc_kv, state_win_kv, page_table, g_attn, w_in, g_sgu, w_s, b_s,
           w_c1, b_c1, w_c2, g_out_a, g_out_b, w_out, g_ffn, w_gate_up, w_down, g_final):
    depth = w_in.shape[0]
    b, t, d = x_prompt.shape
    db, t_s, _ = x_sample.shape
    assert depth == 1 and t_s == 1
    n_pages = page_table.shape[1]
    past = n_pages * PAGE_SIZE
    wb = state_win_kv.shape[2]
    assert wb == WINDOW and past % SLC_BLOCK == 0
    l = 0
    w = _prepare_weights(l, g_attn, w_in, g_sgu, w_s, b_s, w_c1, b_c1, w_c2, g_out_a, g_out_b, w_out,
                         g_ffn, w_gate_up, w_down, g_final)
    xp = x_prompt.reshape(b * t, d)
    mix_a, q, kvc, gates, kvc_t, kvs_t, kvw_t, kts, vs, ktw, vw = _in_projection(
        xp, jnp.arange(t, dtype=jnp.int32), w, prompt_shape=(b, t))
    kck_t, kcv = _compress_prompt(kvc.reshape(b, t, KV_LANES), w)
    mix_b = _nsa_prompt(q, gates, kck_t, kcv, kts, vs, ktw, vw, w, b, t)
    y_prompt = _output_ffn(xp, mix_a, mix_b, w, tm=min(FFN_ROWS, b * t)).reshape(b, t, d)

    xs = x_sample.reshape(db, d)
    pos_s = past + jnp.zeros((db,), jnp.int32)
    mix_a_s, q_s, kvc_s, kvs_s, kvw_s, gates_s, v_rows = _in_projection(xs, pos_s, w)
    kc_s = _compress_sample(_feature_major(cache_cmp_kv[l]), page_table, kvc_s, w)
    lp = -(-(past + t_s) // SLC_BLOCK) * SLC_BLOCK
    n_cmp_s = lp // CMP_STRIDE - 1
    n_blocks_s = (n_cmp_s + 1) * CMP_STRIDE // SLC_BLOCK
    o_cmp_s, idx_s = _select_sample(q_s, kc_s, past, n_cmp_s, n_blocks_s)
    mix_b_s, win_new_t = _attend_sample(q_s, gates_s, o_cmp_s, idx_s, _feature_major(cache_slc_kv[l]), page_table,
                                        kvs_s, _feature_major(state_win_kv[l]), kvw_s, w, past // SLC_BLOCK)
    y_sample = _output_ffn(xs, mix_a_s, mix_b_s.reshape(db, -1).astype(BF16), w, tm=db).reshape(db, t_s, d)

    kv_shape = (2, N_KV, HEAD_DIM)
    return (y_prompt, y_sample,
            _row_major(kvc_t)[None], _row_major(kvs_t)[None], _row_major(kvw_t[:, :, t - min(WINDOW, t):])[None],
            kvc_s.reshape(1, db, t_s, *kv_shape), kvs_s.reshape(1, db, t_s, *kv_shape),
            _row_major(win_new_t)[None], v_rows.reshape(1, db, t_s, -1))


def _feature_major(kv):
    n, rows = kv.shape[:2]
    return jnp.transpose(kv, (0, 2, 3, 4, 1)).reshape(n, KV_LANES, rows)


def _row_major(kv_t):
    n, _, rows = kv_t.shape
    return jnp.transpose(kv_t.reshape(n, 2, N_KV, HEAD_DIM, rows), (0, 4, 1, 2, 3))
```
